```python
import math
import jax, jax.numpy as jnp
from jax import lax
import numpy as np

D_MODEL = 1024
BATCH = 2
SEQ = 8192
DEPTH = 4
DEC_BATCH = 128
DEC_SEQ = 8
PAST_LEN = 2048
PAGE_SIZE = 128

N_MIXERS = 4
N_SSD = (DEPTH + 3) // N_MIXERS
N_RWKV = (DEPTH + 2) // N_MIXERS
N_MOBA = (DEPTH + 1) // N_MIXERS
N_MLSTM = DEPTH // N_MIXERS
NORM_EPS = 1e-6
N_MOD = 9
D_FF = 2816

SSD_D_INNER = 2 * D_MODEL
SSD_HEADDIM = 64
SSD_HEADS = SSD_D_INNER // SSD_HEADDIM
SSD_GROUPS = 4
SSD_HPG = SSD_HEADS // SSD_GROUPS
SSD_STATE = 128
SSD_CONV = 4
SSD_CHUNK = 128
SSD_CONV_DIM = SSD_D_INNER + 2 * SSD_GROUPS * SSD_STATE
SSD_IN = SSD_D_INNER + SSD_CONV_DIM + SSD_HEADS

RW_HEADDIM = 64
RW_HEADS = D_MODEL // RW_HEADDIM
RW_DECAY_LORA = 64
RW_AAA_LORA = 64
RW_GATE_LORA = 128
RW_LNX_EPS = 64e-5

MB_HEADS = 16
MB_HEADDIM = D_MODEL // MB_HEADS
MB_BLOCK = 256
MB_TOPK = 3
MB_QBLOCK = 16
ROPE_DIM = MB_HEADDIM // 4
ROPE_THETA = 500000.0

ML_HEADS = 8
ML_QK_DIM = D_MODEL // 2 // ML_HEADS
ML_V_DIM = D_MODEL // ML_HEADS
ML_CHUNK = 64
ML_IN = 2 * ML_HEADS * ML_QK_DIM + 2 * ML_HEADS * ML_V_DIM + 2 * ML_HEADS

kernel_name = "hybrid_ssd_rwkv7_moba_mlstm_macaron_adaln_step"


def rmsnorm(x, w):
    xf = x.astype(jnp.float32)
    y = xf * lax.rsqrt(jnp.mean(xf * xf, axis=-1, keepdims=True) + NORM_EPS)
    return (y * w).astype(x.dtype)


def sublayer_in(x, g, shift, scale):
    return rmsnorm(x, g) * (1 + scale) + shift


def adaln_params(c, ada_w, ada_b):
    mod = jax.nn.silu(c) @ ada_w + ada_b
    return mod.reshape(c.shape[0], N_MOD, 1, D_MODEL)


def swiglu(h, w_in, w_out):
    g, u = jnp.split(h @ w_in, 2, axis=-1)
    return (jax.nn.silu(g) * u) @ w_out


def macaron_ffn(x, mod, m0, g, w_in, w_out):
    h = sublayer_in(x, g, mod[:, m0], mod[:, m0 + 1])
    return x + 0.5 * mod[:, m0 + 2] * swiglu(h, w_in, w_out)


def causal_dwconv(x, buf, w, b):
    L, K = x.shape[1], w.shape[0]
    xp = jnp.concatenate([buf.astype(x.dtype), x], axis=1)
    y = b + sum(xp[:, k:k + L] * w[k] for k in range(K))
    return y, xp[:, -(K - 1):]


def ssd_scan(x, dt, a, bm, cm, s0):
    B, L = x.shape[0], x.shape[1]
    Q = math.gcd(L, SSD_CHUNK)
    nc = L // Q
    chunks = lambda t: jnp.moveaxis(t.reshape(B, nc, Q, *t.shape[2:]), 1, 0)
    causal = jnp.tril(jnp.ones((Q, Q), bool))[None, :, :, None, None]

    def step(s, inp):
        xc, dtc, bc, cc = inp
        xf, bf, cf = xc.astype(jnp.float32), bc.astype(jnp.float32), cc.astype(jnp.float32)
        cs = jnp.cumsum(dtc * a, axis=1)
        decay = jnp.exp(jnp.where(causal, cs[:, :, None] - cs[:, None], -jnp.inf))
        cb = jnp.einsum('btgn,bsgn->btsg', cf, bf)
        wgt = cb[..., None] * decay * dtc[:, None]
        y = jnp.einsum('btsgr,bsgrp->btgrp', wgt, xf)
        y = y + jnp.einsum('btgn,bgrpn->btgrp', cf, s) * jnp.exp(cs)[..., None]
        w_end = jnp.exp(cs[:, -1:] - cs) * dtc
        s_new = s * jnp.exp(cs[:, -1])[..., None, None] + jnp.einsum('bsgr,bsgn,bsgrp->bgrpn', w_end, bf, xf)
        return s_new, y

    s_fin, y = lax.scan(step, s0, (chunks(x), chunks(dt), chunks(bm), chunks(cm)))
    y = jnp.moveaxis(y, 0, 1).reshape(x.shape)
    return y, s_fin


def ssd_mixer(h, conv_buf, ssm0, w_in, conv_w, conv_b, dt_bias, a_log, d_skip, norm_w, w_out):
    B, L, _ = h.shape
    G, R, P, N = SSD_GROUPS, SSD_HPG, SSD_HEADDIM, SSD_STATE
    zxbcdt = h @ w_in
    z = zxbcdt[..., :SSD_D_INNER]
    xbc = zxbcdt[..., SSD_D_INNER:SSD_D_INNER + SSD_CONV_DIM]
    dt_raw = zxbcdt[..., SSD_D_INNER + SSD_CONV_DIM:]
    xbc, new_conv = causal_dwconv(xbc, conv_buf, conv_w, conv_b)
    xbc = jax.nn.silu(xbc)
    xs = xbc[..., :SSD_D_INNER].reshape(B, L, G, R, P)
    bm = xbc[..., SSD_D_INNER:SSD_D_INNER + G * N].reshape(B, L, G, N)
    cm = xbc[..., SSD_D_INNER + G * N:].reshape(B, L, G, N)
    dt = jax.nn.softplus(dt_raw.astype(jnp.float32) + dt_bias).reshape(B, L, G, R)
    a = -jnp.exp(a_log.astype(jnp.float32)).reshape(G, R)
    y, s_fin = ssd_scan(xs, dt, a, bm, cm, ssm0.astype(jnp.float32).reshape(B, G, R, P, N))
    y = y + xs.astype(jnp.float32) * d_skip.reshape(G, R, 1)
    yz = y.reshape(B, L, SSD_D_INNER) * jax.nn.silu(z.astype(jnp.float32))
    yz = rmsnorm(yz.reshape(B, L, G, -1), norm_w.reshape(G, -1)).reshape(B, L, SSD_D_INNER)
    return yz.astype(h.dtype) @ w_out, new_conv, s_fin.reshape(B, SSD_HEADS, P, N)


def rwkv7_mixer(h, shift_buf, wkv0, mu, w_r, w_k, w_v, w0, w1, w2, a0, a1, a2, g1, g2,
                k_k, k_a, r_k, lnx_w, lnx_b, w_o):
    B, L, D = h.shape
    H, N = RW_HEADS, RW_HEADDIM
    prev = jnp.concatenate([shift_buf[:, None, :].astype(h.dtype), h[:, :-1]], axis=1)
    xx = prev - h
    xr, xw, xk = h + xx * mu[0], h + xx * mu[1], h + xx * mu[2]
    xv, xa, xg = h + xx * mu[3], h + xx * mu[4], h + xx * mu[5]
    r, k, v = xr @ w_r, xk @ w_k, xv @ w_v
    w_raw = (w0 + jnp.tanh(xw @ w1) @ w2).astype(jnp.float32)
    decay = jnp.exp(-jnp.exp(-jax.nn.softplus(-w_raw) - 0.5))
    a = jax.nn.sigmoid(a0 + (xa @ a1) @ a2)
    g = jax.nn.sigmoid(xg @ g1) @ g2
    heads = lambda t: t.reshape(B, L, H, N).astype(jnp.float32)
    kk = heads(k * k_k)
    kk = kk / jnp.maximum(jnp.sqrt(jnp.sum(kk * kk, axis=-1, keepdims=True)), 1e-12)
    k = k * (1 + (a - 1) * k_a)
    rh, kh, vh, ah, wh = heads(r), heads(k), heads(v), heads(a), heads(decay)
    tm = lambda t: jnp.moveaxis(t, 1, 0)

    def step(S, inp):
        r_t, w_t, k_t, v_t, kk_t, a_t = inp
        sa = jnp.einsum('bhvk,bhk->bhv', S, -kk_t)
        S = S * w_t[:, :, None, :] + sa[..., None] * (kk_t * a_t)[:, :, None, :] + v_t[..., None] * k_t[:, :, None, :]
        return S, jnp.einsum('bhvk,bhk->bhv', S, r_t)

    S_fin, y = lax.scan(step, wkv0.astype(jnp.float32), (tm(rh), tm(wh), tm(kh), tm(vh), tm(kk), tm(ah)))
    y = jnp.moveaxis(y, 0, 1)
    mean = jnp.mean(y, axis=-1, keepdims=True)
    var = jnp.mean(jnp.square(y - mean), axis=-1, keepdims=True)
    y = ((y - mean) * lax.rsqrt(var + RW_LNX_EPS)).reshape(B, L, D) * lnx_w + lnx_b
    bonus = jnp.sum(rh * kh * r_k, axis=-1, keepdims=True) * vh
    y = (y + bonus.reshape(B, L, D)).astype(h.dtype)
    return (y * g) @ w_o, h[:, -1], S_fin


def rope_partial(x, pos):
    half = ROPE_DIM // 2
    inv = ROPE_THETA ** (-jnp.arange(half, dtype=jnp.float32) / half)
    ang = pos.astype(jnp.float32)[:, None] * inv
    cos, sin = jnp.cos(ang)[:, None, :], jnp.sin(ang)[:, None, :]
    x1 = x[..., :half].astype(jnp.float32)
    x2 = x[..., half:ROPE_DIM].astype(jnp.float32)
    rot = jnp.concatenate([x1 * cos - x2 * sin, x2 * cos + x1 * sin], axis=-1).astype(x.dtype)
    return jnp.concatenate([rot, x[..., ROPE_DIM:]], axis=-1)


def moba_qkv(h, pos, w_qkv, q_norm, k_norm):
    B, L, _ = h.shape
    qkv = (h @ w_qkv).reshape(B, L, 3, MB_HEADS, MB_HEADDIM)
    q = rope_partial(rmsnorm(qkv[:, :, 0], q_norm), pos)
    k = rope_partial(rmsnorm(qkv[:, :, 1], k_norm), pos)
    return q, k, qkv[:, :, 2]


def moba_attend_one(qpos, q, k_all, v_all):
    Lq, Lk = q.shape[0], k_all.shape[0]
    H, Dh, K = MB_HEADS, MB_HEADDIM, MB_TOPK
    nb = max(-(-Lk // MB_BLOCK), K)
    pad = nb * MB_BLOCK - Lk
    blocks = lambda t: jnp.pad(t, ((0, pad), (0, 0), (0, 0))).reshape(nb, MB_BLOCK, H, Dh).transpose(2, 0, 1, 3)
    kb, vb = blocks(k_all), blocks(v_all)
    kmean = jnp.mean(kb.astype(jnp.float32), axis=2)
    cur = qpos // MB_BLOCK
    gate = jnp.einsum('qhd,hnd->hqn', q.astype(jnp.float32), kmean)
    fully_past = jnp.arange(nb)[None, :] < cur[:, None]
    gate = jnp.where(fully_past[None], gate, -jnp.inf)
    _, top = lax.top_k(gate, K)
    sel_ok = jnp.arange(K)[None, :] < cur[:, None]
    idx = jnp.concatenate([top, jnp.broadcast_to(cur[None, :, None], (H, Lq, 1))], axis=-1)
    qb = math.gcd(Lq, MB_QBLOCK)
    nq = Lq // qb
    hidx = jnp.arange(H)[:, None, None]
    scale = MB_HEADDIM ** -0.5

    def one_block(args):
        qs, ps, ids, ok = args
        kg, vg = kb[hidx, ids], vb[hidx, ids]
        s = jnp.einsum('qhd,hqsbd->hqsb', qs, kg).astype(jnp.float32) * scale
        kpos = ids[..., None] * MB_BLOCK + jnp.arange(MB_BLOCK)
        valid = jnp.concatenate([ok, jnp.ones((qb, 1), bool)], axis=-1)
        mask = valid[None, :, :, None] & (kpos <= ps[None, :, None, None])
        s = jnp.where(mask, s, -jnp.inf)
        p = jax.nn.softmax(s.reshape(H, qb, -1), axis=-1).reshape(s.shape)
        return jnp.einsum('hqsb,hqsbd->qhd', p.astype(vg.dtype), vg)

    out = lax.map(one_block, (q.reshape(nq, qb, H, Dh), qpos.reshape(nq, qb),
                              idx.reshape(H, nq, qb, K + 1).transpose(1, 0, 2, 3), sel_ok.reshape(nq, qb, K)))
    return out.reshape(Lq, H, Dh)


def moba_sample_one(qpos, ck, cv, args):
    q, k_new, v_new, pt = args
    k_all = jnp.concatenate([ck[pt].reshape(-1, MB_HEADS, MB_HEADDIM).astype(k_new.dtype), k_new], axis=0)
    v_all = jnp.concatenate([cv[pt].reshape(-1, MB_HEADS, MB_HEADDIM).astype(v_new.dtype), v_new], axis=0)
    return moba_attend_one(qpos, q, k_all, v_all)


def mlstm_chunked(q, k, v, i_pre, log_f, c0, n0, m0):
    B, L = q.shape[0], q.shape[1]
    Q = math.gcd(L, ML_CHUNK)
    nc = L // Q
    chunks = lambda t: jnp.moveaxis(t.reshape(B, nc, Q, *t.shape[2:]), 1, 0)
    causal = jnp.tril(jnp.ones((Q, Q), bool))[None, :, :, None]

    def step(carry, inp):
        C, n, m = carry
        qc, kc, vc, ic, fc = inp
        b = jnp.cumsum(fc, axis=1)
        dmat = jnp.where(causal, b[:, :, None] - b[:, None] + ic[:, None], -jnp.inf)
        m_inter = b + m[:, None]
        m_t = jnp.maximum(m_inter, jnp.max(dmat, axis=2))
        w_intra = jnp.exp(dmat - m_t[:, :, None])
        w_inter = jnp.exp(m_inter - m_t)
        qk = jnp.einsum('bthd,bshd->btsh', qc, kc) * w_intra
        num = jnp.einsum('btsh,bshv->bthv', qk, vc) + w_inter[..., None] * jnp.einsum('bhvd,bthd->bthv', C, qc)
        den = jnp.sum(qk, axis=2) + w_inter * jnp.einsum('bhd,bthd->bth', n, qc)
        hout = num / jnp.maximum(jnp.abs(den), jnp.exp(-m_t))[..., None]
        m_last = m_t[:, -1]
        w_state = jnp.exp(b[:, -1:] - b + ic - m_last[:, None])
        decay = jnp.exp(b[:, -1] + m - m_last)
        C = decay[..., None, None] * C + jnp.einsum('bsh,bshv,bshd->bhvd', w_state, vc, kc)
        n = decay[..., None] * n + jnp.einsum('bsh,bshd->bhd', w_state, kc)
        return (C, n, m_last), hout

    carry0 = (c0.astype(jnp.float32), n0.astype(jnp.float32), m0.astype(jnp.float32))
    (C, n, m), hout = lax.scan(step, carry0, (chunks(q), chunks(k), chunks(v), chunks(i_pre), chunks(log_f)))
    return jnp.moveaxis(hout, 0, 1).reshape(v.shape), C, n, m


def mlstm_mixer(h, c0, n0, m0, w_in, b_if, norm_w, w_o):
    B, L, _ = h.shape
    H, DK, DV = ML_HEADS, ML_QK_DIM, ML_V_DIM
    proj = h @ w_in
    o1, o2, o3, o4 = H * DK, 2 * H * DK, 2 * H * DK + H * DV, 2 * H * DK + 2 * H * DV
    q = proj[..., :o1].reshape(B, L, H, DK).astype(jnp.float32) * DK ** -0.5
    k = proj[..., o1:o2].reshape(B, L, H, DK).astype(jnp.float32)
    v = proj[..., o2:o3].reshape(B, L, H, DV).astype(jnp.float32)
    o = jax.nn.sigmoid(proj[..., o3:o4])
    gates = proj[..., o4:].astype(jnp.float32) + b_if
    i_pre, log_f = gates[..., :H], jax.nn.log_sigmoid(gates[..., H:])
    hh, C, n, m = mlstm_chunked(q, k, v, i_pre, log_f, c0, n0, m0)
    hh = rmsnorm(hh, norm_w.reshape(H, DV)).reshape(B, L, H * DV)
    return (o * hh.astype(h.dtype)) @ w_o, C, n, m


def setup_inputs(seed: int = 0) -> dict:
    key = jax.random.key(seed)
    ks = iter(jax.random.split(key, 80))
    nrm = lambda shape, s: s * jax.random.normal(next(ks), shape, jnp.float32)
    unif = lambda shape, lo, hi: jax.random.uniform(next(ks), shape, jnp.float32, lo, hi)
    D = D_MODEL
    n_pages = PAST_LEN // PAGE_SIZE
    n_pool = (DEC_BATCH * n_pages * 5) // 4
    page_table = jax.random.permutation(next(ks), n_pool)[:DEC_BATCH * n_pages].reshape(DEC_BATCH, n_pages).astype(jnp.int32)
    dt0 = jnp.exp(unif((N_SSD, SSD_HEADS), math.log(1e-3), math.log(1e-1)))
    b_if = jnp.concatenate([nrm((N_MLSTM, ML_HEADS), 0.1),
                            jnp.linspace(3.0, 6.0, ML_HEADS)[None] + nrm((N_MLSTM, ML_HEADS), 0.1)], axis=-1)
    return {
        "x_prompt": nrm((BATCH, SEQ, D), 1.0),
        "x_sample": nrm((DEC_BATCH, DEC_SEQ, D), 1.0),
        "state_ssd_conv": nrm((N_SSD, DEC_BATCH, SSD_CONV - 1, SSD_CONV_DIM), 1.0),
        "state_ssd": nrm((N_SSD, DEC_BATCH, SSD_HEADS, SSD_HEADDIM, SSD_STATE), 0.1),
        "state_rwkv_shift": nrm((N_RWKV, DEC_BATCH, D), 1.0),
        "state_rwkv": nrm((N_RWKV, DEC_BATCH, RW_HEADS, RW_HEADDIM, RW_HEADDIM), 0.1),
        "cache_k": nrm((N_MOBA, n_pool, PAGE_SIZE, MB_HEADS, MB_HEADDIM), 1.0),
        "cache_v": nrm((N_MOBA, n_pool, PAGE_SIZE, MB_HEADS, MB_HEADDIM), 1.0),
        "state_mlstm_c": nrm((N_MLSTM, DEC_BATCH, ML_HEADS, ML_V_DIM, ML_QK_DIM), 0.1),
        "state_mlstm_n": nrm((N_MLSTM, DEC_BATCH, ML_HEADS, ML_QK_DIM), 0.3),
        "state_mlstm_m": nrm((N_MLSTM, DEC_BATCH, ML_HEADS), 0.5),
        "page_table": page_table,
        "c_prompt": nrm((BATCH, D), 1.0),
        "c_sample": nrm((DEC_BATCH, D), 1.0),
        "ada_w": nrm((DEPTH, D, N_MOD * D), D ** -0.5),
        "ada_b": nrm((DEPTH, N_MOD * D), 0.02),
        "norm_w": 1.0 + nrm((DEPTH, 3, D), 0.02),
        "ffn_w_in": nrm((DEPTH, 2, D, 2 * D_FF), D ** -0.5),
        "ffn_w_out": nrm((DEPTH, 2, D_FF, D), D_FF ** -0.5),
        "ssd_w_in": nrm((N_SSD, D, SSD_IN), D ** -0.5),
        "ssd_conv_w": nrm((N_SSD, SSD_CONV, SSD_CONV_DIM), 0.5),
        "ssd_conv_b": nrm((N_SSD, SSD_CONV_DIM), 0.02),
        "ssd_dt_bias": dt0 + jnp.log(-jnp.expm1(-dt0)),
        "ssd_a_log": jnp.log(unif((N_SSD, SSD_HEADS), 1.0, 16.0)),
        "ssd_d": 1.0 + nrm((N_SSD, SSD_HEADS), 0.1),
        "ssd_norm_w": 1.0 + nrm((N_SSD, SSD_D_INNER), 0.02),
        "ssd_w_out": nrm((N_SSD, SSD_D_INNER, D), SSD_D_INNER ** -0.5),
        "rw_mu": unif((N_RWKV, 6, D), 0.0, 1.0),
        "rw_w_r": nrm((N_RWKV, D, D), D ** -0.5),
        "rw_w_k": nrm((N_RWKV, D, D), D ** -0.5),
        "rw_w_v": nrm((N_RWKV, D, D), D ** -0.5),
        "rw_w0": unif((N_RWKV, D), -5.0, 0.0),
        "rw_w1": nrm((N_RWKV, D, RW_DECAY_LORA), D ** -0.5),
        "rw_w2": nrm((N_RWKV, RW_DECAY_LORA, D), 0.1),
        "rw_a0": nrm((N_RWKV, D), 0.1),
        "rw_a1": nrm((N_RWKV, D, RW_AAA_LORA), D ** -0.5),
        "rw_a2": nrm((N_RWKV, RW_AAA_LORA, D), 0.1),
        "rw_g1": nrm((N_RWKV, D, RW_GATE_LORA), D ** -0.5),
        "rw_g2": nrm((N_RWKV, RW_GATE_LORA, D), RW_GATE_LORA ** -0.5),
        "rw_k_k": 0.85 + nrm((N_RWKV, D), 0.02),
        "rw_k_a": 1.0 + nrm((N_RWKV, D), 0.02),
        "rw_r_k": nrm((N_RWKV, RW_HEADS, RW_HEADDIM), 0.1),
        "rw_lnx_w": 1.0 + nrm((N_RWKV, D), 0.02),
        "rw_lnx_b": nrm((N_RWKV, D), 0.02),
        "rw_w_o": nrm((N_RWKV, D, D), D ** -0.5),
        "mb_w_qkv": nrm((N_MOBA, D, 3 * D), D ** -0.5),
        "mb_q_norm": 1.0 + nrm((N_MOBA, MB_HEADDIM), 0.02),
        "mb_k_norm": 1.0 + nrm((N_MOBA, MB_HEADDIM), 0.02),
        "mb_w_o": nrm((N_MOBA, D, D), D ** -0.5),
        "ml_w_in": nrm((N_MLSTM, D, ML_IN), D ** -0.5),
        "ml_b_if": b_if,
        "ml_norm_w": 1.0 + nrm((N_MLSTM, ML_HEADS * ML_V_DIM), 0.02),
        "ml_w_o": nrm((N_MLSTM, ML_HEADS * ML_V_DIM, D), (ML_HEADS * ML_V_DIM) ** -0.5),
    }


def reference(x_prompt, x_sample, state_ssd_conv, state_ssd, state_rwkv_shift, state_rwkv,
              cache_k, cache_v, state_mlstm_c, state_mlstm_n, state_mlstm_m, page_table,
              c_prompt, c_sample, ada_w, ada_b, norm_w, ffn_w_in, ffn_w_out,
              ssd_w_in, ssd_conv_w, ssd_conv_b, ssd_dt_bias, ssd_a_log, ssd_d, ssd_norm_w, ssd_w_out,
              rw_mu, rw_w_r, rw_w_k, rw_w_v, rw_w0, rw_w1, rw_w2, rw_a0, rw_a1, rw_a2, rw_g1, rw_g2,
              rw_k_k, rw_k_a, rw_r_k, rw_lnx_w, rw_lnx_b, rw_w_o,
              mb_w_qkv, mb_q_norm, mb_k_norm, mb_w_o,
              ml_w_in, ml_b_if, ml_norm_w, ml_w_o):
    B, S = x_prompt.shape[0], x_prompt.shape[1]
    Bs, Ls = x_sample.shape[0], x_sample.shape[1]
    pos_p = jnp.arange(S, dtype=jnp.int32)
    pos_s = PAST_LEN + jnp.arange(Ls, dtype=jnp.int32)
    f32 = jnp.float32
    xp, xs = x_prompt, x_sample
    ssd_conv_p, ssd_conv_s, ssd_p, ssd_s = [], [], [], []
    rw_shift_p, rw_shift_s, rw_p, rw_s = [], [], [], []
    k_p, k_s, v_p, v_s = [], [], [], []
    ml_c_p, ml_c_s, ml_n_p, ml_n_s, ml_m_p, ml_m_s = [], [], [], [], [], []
    for i in range(DEPTH):
        kind, j = i % N_MIXERS, i // N_MIXERS
        mp = adaln_params(c_prompt, ada_w[i], ada_b[i])
        ms = adaln_params(c_sample, ada_w[i], ada_b[i])
        xp = macaron_ffn(xp, mp, 0, norm_w[i, 0], ffn_w_in[i, 0], ffn_w_out[i, 0])
        xs = macaron_ffn(xs, ms, 0, norm_w[i, 0], ffn_w_in[i, 0], ffn_w_out[i, 0])
        hp = sublayer_in(xp, norm_w[i, 1], mp[:, 3], mp[:, 4])
        hs = sublayer_in(xs, norm_w[i, 1], ms[:, 3], ms[:, 4])
        if kind == 0:
            w = (ssd_w_in[j], ssd_conv_w[j], ssd_conv_b[j], ssd_dt_bias[j], ssd_a_log[j], ssd_d[j], ssd_norm_w[j], ssd_w_out[j])
            yp, cp, sp = ssd_mixer(hp, jnp.zeros((B, SSD_CONV - 1, SSD_CONV_DIM), hp.dtype),
                                   jnp.zeros((B, SSD_HEADS, SSD_HEADDIM, SSD_STATE), f32), *w)
            ys, cs_, ss = ssd_mixer(hs, state_ssd_conv[j], state_ssd[j], *w)
            ssd_conv_p.append(cp); ssd_conv_s.append(cs_); ssd_p.append(sp); ssd_s.append(ss)
        elif kind == 1:
            w = (rw_mu[j], rw_w_r[j], rw_w_k[j], rw_w_v[j], rw_w0[j], rw_w1[j], rw_w2[j], rw_a0[j], rw_a1[j],
                 rw_a2[j], rw_g1[j], rw_g2[j], rw_k_k[j], rw_k_a[j], rw_r_k[j], rw_lnx_w[j], rw_lnx_b[j], rw_w_o[j])
            yp, shp, sp = rwkv7_mixer(hp, jnp.zeros((B, D_MODEL), hp.dtype),
                                      jnp.zeros((B, RW_HEADS, RW_HEADDIM, RW_HEADDIM), f32), *w)
            ys, shs, ss = rwkv7_mixer(hs, state_rwkv_shift[j], state_rwkv[j], *w)
            rw_shift_p.append(shp); rw_shift_s.append(shs); rw_p.append(sp); rw_s.append(ss)
        elif kind == 2:
            qp, kp, vp = moba_qkv(hp, pos_p, mb_w_qkv[j], mb_q_norm[j], mb_k_norm[j])
            qs, ks_, vs = moba_qkv(hs, pos_s, mb_w_qkv[j], mb_q_norm[j], mb_k_norm[j])
            op = lax.map(lambda a: moba_attend_one(pos_p, a[0], a[1], a[2]), (qp, kp, vp))
            ck, cv = cache_k[j], cache_v[j]
            os_ = lax.map(lambda a: moba_sample_one(pos_s, ck, cv, a), (qs, ks_, vs, page_table))
            yp = op.reshape(B, S, D_MODEL) @ mb_w_o[j]
            ys = os_.reshape(Bs, Ls, D_MODEL) @ mb_w_o[j]
            k_p.append(kp); k_s.append(ks_); v_p.append(vp); v_s.append(vs)
        else:
            w = (ml_w_in[j], ml_b_if[j], ml_norm_w[j], ml_w_o[j])
            yp, cp, np_, mp_ = mlstm_mixer(hp, jnp.zeros((B, ML_HEADS, ML_V_DIM, ML_QK_DIM), f32),
                                           jnp.zeros((B, ML_HEADS, ML_QK_DIM), f32), jnp.zeros((B, ML_HEADS), f32), *w)
            ys, cs_, ns_, ms_ = mlstm_mixer(hs, state_mlstm_c[j], state_mlstm_n[j], state_mlstm_m[j], *w)
            ml_c_p.append(cp); ml_c_s.append(cs_); ml_n_p.append(np_); ml_n_s.append(ns_)
            ml_m_p.append(mp_); ml_m_s.append(ms_)
        xp = xp + mp[:, 5] * yp.astype(xp.dtype)
        xs = xs + ms[:, 5] * ys.astype(xs.dtype)
        xp = macaron_ffn(xp, mp, 6, norm_w[i, 2], ffn_w_in[i, 1], ffn_w_out[i, 1])
        xs = macaron_ffn(xs, ms, 6, norm_w[i, 2], ffn_w_in[i, 1], ffn_w_out[i, 1])
    return (xp, xs,
            jnp.stack(ssd_conv_p), jnp.stack(ssd_conv_s), jnp.stack(ssd_p), jnp.stack(ssd_s),
            jnp.stack(rw_shift_p), jnp.stack(rw_shift_s), jnp.stack(rw_p), jnp.stack(rw_s),
            jnp.stack(k_p), jnp.stack(k_s), jnp.stack(v_p), jnp.stack(v_s),
            jnp.stack(ml_c_p), jnp.stack(ml_c_s), jnp.stack(ml_n_p), jnp.stack(ml_n_s),
            jnp.stack(ml_m_p), jnp.stack(ml_m_s))
```

```python
import functools
import math

import jax
import jax.numpy as jnp
from jax import lax
from jax.experimental import pallas as pl
from jax.experimental.pallas import tpu as pltpu

D_MODEL = 1024
DEPTH = 4
PAST_LEN = 2048
N_MIXERS = 4
NORM_EPS = 1e-6
N_MOD = 9
D_FF = 2816

SSD_D_INNER = 2 * D_MODEL
SSD_HEADDIM = 64
SSD_HEADS = SSD_D_INNER // SSD_HEADDIM
SSD_GROUPS = 4
SSD_HPG = SSD_HEADS // SSD_GROUPS
SSD_STATE = 128
SSD_CONV = 4
SSD_CHUNK = 128
SSD_CONV_DIM = SSD_D_INNER + 2 * SSD_GROUPS * SSD_STATE

RW_HEADDIM = 64
RW_HEADS = D_MODEL // RW_HEADDIM
RW_LNX_EPS = 64e-5

MB_HEADS = 16
MB_HEADDIM = D_MODEL // MB_HEADS
MB_BLOCK = 256
MB_TOPK = 3
MB_QBLOCK = 16
ROPE_DIM = MB_HEADDIM // 4
ROPE_THETA = 500000.0

ML_HEADS = 8
ML_QK_DIM = D_MODEL // 2 // ML_HEADS
ML_V_DIM = D_MODEL // ML_HEADS
ML_CHUNK = 64

LANE = 128
VMEM_LIMIT = 48 * 1024 * 1024

BF16 = jnp.bfloat16
F32 = jnp.float32


def _cparams(sem):
    return pltpu.CompilerParams(dimension_semantics=sem, vmem_limit_bytes=VMEM_LIMIT)


def _mm_kernel(a_ref, w_ref, o_ref):
    o_ref[...] = jnp.dot(a_ref[...].astype(BF16), w_ref[...], preferred_element_type=F32)


def _pick(n, cands):
    for c in cands:
        if n % c == 0:
            return c
    raise ValueError(n)


def mm(a, w):
    lead = a.shape[:-1]
    K = a.shape[-1]
    N = w.shape[-1]
    a2 = a.reshape(-1, K)
    M = a2.shape[0]
    Mp = -(-M // 8) * 8
    if Mp != M:
        a2 = jnp.pad(a2, ((0, Mp - M), (0, 0)))
    Np = -(-N // LANE) * LANE
    wb = w.astype(BF16)
    if Np != N:
        wb = jnp.pad(wb, ((0, 0), (0, Np - N)))
    tm = _pick(Mp, (512, 256, 128, 64, 32, 16, 8))
    tn = _pick(Np, (1024, 512, 256, 128))
    out = pl.pallas_call(
        _mm_kernel,
        grid=(Mp // tm, Np // tn),
        in_specs=[pl.BlockSpec((tm, K), lambda i, j: (i, 0)),
                  pl.BlockSpec((K, tn), lambda i, j: (0, j))],
        out_specs=pl.BlockSpec((tm, tn), lambda i, j: (i, j)),
        out_shape=jax.ShapeDtypeStruct((Mp, Np), F32),
        compiler_params=_cparams(("parallel", "arbitrary")),
    )(a2, wb)
    return out[:M, :N].reshape(*lead, N)


def _ffn_kernel(x_ref, sh_ref, sc_ref, gt_ref, nw_ref, wg_ref, wu_ref, wo_ref, o_ref, h_sc, acc_sc):
    k = pl.program_id(2)
    gb, rb, d = x_ref.shape

    @pl.when(k == 0)
    def _():
        x = x_ref[...]
        y = x * lax.rsqrt(jnp.mean(x * x, axis=-1, keepdims=True) + NORM_EPS) * nw_ref[...]
        h = y * (1.0 + sc_ref[...]) + sh_ref[...]
        h_sc[...] = h.reshape(gb * rb, d).astype(BF16)

    h = h_sc[...]
    g = jnp.dot(h, wg_ref[...], preferred_element_type=F32)
    u = jnp.dot(h, wu_ref[...], preferred_element_type=F32)
    a = (g * jax.nn.sigmoid(g) * u).astype(BF16)
    part = jnp.dot(a, wo_ref[...], preferred_element_type=F32)

    @pl.when(k == 0)
    def _():
        acc_sc[...] = part

    @pl.when(k > 0)
    def _():
        acc_sc[...] += part

    @pl.when(k == pl.num_programs(2) - 1)
    def _():
        o_ref[...] = x_ref[...] + 0.5 * gt_ref[...] * acc_sc[...].reshape(gb, rb, d)


def ffn(x, shift, scale, gate, nw, w_in_b, w_out_b, gb, rb):
    G, R, D = x.shape
    tf = D_FF // 2
    nk = D_FF // tf
    xspec = pl.BlockSpec((gb, rb, D), lambda g, r, k: (g, r, 0))
    mspec = pl.BlockSpec((gb, 1, D), lambda g, r, k: (g, 0, 0))
    return pl.pallas_call(
        _ffn_kernel,
        grid=(G // gb, R // rb, nk),
        in_specs=[xspec, mspec, mspec, mspec,
                  pl.BlockSpec((1, 1, D), lambda g, r, k: (0, 0, 0)),
                  pl.BlockSpec((D, tf), lambda g, r, k: (0, k)),
                  pl.BlockSpec((D, tf), lambda g, r, k: (0, k + nk)),
                  pl.BlockSpec((tf, D), lambda g, r, k: (k, 0))],
        out_specs=xspec,
        out_shape=jax.ShapeDtypeStruct(x.shape, F32),
        scratch_shapes=[pltpu.VMEM((gb * rb, D), BF16), pltpu.VMEM((gb * rb, D), F32)],
        compiler_params=_cparams(("parallel", "parallel", "arbitrary")),
    )(x, shift, scale, gate, nw.reshape(1, 1, D), w_in_b, w_in_b, w_out_b)


def rmsnorm(x, w):
    xf = x.astype(jnp.float32)
    y = xf * lax.rsqrt(jnp.mean(xf * xf, axis=-1, keepdims=True) + NORM_EPS)
    return (y * w).astype(x.dtype)


def sublayer_in(x, g, shift, scale):
    return rmsnorm(x, g) * (1 + scale) + shift


def causal_dwconv(x, buf, w, b):
    L, K = x.shape[1], w.shape[0]
    xp = jnp.concatenate([buf.astype(x.dtype), x], axis=1)
    y = b + sum(xp[:, k:k + L] * w[k] for k in range(K))
    return y, xp[:, -(K - 1):]


def ssd_scan(x, dt, a, bm, cm, s0):
    B, L = x.shape[0], x.shape[1]
    Q = math.gcd(L, SSD_CHUNK)
    nc = L // Q
    chunks = lambda t: jnp.moveaxis(t.reshape(B, nc, Q, *t.shape[2:]), 1, 0)
    causal = jnp.tril(jnp.ones((Q, Q), bool))[None, :, :, None, None]

    def step(s, inp):
        xc, dtc, bc, cc = inp
        xf, bf, cf = xc.astype(jnp.float32), bc.astype(jnp.float32), cc.astype(jnp.float32)
        cs = jnp.cumsum(dtc * a, axis=1)
        decay = jnp.exp(jnp.where(causal, cs[:, :, None] - cs[:, None], -jnp.inf))
        cb = jnp.einsum('btgn,bsgn->btsg', cf, bf)
        wgt = cb[..., None] * decay * dtc[:, None]
        y = jnp.einsum('btsgr,bsgrp->btgrp', wgt, xf)
        y = y + jnp.einsum('btgn,bgrpn->btgrp', cf, s) * jnp.exp(cs)[..., None]
        w_end = jnp.exp(cs[:, -1:] - cs) * dtc
        s_new = s * jnp.exp(cs[:, -1])[..., None, None] + jnp.einsum('bsgr,bsgn,bsgrp->bgrpn', w_end, bf, xf)
        return s_new, y

    s_fin, y = lax.scan(step, s0, (chunks(x), chunks(dt), chunks(bm), chunks(cm)))
    y = jnp.moveaxis(y, 0, 1).reshape(x.shape)
    return y, s_fin


def ssd_mixer(h, conv_buf, ssm0, w_in, conv_w, conv_b, dt_bias, a_log, d_skip, norm_w, w_out):
    B, L, _ = h.shape
    G, R, P, N = SSD_GROUPS, SSD_HPG, SSD_HEADDIM, SSD_STATE
    zxbcdt = mm(h, w_in)
    z = zxbcdt[..., :SSD_D_INNER]
    xbc = zxbcdt[..., SSD_D_INNER:SSD_D_INNER + SSD_CONV_DIM]
    dt_raw = zxbcdt[..., SSD_D_INNER + SSD_CONV_DIM:]
    xbc, new_conv = causal_dwconv(xbc, conv_buf, conv_w, conv_b)
    xbc = jax.nn.silu(xbc)
    xs = xbc[..., :SSD_D_INNER].reshape(B, L, G, R, P)
    bm = xbc[..., SSD_D_INNER:SSD_D_INNER + G * N].reshape(B, L, G, N)
    cm = xbc[..., SSD_D_INNER + G * N:].reshape(B, L, G, N)
    dt = jax.nn.softplus(dt_raw.astype(jnp.float32) + dt_bias).reshape(B, L, G, R)
    a = -jnp.exp(a_log.astype(jnp.float32)).reshape(G, R)
    y, s_fin = ssd_scan(xs, dt, a, bm, cm, ssm0.astype(jnp.float32).reshape(B, G, R, P, N))
    y = y + xs.astype(jnp.float32) * d_skip.reshape(G, R, 1)
    yz = y.reshape(B, L, SSD_D_INNER) * jax.nn.silu(z.astype(jnp.float32))
    yz = rmsnorm(yz.reshape(B, L, G, -1), norm_w.reshape(G, -1)).reshape(B, L, SSD_D_INNER)
    return mm(yz.astype(h.dtype), w_out), new_conv, s_fin.reshape(B, SSD_HEADS, P, N)


def rwkv7_mixer(h, shift_buf, wkv0, mu, w_r, w_k, w_v, w0, w1, w2, a0, a1, a2, g1, g2,
                k_k, k_a, r_k, lnx_w, lnx_b, w_o):
    B, L, D = h.shape
    H, N = RW_HEADS, RW_HEADDIM
    prev = jnp.concatenate([shift_buf[:, None, :].astype(h.dtype), h[:, :-1]], axis=1)
    xx = prev - h
    xr, xw, xk = h + xx * mu[0], h + xx * mu[1], h + xx * mu[2]
    xv, xa, xg = h + xx * mu[3], h + xx * mu[4], h + xx * mu[5]
    r, k, v = mm(xr, w_r), mm(xk, w_k), mm(xv, w_v)
    w_raw = (w0 + mm(jnp.tanh(mm(xw, w1)), w2)).astype(jnp.float32)
    decay = jnp.exp(-jnp.exp(-jax.nn.softplus(-w_raw) - 0.5))
    a = jax.nn.sigmoid(a0 + mm(mm(xa, a1), a2))
    g = mm(jax.nn.sigmoid(mm(xg, g1)), g2)
    heads = lambda t: t.reshape(B, L, H, N).astype(jnp.float32)
    kk = heads(k * k_k)
    kk = kk / jnp.maximum(jnp.sqrt(jnp.sum(kk * kk, axis=-1, keepdims=True)), 1e-12)
    k = k * (1 + (a - 1) * k_a)
    rh, kh, vh, ah, wh = heads(r), heads(k), heads(v), heads(a), heads(decay)
    tm = lambda t: jnp.moveaxis(t, 1, 0)

    def step(S, inp):
        r_t, w_t, k_t, v_t, kk_t, a_t = inp
        sa = jnp.einsum('bhvk,bhk->bhv', S, -kk_t)
        S = S * w_t[:, :, None, :] + sa[..., None] * (kk_t * a_t)[:, :, None, :] + v_t[..., None] * k_t[:, :, None, :]
        return S, jnp.einsum('bhvk,bhk->bhv', S, r_t)

    S_fin, y = lax.scan(step, wkv0.astype(jnp.float32), (tm(rh), tm(wh), tm(kh), tm(vh), tm(kk), tm(ah)))
    y = jnp.moveaxis(y, 0, 1)
    mean = jnp.mean(y, axis=-1, keepdims=True)
    var = jnp.mean(jnp.square(y - mean), axis=-1, keepdims=True)
    y = ((y - mean) * lax.rsqrt(var + RW_LNX_EPS)).reshape(B, L, D) * lnx_w + lnx_b
    bonus = jnp.sum(rh * kh * r_k, axis=-1, keepdims=True) * vh
    y = (y + bonus.reshape(B, L, D)).astype(h.dtype)
    return mm(y * g, w_o), h[:, -1], S_fin


def rope_partial(x, pos):
    half = ROPE_DIM // 2
    inv = ROPE_THETA ** (-jnp.arange(half, dtype=jnp.float32) / half)
    ang = pos.astype(jnp.float32)[:, None] * inv
    cos, sin = jnp.cos(ang)[:, None, :], jnp.sin(ang)[:, None, :]
    x1 = x[..., :half].astype(jnp.float32)
    x2 = x[..., half:ROPE_DIM].astype(jnp.float32)
    rot = jnp.concatenate([x1 * cos - x2 * sin, x2 * cos + x1 * sin], axis=-1).astype(x.dtype)
    return jnp.concatenate([rot, x[..., ROPE_DIM:]], axis=-1)


def moba_qkv(h, pos, w_qkv, q_norm, k_norm):
    B, L, _ = h.shape
    qkv = mm(h, w_qkv).reshape(B, L, 3, MB_HEADS, MB_HEADDIM)
    q = rope_partial(rmsnorm(qkv[:, :, 0], q_norm), pos)
    k = rope_partial(rmsnorm(qkv[:, :, 1], k_norm), pos)
    return q, k, qkv[:, :, 2]


def moba_attend_one(qpos, q, k_all, v_all):
    Lq, Lk = q.shape[0], k_all.shape[0]
    H, Dh, K = MB_HEADS, MB_HEADDIM, MB_TOPK
    nb = max(-(-Lk // MB_BLOCK), K)
    pad = nb * MB_BLOCK - Lk
    blocks = lambda t: jnp.pad(t, ((0, pad), (0, 0), (0, 0))).reshape(nb, MB_BLOCK, H, Dh).transpose(2, 0, 1, 3)
    kb, vb = blocks(k_all), blocks(v_all)
    kmean = jnp.mean(kb.astype(jnp.float32), axis=2)
    cur = qpos // MB_BLOCK
    gate = jnp.einsum('qhd,hnd->hqn', q.astype(jnp.float32), kmean)
    fully_past = jnp.arange(nb)[None, :] < cur[:, None]
    gate = jnp.where(fully_past[None], gate, -jnp.inf)
    _, top = lax.top_k(gate, K)
    sel_ok = jnp.arange(K)[None, :] < cur[:, None]
    idx = jnp.concatenate([top, jnp.broadcast_to(cur[None, :, None], (H, Lq, 1))], axis=-1)
    qb = math.gcd(Lq, MB_QBLOCK)
    nq = Lq // qb
    hidx = jnp.arange(H)[:, None, None]
    scale = MB_HEADDIM ** -0.5

    def one_block(args):
        qs, ps, ids, ok = args
        kg, vg = kb[hidx, ids], vb[hidx, ids]
        s = jnp.einsum('qhd,hqsbd->hqsb', qs, kg).astype(jnp.float32) * scale
        kpos = ids[..., None] * MB_BLOCK + jnp.arange(MB_BLOCK)
        valid = jnp.concatenate([ok, jnp.ones((qb, 1), bool)], axis=-1)
        mask = valid[None, :, :, None] & (kpos <= ps[None, :, None, None])
        s = jnp.where(mask, s, -jnp.inf)
        p = jax.nn.softmax(s.reshape(H, qb, -1), axis=-1).reshape(s.shape)
        return jnp.einsum('hqsb,hqsbd->qhd', p.astype(vg.dtype), vg)

    out = lax.map(one_block, (q.reshape(nq, qb, H, Dh), qpos.reshape(nq, qb),
                              idx.reshape(H, nq, qb, K + 1).transpose(1, 0, 2, 3), sel_ok.reshape(nq, qb, K)))
    return out.reshape(Lq, H, Dh)


def moba_sample_one(qpos, ck, cv, args):
    q, k_new, v_new, pt = args
    k_all = jnp.concatenate([ck[pt].reshape(-1, MB_HEADS, MB_HEADDIM).astype(k_new.dtype), k_new], axis=0)
    v_all = jnp.concatenate([cv[pt].reshape(-1, MB_HEADS, MB_HEADDIM).astype(v_new.dtype), v_new], axis=0)
    return moba_attend_one(qpos, q, k_all, v_all)


def mlstm_chunked(q, k, v, i_pre, log_f, c0, n0, m0):
    B, L = q.shape[0], q.shape[1]
    Q = math.gcd(L, ML_CHUNK)
    nc = L // Q
    chunks = lambda t: jnp.moveaxis(t.reshape(B, nc, Q, *t.shape[2:]), 1, 0)
    causal = jnp.tril(jnp.ones((Q, Q), bool))[None, :, :, None]

    def step(carry, inp):
        C, n, m = carry
        qc, kc, vc, ic, fc = inp
        b = jnp.cumsum(fc, axis=1)
        dmat = jnp.where(causal, b[:, :, None] - b[:, None] + ic[:, None], -jnp.inf)
        m_inter = b + m[:, None]
        m_t = jnp.maximum(m_inter, jnp.max(dmat, axis=2))
        w_intra = jnp.exp(dmat - m_t[:, :, None])
        w_inter = jnp.exp(m_inter - m_t)
        qk = jnp.einsum('bthd,bshd->btsh', qc, kc) * w_intra
        num = jnp.einsum('btsh,bshv->bthv', qk, vc) + w_inter[..., None] * jnp.einsum('bhvd,bthd->bthv', C, qc)
        den = jnp.sum(qk, axis=2) + w_inter * jnp.einsum('bhd,bthd->bth', n, qc)
        hout = num / jnp.maximum(jnp.abs(den), jnp.exp(-m_t))[..., None]
        m_last = m_t[:, -1]
        w_state = jnp.exp(b[:, -1:] - b + ic - m_last[:, None])
        decay = jnp.exp(b[:, -1] + m - m_last)
        C = decay[..., None, None] * C + jnp.einsum('bsh,bshv,bshd->bhvd', w_state, vc, kc)
        n = decay[..., None] * n + jnp.einsum('bsh,bshd->bhd', w_state, kc)
        return (C, n, m_last), hout

    carry0 = (c0.astype(jnp.float32), n0.astype(jnp.float32), m0.astype(jnp.float32))
    (C, n, m), hout = lax.scan(step, carry0, (chunks(q), chunks(k), chunks(v), chunks(i_pre), chunks(log_f)))
    return jnp.moveaxis(hout, 0, 1).reshape(v.shape), C, n, m


def mlstm_mixer(h, c0, n0, m0, w_in, b_if, norm_w, w_o):
    B, L, _ = h.shape
    H, DK, DV = ML_HEADS, ML_QK_DIM, ML_V_DIM
    proj = mm(h, w_in)
    o1, o2, o3, o4 = H * DK, 2 * H * DK, 2 * H * DK + H * DV, 2 * H * DK + 2 * H * DV
    q = proj[..., :o1].reshape(B, L, H, DK).astype(jnp.float32) * DK ** -0.5
    k = proj[..., o1:o2].reshape(B, L, H, DK).astype(jnp.float32)
    v = proj[..., o2:o3].reshape(B, L, H, DV).astype(jnp.float32)
    o = jax.nn.sigmoid(proj[..., o3:o4])
    gates = proj[..., o4:].astype(jnp.float32) + b_if
    i_pre, log_f = gates[..., :H], jax.nn.log_sigmoid(gates[..., H:])
    hh, C, n, m = mlstm_chunked(q, k, v, i_pre, log_f, c0, n0, m0)
    hh = rmsnorm(hh, norm_w.reshape(H, DV)).reshape(B, L, H * DV)
    return mm(o * hh.astype(h.dtype), w_o), C, n, m


def kernel(x_prompt, x_sample, state_ssd_conv, state_ssd, state_rwkv_shift, state_rwkv, cache_k, cache_v, state_mlstm_c, state_mlstm_n, state_mlstm_m, page_table, c_prompt, c_sample, ada_w, ada_b, norm_w, ffn_w_in, ffn_w_out, ssd_w_in, ssd_conv_w, ssd_conv_b, ssd_dt_bias, ssd_a_log, ssd_d, ssd_norm_w, ssd_w_out, rw_mu, rw_w_r, rw_w_k, rw_w_v, rw_w0, rw_w1, rw_w2, rw_a0, rw_a1, rw_a2, rw_g1, rw_g2, rw_k_k, rw_k_a, rw_r_k, rw_lnx_w, rw_lnx_b, rw_w_o, mb_w_qkv, mb_q_norm, mb_k_norm, mb_w_o, ml_w_in, ml_b_if, ml_norm_w, ml_w_o):
    B, S = x_prompt.shape[0], x_prompt.shape[1]
    Bs, Ls = x_sample.shape[0], x_sample.shape[1]
    D = D_MODEL
    pos_p = jnp.arange(S, dtype=jnp.int32)
    pos_s = PAST_LEN + jnp.arange(Ls, dtype=jnp.int32)
    xp, xs = x_prompt, x_sample
    outs = {}
    c_all = jnp.concatenate([c_prompt, c_sample], axis=0)
    sc_all = jax.nn.silu(c_all)
    for i in range(DEPTH):
        kind, j = i % N_MIXERS, i // N_MIXERS
        mod = (mm(sc_all, ada_w[i]) + ada_b[i]).reshape(B + Bs, N_MOD, 1, D)
        mp, ms = mod[:B], mod[B:]
        wi0, wo0 = ffn_w_in[i, 0].astype(BF16), ffn_w_out[i, 0].astype(BF16)
        wi1, wo1 = ffn_w_in[i, 1].astype(BF16), ffn_w_out[i, 1].astype(BF16)
        xp = ffn(xp, mp[:, 0], mp[:, 1], mp[:, 2], norm_w[i, 0], wi0, wo0, 1, 512)
        xs = ffn(xs, ms[:, 0], ms[:, 1], ms[:, 2], norm_w[i, 0], wi0, wo0, 64, Ls)
        hp = sublayer_in(xp, norm_w[i, 1], mp[:, 3], mp[:, 4])
        hs = sublayer_in(xs, norm_w[i, 1], ms[:, 3], ms[:, 4])
        if kind == 0:
            w = (ssd_w_in[j], ssd_conv_w[j], ssd_conv_b[j], ssd_dt_bias[j], ssd_a_log[j], ssd_d[j], ssd_norm_w[j], ssd_w_out[j])
            yp, cp, sp = ssd_mixer(hp, jnp.zeros((B, SSD_CONV - 1, SSD_CONV_DIM), F32),
                                   jnp.zeros((B, SSD_HEADS, SSD_HEADDIM, SSD_STATE), F32), *w)
            ys, cs_, ss = ssd_mixer(hs, state_ssd_conv[j], state_ssd[j], *w)
            outs[2], outs[3], outs[4], outs[5] = cp[None], cs_[None], sp[None], ss[None]
        elif kind == 1:
            w = (rw_mu[j], rw_w_r[j], rw_w_k[j], rw_w_v[j], rw_w0[j], rw_w1[j], rw_w2[j], rw_a0[j], rw_a1[j],
                 rw_a2[j], rw_g1[j], rw_g2[j], rw_k_k[j], rw_k_a[j], rw_r_k[j], rw_lnx_w[j], rw_lnx_b[j], rw_w_o[j])
            yp, shp, sp = rwkv7_mixer(hp, jnp.zeros((B, D), F32),
                                      jnp.zeros((B, RW_HEADS, RW_HEADDIM, RW_HEADDIM), F32), *w)
            ys, shs, ss = rwkv7_mixer(hs, state_rwkv_shift[j], state_rwkv[j], *w)
            outs[6], outs[7], outs[8], outs[9] = shp[None], shs[None], sp[None], ss[None]
        elif kind == 2:
            qp, kp, vp = moba_qkv(hp, pos_p, mb_w_qkv[j], mb_q_norm[j], mb_k_norm[j])
            qs, ks_, vs = moba_qkv(hs, pos_s, mb_w_qkv[j], mb_q_norm[j], mb_k_norm[j])
            op = lax.map(lambda a: moba_attend_one(pos_p, a[0], a[1], a[2]), (qp, kp, vp))
            ck, cv = cache_k[j], cache_v[j]
            os_ = lax.map(lambda a: moba_sample_one(pos_s, ck, cv, a), (qs, ks_, vs, page_table))
            yp = mm(op.reshape(B, S, D), mb_w_o[j])
            ys = mm(os_.reshape(Bs, Ls, D), mb_w_o[j])
            outs[10], outs[11], outs[12], outs[13] = kp[None], ks_[None], vp[None], vs[None]
        else:
            w = (ml_w_in[j], ml_b_if[j], ml_norm_w[j], ml_w_o[j])
            yp, cp, np_, mp_ = mlstm_mixer(hp, jnp.zeros((B, ML_HEADS, ML_V_DIM, ML_QK_DIM), F32),
                                           jnp.zeros((B, ML_HEADS, ML_QK_DIM), F32), jnp.zeros((B, ML_HEADS), F32), *w)
            ys, cs_, ns_, ms_ = mlstm_mixer(hs, state_mlstm_c[j], state_mlstm_n[j], state_mlstm_m[j], *w)
            outs[14], outs[15], outs[16], outs[17], outs[18], outs[19] = cp[None], cs_[None], np_[None], ns_[None], mp_[None], ms_[None]
        xp = xp + mp[:, 5] * yp
        xs = xs + ms[:, 5] * ys
        xp = ffn(xp, mp[:, 6], mp[:, 7], mp[:, 8], norm_w[i, 2], wi1, wo1, 1, 512)
        xs = ffn(xs, ms[:, 6], ms[:, 7], ms[:, 8], norm_w[i, 2], wi1, wo1, 64, Ls)
    outs[0], outs[1] = xp, xs
    return tuple(outs[i] for i in range(20))
```

```python
import functools
import math

import jax
import jax.numpy as jnp
from jax import lax
from jax.experimental import pallas as pl
from jax.experimental.pallas import tpu as pltpu

D_MODEL = 1024
DEPTH = 4
PAST_LEN = 2048
N_MIXERS = 4
NORM_EPS = 1e-6
N_MOD = 9
D_FF = 2816

SSD_D_INNER = 2 * D_MODEL
SSD_HEADDIM = 64
SSD_HEADS = SSD_D_INNER // SSD_HEADDIM
SSD_GROUPS = 4
SSD_HPG = SSD_HEADS // SSD_GROUPS
SSD_STATE = 128
SSD_CONV = 4
SSD_CHUNK = 128
SSD_CONV_DIM = SSD_D_INNER + 2 * SSD_GROUPS * SSD_STATE

RW_HEADDIM = 64
RW_HEADS = D_MODEL // RW_HEADDIM
RW_LNX_EPS = 64e-5

MB_HEADS = 16
MB_HEADDIM = D_MODEL // MB_HEADS
MB_BLOCK = 256
MB_TOPK = 3
MB_QBLOCK = 16
ROPE_DIM = MB_HEADDIM // 4
ROPE_THETA = 500000.0

ML_HEADS = 8
ML_QK_DIM = D_MODEL // 2 // ML_HEADS
ML_V_DIM = D_MODEL // ML_HEADS
ML_CHUNK = 64

LANE = 128
VMEM_LIMIT = 48 * 1024 * 1024

BF16 = jnp.bfloat16
F32 = jnp.float32


def _cparams(sem):
    return pltpu.CompilerParams(dimension_semantics=sem, vmem_limit_bytes=VMEM_LIMIT)


def _mm_kernel(a_ref, w_ref, o_ref):
    o_ref[...] = jnp.dot(a_ref[...].astype(BF16), w_ref[...], preferred_element_type=F32)


def _pick(n, cands):
    for c in cands:
        if n % c == 0:
            return c
    raise ValueError(n)


def mm(a, w):
    lead = a.shape[:-1]
    K = a.shape[-1]
    N = w.shape[-1]
    a2 = a.reshape(-1, K)
    M = a2.shape[0]
    Mp = -(-M // 8) * 8
    if Mp != M:
        a2 = jnp.pad(a2, ((0, Mp - M), (0, 0)))
    Np = -(-N // LANE) * LANE
    wb = w.astype(BF16)
    if Np != N:
        wb = jnp.pad(wb, ((0, 0), (0, Np - N)))
    tm = _pick(Mp, (512, 256, 128, 64, 32, 16, 8))
    tn = _pick(Np, (1024, 512, 256, 128))
    out = pl.pallas_call(
        _mm_kernel, name="mm",
        grid=(Mp // tm, Np // tn),
        in_specs=[pl.BlockSpec((tm, K), lambda i, j: (i, 0)),
                  pl.BlockSpec((K, tn), lambda i, j: (0, j))],
        out_specs=pl.BlockSpec((tm, tn), lambda i, j: (i, j)),
        out_shape=jax.ShapeDtypeStruct((Mp, Np), F32),
        compiler_params=_cparams(("parallel", "arbitrary")),
    )(a2, wb)
    return out[:M, :N].reshape(*lead, N)


def _ffn_kernel(x_ref, sh_ref, sc_ref, gt_ref, nw_ref, wg_ref, wu_ref, wo_ref, o_ref, h_sc, acc_sc):
    k = pl.program_id(2)
    gb, rb, d = x_ref.shape

    @pl.when(k == 0)
    def _():
        x = x_ref[...]
        y = x * lax.rsqrt(jnp.mean(x * x, axis=-1, keepdims=True) + NORM_EPS) * nw_ref[...]
        h = y * (1.0 + sc_ref[...]) + sh_ref[...]
        h_sc[...] = h.reshape(gb * rb, d).astype(BF16)

    h = h_sc[...]
    g = jnp.dot(h, wg_ref[...], preferred_element_type=F32)
    u = jnp.dot(h, wu_ref[...], preferred_element_type=F32)
    a = (g * jax.nn.sigmoid(g) * u).astype(BF16)
    part = jnp.dot(a, wo_ref[...], preferred_element_type=F32)

    @pl.when(k == 0)
    def _():
        acc_sc[...] = part

    @pl.when(k > 0)
    def _():
        acc_sc[...] += part

    @pl.when(k == pl.num_programs(2) - 1)
    def _():
        o_ref[...] = x_ref[...] + 0.5 * gt_ref[...] * acc_sc[...].reshape(gb, rb, d)


def ffn(x, shift, scale, gate, nw, w_in_b, w_out_b, gb, rb):
    G, R, D = x.shape
    tf = D_FF // 2
    nk = D_FF // tf
    xspec = pl.BlockSpec((gb, rb, D), lambda g, r, k: (g, r, 0))
    mspec = pl.BlockSpec((gb, 1, D), lambda g, r, k: (g, 0, 0))
    return pl.pallas_call(
        _ffn_kernel, name="ffn",
        grid=(G // gb, R // rb, nk),
        in_specs=[xspec, mspec, mspec, mspec,
                  pl.BlockSpec((1, 1, D), lambda g, r, k: (0, 0, 0)),
                  pl.BlockSpec((D, tf), lambda g, r, k: (0, k)),
                  pl.BlockSpec((D, tf), lambda g, r, k: (0, k + nk)),
                  pl.BlockSpec((tf, D), lambda g, r, k: (k, 0))],
        out_specs=xspec,
        out_shape=jax.ShapeDtypeStruct(x.shape, F32),
        scratch_shapes=[pltpu.VMEM((gb * rb, D), BF16), pltpu.VMEM((gb * rb, D), F32)],
        compiler_params=_cparams(("parallel", "parallel", "arbitrary")),
    )(x, shift, scale, gate, nw.reshape(1, 1, D), w_in_b, w_in_b, w_out_b)


HI = lax.Precision.HIGHEST
NN = ((1,), (0,))
NT = ((1,), (1,))
TN = ((0,), (0,))
NEG = -1e30


def _dot(a, b, dims=NN, hi=False):
    if hi:
        return lax.dot_general(a, b, (dims, ((), ())), precision=HI, preferred_element_type=F32)
    return lax.dot_general(a.astype(BF16), b.astype(BF16), (dims, ((), ())), preferred_element_type=F32)


def _modnorm_kernel(x_ref, sh_ref, sc_ref, nw_ref, o_ref):
    x = x_ref[...]
    y = x * lax.rsqrt(jnp.mean(x * x, axis=-1, keepdims=True) + NORM_EPS) * nw_ref[...]
    o_ref[...] = (y * (1.0 + sc_ref[...]) + sh_ref[...]).astype(o_ref.dtype)


def modnorm(x, shift, scale, nw, gb, rb, dtype):
    G, R, D = x.shape
    xspec = pl.BlockSpec((gb, rb, D), lambda g, r: (g, r, 0))
    mspec = pl.BlockSpec((gb, 1, D), lambda g, r: (g, 0, 0))
    return pl.pallas_call(
        _modnorm_kernel, name="modnorm",
        grid=(G // gb, R // rb),
        in_specs=[xspec, mspec, mspec, pl.BlockSpec((1, 1, D), lambda g, r: (0, 0, 0))],
        out_specs=xspec,
        out_shape=jax.ShapeDtypeStruct(x.shape, dtype),
        compiler_params=_cparams(("parallel", "parallel")),
    )(x, shift, scale, nw.reshape(1, 1, D))


def _oproj_kernel(*refs, has_mul):
    if has_mul:
        y_ref, mul_ref, w_ref, x_ref, gt_ref, o_ref = refs
    else:
        y_ref, w_ref, x_ref, gt_ref, o_ref = refs
    gb, rb, kd = y_ref.shape
    y = y_ref[...]
    if has_mul:
        y = y * mul_ref[...]
    acc = jnp.dot(y.reshape(gb * rb, kd).astype(BF16), w_ref[...], preferred_element_type=F32)
    o_ref[...] = x_ref[...] + gt_ref[...] * acc.reshape(gb, rb, acc.shape[-1])


def oproj(y, mul, w_b, x, gate, gb, rb):
    G, R, Kd = y.shape
    D = x.shape[-1]
    yspec = pl.BlockSpec((gb, rb, Kd), lambda g, r: (g, r, 0))
    xspec = pl.BlockSpec((gb, rb, D), lambda g, r: (g, r, 0))
    ins = [y] + ([mul] if mul is not None else []) + [w_b, x, gate]
    specs = [yspec] + ([yspec] if mul is not None else []) + [
        pl.BlockSpec((Kd, D), lambda g, r: (0, 0)), xspec,
        pl.BlockSpec((gb, 1, D), lambda g, r: (g, 0, 0))]
    return pl.pallas_call(
        functools.partial(_oproj_kernel, has_mul=mul is not None), name="oproj",
        grid=(G // gb, R // rb),
        in_specs=specs,
        out_specs=xspec,
        out_shape=jax.ShapeDtypeStruct(x.shape, F32),
        compiler_params=_cparams(("parallel", "parallel")),
    )(*ins)


def _qkpost_kernel(q_ref, k_ref, cos_ref, sin_ref, qw_ref, kw_ref, qo_ref, ko_ref):
    gb, rb, w = q_ref.shape
    n = gb * rb
    lane = lax.broadcasted_iota(jnp.int32, (n, w), 1)
    lo = lane < MB_HEADDIM
    first = (lane % MB_HEADDIM) < (ROPE_DIM // 2)
    cos = jnp.broadcast_to(cos_ref[...][None], (gb, rb, w)).reshape(n, w)
    sin = jnp.broadcast_to(sin_ref[...][None], (gb, rb, w)).reshape(n, w)

    def norm_rope(x, wt):
        x2 = x * x
        s_lo = jnp.sum(jnp.where(lo, x2, 0.0), axis=-1, keepdims=True)
        s_hi = jnp.sum(jnp.where(lo, 0.0, x2), axis=-1, keepdims=True)
        ms = jnp.where(lo, s_lo, s_hi) * (1.0 / MB_HEADDIM)
        y = x * lax.rsqrt(ms + NORM_EPS) * wt
        up = pltpu.roll(y, w - ROPE_DIM // 2, 1)
        dn = pltpu.roll(y, ROPE_DIM // 2, 1)
        return y * cos + jnp.where(first, up, dn) * sin

    qo_ref[...] = norm_rope(q_ref[...].reshape(n, w), qw_ref[...]).reshape(gb, rb, w)
    ko_ref[...] = norm_rope(k_ref[...].reshape(n, w), kw_ref[...]).reshape(gb, rb, w)


def _rope_tables(pos):
    half = ROPE_DIM // 2
    inv = ROPE_THETA ** (-jnp.arange(half, dtype=F32) / half)
    ang = pos.astype(F32)[:, None] * inv
    c, s = jnp.cos(ang), jnp.sin(ang)
    ones = jnp.ones((pos.shape[0], MB_HEADDIM - ROPE_DIM), F32)
    ch = jnp.concatenate([c, c, ones], axis=-1)
    sh = jnp.concatenate([-s, s, 0.0 * ones], axis=-1)
    return jnp.concatenate([ch, ch], axis=-1), jnp.concatenate([sh, sh], axis=-1)


def qkpost(qkv, pos, q_norm, k_norm, gb, rb):
    G, R, _ = qkv.shape
    D = D_MODEL
    npair = D // LANE
    cos, sin = _rope_tables(pos)
    tspec = pl.BlockSpec((rb, LANE), lambda g, r, p: (r, 0))
    wspec = pl.BlockSpec((1, LANE), lambda g, r, p: (0, 0))
    ospec = pl.BlockSpec((gb, rb, LANE), lambda g, r, p: (g, r, p))
    return pl.pallas_call(
        _qkpost_kernel, name="qkpost",
        grid=(G // gb, R // rb, npair),
        in_specs=[pl.BlockSpec((gb, rb, LANE), lambda g, r, p: (g, r, p)),
                  pl.BlockSpec((gb, rb, LANE), lambda g, r, p: (g, r, p + npair)),
                  tspec, tspec, wspec, wspec],
        out_specs=[ospec, ospec],
        out_shape=[jax.ShapeDtypeStruct((G, R, D), F32)] * 2,
        compiler_params=_cparams(("parallel", "parallel", "parallel")),
    )(qkv, qkv, cos, sin, jnp.tile(q_norm, 2).reshape(1, LANE), jnp.tile(k_norm, 2).reshape(1, LANE))


def _top3_mask(g, idx, nb):
    sel = jnp.zeros(g.shape, jnp.bool_)
    for _ in range(MB_TOPK):
        mx = jnp.max(g, axis=-1, keepdims=True)
        first = jnp.min(jnp.where(g == mx, idx, nb), axis=-1, keepdims=True)
        pick = idx == first
        sel = jnp.logical_or(sel, pick)
        g = jnp.where(pick, -jnp.inf, g)
    return sel


def _moba_prompt_kernel(q_ref, k_ref, v_ref, o_ref, kmean_sc, selx_sc):
    c = pl.program_id(2)
    T = MB_BLOCK
    nb = k_ref.shape[1] // T
    scale = MB_HEADDIM ** -0.5

    @pl.when(c == 0)
    def _():
        for n in range(nb):
            kmean_sc[n:n + 1, :] = jnp.mean(k_ref[0, n * T:(n + 1) * T, :], axis=0, keepdims=True)

    q = q_ref[0]
    lo = lax.broadcasted_iota(jnp.int32, (T, LANE), 1) < MB_HEADDIM
    qh = (jnp.where(lo, q, 0.0), jnp.where(lo, 0.0, q))
    blk = lax.broadcasted_iota(jnp.int32, (T, nb), 1)
    for hh in range(2):
        g = _dot(qh[hh], kmean_sc[...], NT, hi=True)
        g = jnp.where(blk < c, g, -jnp.inf)
        sel = jnp.logical_and(_top3_mask(g, blk, nb), blk < c).astype(F32)
        for n in range(nb):
            selx_sc[hh, n] = jnp.broadcast_to(sel[:, n:n + 1], (T, LANE))
    qb = tuple((x * scale).astype(BF16) for x in qh)

    def blocks(n):
        st = pl.multiple_of(n * T, T)
        return k_ref[0, pl.ds(st, T), :].astype(BF16), v_ref[0, pl.ds(st, T), :].astype(BF16)

    kc, vc = blocks(c)
    causal = lax.broadcasted_iota(jnp.int32, (T, T), 0) >= lax.broadcasted_iota(jnp.int32, (T, T), 1)
    init = []
    for hh in range(2):
        s = jnp.where(causal, _dot(qb[hh], kc, NT), NEG)
        m = jnp.max(s, axis=-1, keepdims=True)
        p = jnp.exp(s - m)
        init += [m, jnp.sum(p, axis=-1, keepdims=True), _dot(p, vc)]

    def body(n, carry):
        kn, vn = blocks(n)
        out = []
        for hh in range(2):
            m, l, acc = carry[3 * hh:3 * hh + 3]
            keep = selx_sc[hh, n] > 0.5
            s = _dot(qb[hh], kn, NT)
            s0 = jnp.where(keep, s[:, :LANE], NEG)
            s1 = jnp.where(keep, s[:, LANE:], NEG)
            m_new = jnp.maximum(m, jnp.max(jnp.maximum(s0, s1), axis=-1, keepdims=True))
            alpha = jnp.exp(m - m_new)
            p0 = jnp.exp(s0 - m_new)
            p1 = jnp.exp(s1 - m_new)
            l = alpha * l + jnp.sum(p0 + p1, axis=-1, keepdims=True)
            acc = alpha * acc + _dot(p0, vn[:LANE]) + _dot(p1, vn[LANE:])
            out += [m_new, l, acc]
        return tuple(out)

    res = lax.fori_loop(0, c, body, tuple(init))
    o_ref[0] = jnp.where(lo, res[2] / res[1], res[5] / res[4])


def moba_prompt_attend(q, k, v):
    B, L, D = q.shape
    nb = L // MB_BLOCK
    kvspec = pl.BlockSpec((1, L, LANE), lambda b, p, c: (b, 0, p))
    qspec = pl.BlockSpec((1, MB_BLOCK, LANE), lambda b, p, c: (b, c, p))
    return pl.pallas_call(
        _moba_prompt_kernel, name="moba_prompt",
        grid=(B, D // LANE, nb),
        in_specs=[qspec, kvspec, kvspec],
        out_specs=qspec,
        out_shape=jax.ShapeDtypeStruct((B, L, D), F32),
        scratch_shapes=[pltpu.VMEM((nb, LANE), F32), pltpu.VMEM((2, nb, MB_BLOCK, LANE), F32)],
        compiler_params=_cparams(("parallel", "parallel", "arbitrary")),
    )(q, k, v)


def _moba_sample_kernel(pt_ref, q_ref, kn_ref, vn_ref, k0_ref, k1_ref, v0_ref, v1_ref, o_ref,
                        qbd_sc, qbdb_sc, kmean_sc, o_sc, m_sc, l_sc):
    del pt_ref
    n = pl.program_id(1)
    nblk = pl.num_programs(1)
    lq, d = q_ref.shape[1], q_ref.shape[2]
    rows = MB_HEADS * lq
    scale = MB_HEADDIM ** -0.5
    own = (lax.broadcasted_iota(jnp.int32, (MB_HEADS, lq, d), 0)
           == lax.broadcasted_iota(jnp.int32, (MB_HEADS, lq, d), 2) // MB_HEADDIM)

    @pl.when(n == 0)
    def _():
        qbd = jnp.where(own, q_ref[0][None], 0.0).reshape(rows, d)
        qbd_sc[...] = qbd
        qbdb_sc[...] = (qbd * scale).astype(BF16)

    qb = qbdb_sc[...]
    k0, k1 = k0_ref[0], k1_ref[0]
    s0 = _dot(qb, k0, NT)
    s1 = _dot(qb, k1, NT)
    m = jnp.maximum(jnp.max(s0, axis=-1, keepdims=True), jnp.max(s1, axis=-1, keepdims=True))
    p0 = jnp.exp(s0 - m)
    p1 = jnp.exp(s1 - m)
    o_sc[n] = _dot(p0, v0_ref[0]) + _dot(p1, v1_ref[0])
    m_sc[n] = m
    l_sc[n] = jnp.sum(p0, axis=-1, keepdims=True) + jnp.sum(p1, axis=-1, keepdims=True)
    kmean_sc[n] = (jnp.sum(k0, axis=0, keepdims=True) + jnp.sum(k1, axis=0, keepdims=True)) * (1.0 / MB_BLOCK)

    @pl.when(n == nblk - 1)
    def _():
        qbd = qbd_sc[...]
        nblocks = o_sc.shape[0]
        s = lax.dot_general(qbd * scale, kn_ref[0], (NT, ((), ())), preferred_element_type=F32)
        qi = lax.broadcasted_iota(jnp.int32, (rows, lq), 0) % lq
        s = jnp.where(lax.broadcasted_iota(jnp.int32, (rows, lq), 1) <= qi, s, NEG)
        m_own = jnp.max(s, axis=-1, keepdims=True)
        p = jnp.exp(s - m_own)
        l_own = jnp.sum(p, axis=-1, keepdims=True)
        o_own = jnp.dot(p, vn_ref[0], preferred_element_type=F32)
        g = [jnp.sum(qbd * kmean_sc[j], axis=-1, keepdims=True) for j in range(nblocks)]
        sel = []
        for j in range(nblocks):
            rank = jnp.zeros((rows, 1), jnp.int32)
            for i in range(nblocks):
                if i != j:
                    ahead = (g[i] >= g[j]) if i < j else (g[i] > g[j])
                    rank = rank + ahead.astype(jnp.int32)
            sel.append(rank < MB_TOPK)
        mx = m_own
        for j in range(nblocks):
            mx = jnp.where(sel[j], jnp.maximum(mx, m_sc[j]), mx)
        w_own = jnp.exp(m_own - mx)
        l = l_own * w_own
        o = o_own * w_own
        for j in range(nblocks):
            wj = jnp.where(sel[j], jnp.exp(m_sc[j] - mx), 0.0)
            l = l + wj * l_sc[j]
            o = o + wj * o_sc[j]
        res = (o / l).reshape(MB_HEADS, lq, d)
        o_ref[0] = jnp.sum(jnp.where(own, res, 0.0), axis=0)


def moba_sample_attend(q, k_new, v_new, cache_k, cache_v, page_table):
    Bs, Ls, D = q.shape
    page = cache_k.shape[1]
    assert MB_BLOCK == 2 * page and PAST_LEN % MB_BLOCK == 0 and Ls <= MB_BLOCK
    nblk = PAST_LEN // MB_BLOCK
    assert nblk >= MB_TOPK
    rows = MB_HEADS * Ls
    newspec = pl.BlockSpec((1, Ls, D), lambda b, n, pt: (b, 0, 0))
    pg0 = pl.BlockSpec((1, page, D), lambda b, n, pt: (pt[b, 2 * n], 0, 0))
    pg1 = pl.BlockSpec((1, page, D), lambda b, n, pt: (pt[b, 2 * n + 1], 0, 0))
    return pl.pallas_call(
        _moba_sample_kernel, name="moba_sample",
        grid_spec=pltpu.PrefetchScalarGridSpec(
            num_scalar_prefetch=1,
            grid=(Bs, nblk),
            in_specs=[newspec, newspec, newspec, pg0, pg1, pg0, pg1],
            out_specs=newspec,
            scratch_shapes=[pltpu.VMEM((rows, D), F32), pltpu.VMEM((rows, D), BF16),
                            pltpu.VMEM((nblk, 1, D), F32), pltpu.VMEM((nblk, rows, D), F32),
                            pltpu.VMEM((nblk, rows, 1), F32), pltpu.VMEM((nblk, rows, 1), F32)]),
        out_shape=jax.ShapeDtypeStruct((Bs, Ls, D), F32),
        compiler_params=_cparams(("parallel", "arbitrary")),
    )(page_table, q, k_new, v_new, cache_k, cache_k, cache_v, cache_v)


def rmsnorm(x, w):
    xf = x.astype(jnp.float32)
    y = xf * lax.rsqrt(jnp.mean(xf * xf, axis=-1, keepdims=True) + NORM_EPS)
    return (y * w).astype(x.dtype)


def sublayer_in(x, g, shift, scale):
    return rmsnorm(x, g) * (1 + scale) + shift


def causal_dwconv(x, buf, w, b):
    L, K = x.shape[1], w.shape[0]
    xp = jnp.concatenate([buf.astype(x.dtype), x], axis=1)
    y = b + sum(xp[:, k:k + L] * w[k] for k in range(K))
    return y, xp[:, -(K - 1):]


def ssd_scan(x, dt, a, bm, cm, s0):
    B, L = x.shape[0], x.shape[1]
    Q = math.gcd(L, SSD_CHUNK)
    nc = L // Q
    chunks = lambda t: jnp.moveaxis(t.reshape(B, nc, Q, *t.shape[2:]), 1, 0)
    causal = jnp.tril(jnp.ones((Q, Q), bool))[None, :, :, None, None]

    def step(s, inp):
        xc, dtc, bc, cc = inp
        xf, bf, cf = xc.astype(jnp.float32), bc.astype(jnp.float32), cc.astype(jnp.float32)
        cs = jnp.cumsum(dtc * a, axis=1)
        decay = jnp.exp(jnp.where(causal, cs[:, :, None] - cs[:, None], -jnp.inf))
        cb = jnp.einsum('btgn,bsgn->btsg', cf, bf)
        wgt = cb[..., None] * decay * dtc[:, None]
        y = jnp.einsum('btsgr,bsgrp->btgrp', wgt, xf)
        y = y + jnp.einsum('btgn,bgrpn->btgrp', cf, s) * jnp.exp(cs)[..., None]
        w_end = jnp.exp(cs[:, -1:] - cs) * dtc
        s_new = s * jnp.exp(cs[:, -1])[..., None, None] + jnp.einsum('bsgr,bsgn,bsgrp->bgrpn', w_end, bf, xf)
        return s_new, y

    s_fin, y = lax.scan(step, s0, (chunks(x), chunks(dt), chunks(bm), chunks(cm)))
    y = jnp.moveaxis(y, 0, 1).reshape(x.shape)
    return y, s_fin


def ssd_mixer(h, conv_buf, ssm0, w_in, conv_w, conv_b, dt_bias, a_log, d_skip, norm_w, w_out):
    B, L, _ = h.shape
    G, R, P, N = SSD_GROUPS, SSD_HPG, SSD_HEADDIM, SSD_STATE
    zxbcdt = mm(h, w_in)
    z = zxbcdt[..., :SSD_D_INNER]
    xbc = zxbcdt[..., SSD_D_INNER:SSD_D_INNER + SSD_CONV_DIM]
    dt_raw = zxbcdt[..., SSD_D_INNER + SSD_CONV_DIM:]
    xbc, new_conv = causal_dwconv(xbc, conv_buf, conv_w, conv_b)
    xbc = jax.nn.silu(xbc)
    xs = xbc[..., :SSD_D_INNER].reshape(B, L, G, R, P)
    bm = xbc[..., SSD_D_INNER:SSD_D_INNER + G * N].reshape(B, L, G, N)
    cm = xbc[..., SSD_D_INNER + G * N:].reshape(B, L, G, N)
    dt = jax.nn.softplus(dt_raw.astype(jnp.float32) + dt_bias).reshape(B, L, G, R)
    a = -jnp.exp(a_log.astype(jnp.float32)).reshape(G, R)
    y, s_fin = ssd_scan(xs, dt, a, bm, cm, ssm0.astype(jnp.float32).reshape(B, G, R, P, N))
    y = y + xs.astype(jnp.float32) * d_skip.reshape(G, R, 1)
    yz = y.reshape(B, L, SSD_D_INNER) * jax.nn.silu(z.astype(jnp.float32))
    yz = rmsnorm(yz.reshape(B, L, G, -1), norm_w.reshape(G, -1)).reshape(B, L, SSD_D_INNER)
    return mm(yz.astype(h.dtype), w_out), new_conv, s_fin.reshape(B, SSD_HEADS, P, N)


def rwkv7_mixer(h, shift_buf, wkv0, mu, w_r, w_k, w_v, w0, w1, w2, a0, a1, a2, g1, g2,
                k_k, k_a, r_k, lnx_w, lnx_b, w_o):
    B, L, D = h.shape
    H, N = RW_HEADS, RW_HEADDIM
    prev = jnp.concatenate([shift_buf[:, None, :].astype(h.dtype), h[:, :-1]], axis=1)
    xx = prev - h
    xr, xw, xk = h + xx * mu[0], h + xx * mu[1], h + xx * mu[2]
    xv, xa, xg = h + xx * mu[3], h + xx * mu[4], h + xx * mu[5]
    r, k, v = mm(xr, w_r), mm(xk, w_k), mm(xv, w_v)
    w_raw = (w0 + mm(jnp.tanh(mm(xw, w1)), w2)).astype(jnp.float32)
    decay = jnp.exp(-jnp.exp(-jax.nn.softplus(-w_raw) - 0.5))
    a = jax.nn.sigmoid(a0 + mm(mm(xa, a1), a2))
    g = mm(jax.nn.sigmoid(mm(xg, g1)), g2)
    heads = lambda t: t.reshape(B, L, H, N).astype(jnp.float32)
    kk = heads(k * k_k)
    kk = kk / jnp.maximum(jnp.sqrt(jnp.sum(kk * kk, axis=-1, keepdims=True)), 1e-12)
    k = k * (1 + (a - 1) * k_a)
    rh, kh, vh, ah, wh = heads(r), heads(k), heads(v), heads(a), heads(decay)
    tm = lambda t: jnp.moveaxis(t, 1, 0)

    def step(S, inp):
        r_t, w_t, k_t, v_t, kk_t, a_t = inp
        sa = jnp.einsum('bhvk,bhk->bhv', S, -kk_t)
        S = S * w_t[:, :, None, :] + sa[..., None] * (kk_t * a_t)[:, :, None, :] + v_t[..., None] * k_t[:, :, None, :]
        return S, jnp.einsum('bhvk,bhk->bhv', S, r_t)

    S_fin, y = lax.scan(step, wkv0.astype(jnp.float32), (tm(rh), tm(wh), tm(kh), tm(vh), tm(kk), tm(ah)))
    y = jnp.moveaxis(y, 0, 1)
    mean = jnp.mean(y, axis=-1, keepdims=True)
    var = jnp.mean(jnp.square(y - mean), axis=-1, keepdims=True)
    y = ((y - mean) * lax.rsqrt(var + RW_LNX_EPS)).reshape(B, L, D) * lnx_w + lnx_b
    bonus = jnp.sum(rh * kh * r_k, axis=-1, keepdims=True) * vh
    y = (y + bonus.reshape(B, L, D)).astype(h.dtype)
    return mm(y * g, w_o), h[:, -1], S_fin


def rope_partial(x, pos):
    half = ROPE_DIM // 2
    inv = ROPE_THETA ** (-jnp.arange(half, dtype=jnp.float32) / half)
    ang = pos.astype(jnp.float32)[:, None] * inv
    cos, sin = jnp.cos(ang)[:, None, :], jnp.sin(ang)[:, None, :]
    x1 = x[..., :half].astype(jnp.float32)
    x2 = x[..., half:ROPE_DIM].astype(jnp.float32)
    rot = jnp.concatenate([x1 * cos - x2 * sin, x2 * cos + x1 * sin], axis=-1).astype(x.dtype)
    return jnp.concatenate([rot, x[..., ROPE_DIM:]], axis=-1)


def moba_qkv(h, pos, w_qkv, q_norm, k_norm):
    B, L, _ = h.shape
    qkv = mm(h, w_qkv).reshape(B, L, 3, MB_HEADS, MB_HEADDIM)
    q = rope_partial(rmsnorm(qkv[:, :, 0], q_norm), pos)
    k = rope_partial(rmsnorm(qkv[:, :, 1], k_norm), pos)
    return q, k, qkv[:, :, 2]


def moba_attend_one(qpos, q, k_all, v_all):
    Lq, Lk = q.shape[0], k_all.shape[0]
    H, Dh, K = MB_HEADS, MB_HEADDIM, MB_TOPK
    nb = max(-(-Lk // MB_BLOCK), K)
    pad = nb * MB_BLOCK - Lk
    blocks = lambda t: jnp.pad(t, ((0, pad), (0, 0), (0, 0))).reshape(nb, MB_BLOCK, H, Dh).transpose(2, 0, 1, 3)
    kb, vb = blocks(k_all), blocks(v_all)
    kmean = jnp.mean(kb.astype(jnp.float32), axis=2)
    cur = qpos // MB_BLOCK
    gate = jnp.einsum('qhd,hnd->hqn', q.astype(jnp.float32), kmean)
    fully_past = jnp.arange(nb)[None, :] < cur[:, None]
    gate = jnp.where(fully_past[None], gate, -jnp.inf)
    _, top = lax.top_k(gate, K)
    sel_ok = jnp.arange(K)[None, :] < cur[:, None]
    idx = jnp.concatenate([top, jnp.broadcast_to(cur[None, :, None], (H, Lq, 1))], axis=-1)
    qb = math.gcd(Lq, MB_QBLOCK)
    nq = Lq // qb
    hidx = jnp.arange(H)[:, None, None]
    scale = MB_HEADDIM ** -0.5

    def one_block(args):
        qs, ps, ids, ok = args
        kg, vg = kb[hidx, ids], vb[hidx, ids]
        s = jnp.einsum('qhd,hqsbd->hqsb', qs, kg).astype(jnp.float32) * scale
        kpos = ids[..., None] * MB_BLOCK + jnp.arange(MB_BLOCK)
        valid = jnp.concatenate([ok, jnp.ones((qb, 1), bool)], axis=-1)
        mask = valid[None, :, :, None] & (kpos <= ps[None, :, None, None])
        s = jnp.where(mask, s, -jnp.inf)
        p = jax.nn.softmax(s.reshape(H, qb, -1), axis=-1).reshape(s.shape)
        return jnp.einsum('hqsb,hqsbd->qhd', p.astype(vg.dtype), vg)

    out = lax.map(one_block, (q.reshape(nq, qb, H, Dh), qpos.reshape(nq, qb),
                              idx.reshape(H, nq, qb, K + 1).transpose(1, 0, 2, 3), sel_ok.reshape(nq, qb, K)))
    return out.reshape(Lq, H, Dh)


def moba_sample_one(qpos, ck, cv, args):
    q, k_new, v_new, pt = args
    k_all = jnp.concatenate([ck[pt].reshape(-1, MB_HEADS, MB_HEADDIM).astype(k_new.dtype), k_new], axis=0)
    v_all = jnp.concatenate([cv[pt].reshape(-1, MB_HEADS, MB_HEADDIM).astype(v_new.dtype), v_new], axis=0)
    return moba_attend_one(qpos, q, k_all, v_all)


def mlstm_chunked(q, k, v, i_pre, log_f, c0, n0, m0):
    B, L = q.shape[0], q.shape[1]
    Q = math.gcd(L, ML_CHUNK)
    nc = L // Q
    chunks = lambda t: jnp.moveaxis(t.reshape(B, nc, Q, *t.shape[2:]), 1, 0)
    causal = jnp.tril(jnp.ones((Q, Q), bool))[None, :, :, None]

    def step(carry, inp):
        C, n, m = carry
        qc, kc, vc, ic, fc = inp
        b = jnp.cumsum(fc, axis=1)
        dmat = jnp.where(causal, b[:, :, None] - b[:, None] + ic[:, None], -jnp.inf)
        m_inter = b + m[:, None]
        m_t = jnp.maximum(m_inter, jnp.max(dmat, axis=2))
        w_intra = jnp.exp(dmat - m_t[:, :, None])
        w_inter = jnp.exp(m_inter - m_t)
        qk = jnp.einsum('bthd,bshd->btsh', qc, kc) * w_intra
        num = jnp.einsum('btsh,bshv->bthv', qk, vc) + w_inter[..., None] * jnp.einsum('bhvd,bthd->bthv', C, qc)
        den = jnp.sum(qk, axis=2) + w_inter * jnp.einsum('bhd,bthd->bth', n, qc)
        hout = num / jnp.maximum(jnp.abs(den), jnp.exp(-m_t))[..., None]
        m_last = m_t[:, -1]
        w_state = jnp.exp(b[:, -1:] - b + ic - m_last[:, None])
        decay = jnp.exp(b[:, -1] + m - m_last)
        C = decay[..., None, None] * C + jnp.einsum('bsh,bshv,bshd->bhvd', w_state, vc, kc)
        n = decay[..., None] * n + jnp.einsum('bsh,bshd->bhd', w_state, kc)
        return (C, n, m_last), hout

    carry0 = (c0.astype(jnp.float32), n0.astype(jnp.float32), m0.astype(jnp.float32))
    (C, n, m), hout = lax.scan(step, carry0, (chunks(q), chunks(k), chunks(v), chunks(i_pre), chunks(log_f)))
    return jnp.moveaxis(hout, 0, 1).reshape(v.shape), C, n, m


def mlstm_mixer(h, c0, n0, m0, w_in, b_if, norm_w, w_o):
    B, L, _ = h.shape
    H, DK, DV = ML_HEADS, ML_QK_DIM, ML_V_DIM
    proj = mm(h, w_in)
    o1, o2, o3, o4 = H * DK, 2 * H * DK, 2 * H * DK + H * DV, 2 * H * DK + 2 * H * DV
    q = proj[..., :o1].reshape(B, L, H, DK).astype(jnp.float32) * DK ** -0.5
    k = proj[..., o1:o2].reshape(B, L, H, DK).astype(jnp.float32)
    v = proj[..., o2:o3].reshape(B, L, H, DV).astype(jnp.float32)
    o = jax.nn.sigmoid(proj[..., o3:o4])
    gates = proj[..., o4:].astype(jnp.float32) + b_if
    i_pre, log_f = gates[..., :H], jax.nn.log_sigmoid(gates[..., H:])
    hh, C, n, m = mlstm_chunked(q, k, v, i_pre, log_f, c0, n0, m0)
    hh = rmsnorm(hh, norm_w.reshape(H, DV)).reshape(B, L, H * DV)
    return mm(o * hh.astype(h.dtype), w_o), C, n, m


def kernel(x_prompt, x_sample, state_ssd_conv, state_ssd, state_rwkv_shift, state_rwkv, cache_k, cache_v, state_mlstm_c, state_mlstm_n, state_mlstm_m, page_table, c_prompt, c_sample, ada_w, ada_b, norm_w, ffn_w_in, ffn_w_out, ssd_w_in, ssd_conv_w, ssd_conv_b, ssd_dt_bias, ssd_a_log, ssd_d, ssd_norm_w, ssd_w_out, rw_mu, rw_w_r, rw_w_k, rw_w_v, rw_w0, rw_w1, rw_w2, rw_a0, rw_a1, rw_a2, rw_g1, rw_g2, rw_k_k, rw_k_a, rw_r_k, rw_lnx_w, rw_lnx_b, rw_w_o, mb_w_qkv, mb_q_norm, mb_k_norm, mb_w_o, ml_w_in, ml_b_if, ml_norm_w, ml_w_o):
    B, S = x_prompt.shape[0], x_prompt.shape[1]
    Bs, Ls = x_sample.shape[0], x_sample.shape[1]
    D = D_MODEL
    pos_p = jnp.arange(S, dtype=jnp.int32)
    pos_s = PAST_LEN + jnp.arange(Ls, dtype=jnp.int32)
    xp, xs = x_prompt, x_sample
    outs = {}
    c_all = jnp.concatenate([c_prompt, c_sample], axis=0)
    sc_all = jax.nn.silu(c_all)
    for i in range(DEPTH):
        kind, j = i % N_MIXERS, i // N_MIXERS
        mod = (mm(sc_all, ada_w[i]) + ada_b[i]).reshape(B + Bs, N_MOD, 1, D)
        mp, ms = mod[:B], mod[B:]
        wi0, wo0 = ffn_w_in[i, 0].astype(BF16), ffn_w_out[i, 0].astype(BF16)
        wi1, wo1 = ffn_w_in[i, 1].astype(BF16), ffn_w_out[i, 1].astype(BF16)
        xp = ffn(xp, mp[:, 0], mp[:, 1], mp[:, 2], norm_w[i, 0], wi0, wo0, 1, 512)
        xs = ffn(xs, ms[:, 0], ms[:, 1], ms[:, 2], norm_w[i, 0], wi0, wo0, 64, Ls)
        if kind != 2:
            hp = sublayer_in(xp, norm_w[i, 1], mp[:, 3], mp[:, 4])
            hs = sublayer_in(xs, norm_w[i, 1], ms[:, 3], ms[:, 4])
        if kind == 0:
            w = (ssd_w_in[j], ssd_conv_w[j], ssd_conv_b[j], ssd_dt_bias[j], ssd_a_log[j], ssd_d[j], ssd_norm_w[j], ssd_w_out[j])
            yp, cp, sp = ssd_mixer(hp, jnp.zeros((B, SSD_CONV - 1, SSD_CONV_DIM), F32),
                                   jnp.zeros((B, SSD_HEADS, SSD_HEADDIM, SSD_STATE), F32), *w)
            ys, cs_, ss = ssd_mixer(hs, state_ssd_conv[j], state_ssd[j], *w)
            outs[2], outs[3], outs[4], outs[5] = cp[None], cs_[None], sp[None], ss[None]
        elif kind == 1:
            w = (rw_mu[j], rw_w_r[j], rw_w_k[j], rw_w_v[j], rw_w0[j], rw_w1[j], rw_w2[j], rw_a0[j], rw_a1[j],
                 rw_a2[j], rw_g1[j], rw_g2[j], rw_k_k[j], rw_k_a[j], rw_r_k[j], rw_lnx_w[j], rw_lnx_b[j], rw_w_o[j])
            yp, shp, sp = rwkv7_mixer(hp, jnp.zeros((B, D), F32),
                                      jnp.zeros((B, RW_HEADS, RW_HEADDIM, RW_HEADDIM), F32), *w)
            ys, shs, ss = rwkv7_mixer(hs, state_rwkv_shift[j], state_rwkv[j], *w)
            outs[6], outs[7], outs[8], outs[9] = shp[None], shs[None], sp[None], ss[None]
        elif kind == 2:
            hp = modnorm(xp, mp[:, 3], mp[:, 4], norm_w[i, 1], 1, 512, BF16)
            hs = modnorm(xs, ms[:, 3], ms[:, 4], norm_w[i, 1], 64, Ls, BF16)
            qkv_p, qkv_s = mm(hp, mb_w_qkv[j]), mm(hs, mb_w_qkv[j])
            qp, kp = qkpost(qkv_p, pos_p, mb_q_norm[j], mb_k_norm[j], 1, 512)
            qs, ks_ = qkpost(qkv_s, pos_s, mb_q_norm[j], mb_k_norm[j], 16, Ls)
            vp, vs = qkv_p[..., 2 * D:], qkv_s[..., 2 * D:]
            op = moba_prompt_attend(qp, kp, vp)
            npool, page = cache_k.shape[1], cache_k.shape[2]
            os_ = moba_sample_attend(qs, ks_, vs, cache_k[j].reshape(npool, page, D),
                                     cache_v[j].reshape(npool, page, D), page_table)
            w_o = mb_w_o[j].astype(BF16)
            xp = oproj(op, None, w_o, xp, mp[:, 5], 1, 512)
            xs = oproj(os_, None, w_o, xs, ms[:, 5], 64, Ls)
            hd = (MB_HEADS, MB_HEADDIM)
            outs[10], outs[11] = kp.reshape(1, B, S, *hd), ks_.reshape(1, Bs, Ls, *hd)
            outs[12], outs[13] = vp.reshape(1, B, S, *hd), vs.reshape(1, Bs, Ls, *hd)
        else:
            w = (ml_w_in[j], ml_b_if[j], ml_norm_w[j], ml_w_o[j])
            yp, cp, np_, mp_ = mlstm_mixer(hp, jnp.zeros((B, ML_HEADS, ML_V_DIM, ML_QK_DIM), F32),
                                           jnp.zeros((B, ML_HEADS, ML_QK_DIM), F32), jnp.zeros((B, ML_HEADS), F32), *w)
            ys, cs_, ns_, ms_ = mlstm_mixer(hs, state_mlstm_c[j], state_mlstm_n[j], state_mlstm_m[j], *w)
            outs[14], outs[15], outs[16], outs[17], outs[18], outs[19] = cp[None], cs_[None], np_[None], ns_[None], mp_[None], ms_[None]
        if kind != 2:
            xp = xp + mp[:, 5] * yp
            xs = xs + ms[:, 5] * ys
        xp = ffn(xp, mp[:, 6], mp[:, 7], mp[:, 8], norm_w[i, 2], wi1, wo1, 1, 512)
        xs = ffn(xs, ms[:, 6], ms[:, 7], ms[:, 8], norm_w[i, 2], wi1, wo1, 64, Ls)
    outs[0], outs[1] = xp, xs
    return tuple(outs[i] for i in range(20))
```

```python
import functools
import math

import jax
import jax.numpy as jnp
from jax import lax
from jax.experimental import pallas as pl
from jax.experimental.pallas import tpu as pltpu

D_MODEL = 1024
DEPTH = 4
PAST_LEN = 2048
N_MIXERS = 4
NORM_EPS = 1e-6
N_MOD = 9
D_FF = 2816

SSD_D_INNER = 2 * D_MODEL
SSD_HEADDIM = 64
SSD_HEADS = SSD_D_INNER // SSD_HEADDIM
SSD_GROUPS = 4
SSD_HPG = SSD_HEADS // SSD_GROUPS
SSD_STATE = 128
SSD_CONV = 4
SSD_CHUNK = 128
SSD_CONV_DIM = SSD_D_INNER + 2 * SSD_GROUPS * SSD_STATE

RW_HEADDIM = 64
RW_HEADS = D_MODEL // RW_HEADDIM
RW_LNX_EPS = 64e-5

MB_HEADS = 16
MB_HEADDIM = D_MODEL // MB_HEADS
MB_BLOCK = 256
MB_TOPK = 3
MB_QBLOCK = 16
ROPE_DIM = MB_HEADDIM // 4
ROPE_THETA = 500000.0

ML_HEADS = 8
ML_QK_DIM = D_MODEL // 2 // ML_HEADS
ML_V_DIM = D_MODEL // ML_HEADS
ML_CHUNK = 64

LANE = 128
VMEM_LIMIT = 48 * 1024 * 1024

BF16 = jnp.bfloat16
F32 = jnp.float32


def _cparams(sem):
    return pltpu.CompilerParams(dimension_semantics=sem, vmem_limit_bytes=VMEM_LIMIT)


def _mm_kernel(a_ref, w_ref, o_ref):
    o_ref[...] = jnp.dot(a_ref[...].astype(BF16), w_ref[...], preferred_element_type=F32)


def _pick(n, cands):
    for c in cands:
        if n % c == 0:
            return c
    raise ValueError(n)


def mm(a, w):
    lead = a.shape[:-1]
    K = a.shape[-1]
    N = w.shape[-1]
    a2 = a.reshape(-1, K)
    M = a2.shape[0]
    Mp = -(-M // 8) * 8
    if Mp != M:
        a2 = jnp.pad(a2, ((0, Mp - M), (0, 0)))
    Np = -(-N // LANE) * LANE
    wb = w.astype(BF16)
    if Np != N:
        wb = jnp.pad(wb, ((0, 0), (0, Np - N)))
    tm = _pick(Mp, (512, 256, 128, 64, 32, 16, 8))
    tn = _pick(Np, (1024, 512, 256, 128))
    out = pl.pallas_call(
        _mm_kernel, name="mm",
        grid=(Mp // tm, Np // tn),
        in_specs=[pl.BlockSpec((tm, K), lambda i, j: (i, 0)),
                  pl.BlockSpec((K, tn), lambda i, j: (0, j))],
        out_specs=pl.BlockSpec((tm, tn), lambda i, j: (i, j)),
        out_shape=jax.ShapeDtypeStruct((Mp, Np), F32),
        compiler_params=_cparams(("parallel", "arbitrary")),
    )(a2, wb)
    return out[:M, :N].reshape(*lead, N)


def _ffn_kernel(x_ref, sh_ref, sc_ref, gt_ref, nw_ref, wg_ref, wu_ref, wo_ref, o_ref, h_sc, acc_sc):
    k = pl.program_id(2)
    gb, rb, d = x_ref.shape

    @pl.when(k == 0)
    def _():
        x = x_ref[...]
        y = x * lax.rsqrt(jnp.mean(x * x, axis=-1, keepdims=True) + NORM_EPS) * nw_ref[...]
        h = y * (1.0 + sc_ref[...]) + sh_ref[...]
        h_sc[...] = h.reshape(gb * rb, d).astype(BF16)

    h = h_sc[...]
    g = jnp.dot(h, wg_ref[...], preferred_element_type=F32)
    u = jnp.dot(h, wu_ref[...], preferred_element_type=F32)
    a = (g * jax.nn.sigmoid(g) * u).astype(BF16)
    part = jnp.dot(a, wo_ref[...], preferred_element_type=F32)

    @pl.when(k == 0)
    def _():
        acc_sc[...] = part

    @pl.when(k > 0)
    def _():
        acc_sc[...] += part

    @pl.when(k == pl.num_programs(2) - 1)
    def _():
        o_ref[...] = x_ref[...] + 0.5 * gt_ref[...] * acc_sc[...].reshape(gb, rb, d)


def ffn(x, shift, scale, gate, nw, w_in_b, w_out_b, gb, rb):
    G, R, D = x.shape
    tf = D_FF // 2
    nk = D_FF // tf
    xspec = pl.BlockSpec((gb, rb, D), lambda g, r, k: (g, r, 0))
    mspec = pl.BlockSpec((gb, 1, D), lambda g, r, k: (g, 0, 0))
    return pl.pallas_call(
        _ffn_kernel, name="ffn",
        grid=(G // gb, R // rb, nk),
        in_specs=[xspec, mspec, mspec, mspec,
                  pl.BlockSpec((1, 1, D), lambda g, r, k: (0, 0, 0)),
                  pl.BlockSpec((D, tf), lambda g, r, k: (0, k)),
                  pl.BlockSpec((D, tf), lambda g, r, k: (0, k + nk)),
                  pl.BlockSpec((tf, D), lambda g, r, k: (k, 0))],
        out_specs=xspec,
        out_shape=jax.ShapeDtypeStruct(x.shape, F32),
        scratch_shapes=[pltpu.VMEM((gb * rb, D), BF16), pltpu.VMEM((gb * rb, D), F32)],
        compiler_params=_cparams(("parallel", "parallel", "arbitrary")),
    )(x, shift, scale, gate, nw.reshape(1, 1, D), w_in_b, w_in_b, w_out_b)


HI = lax.Precision.HIGHEST
NN = ((1,), (0,))
NT = ((1,), (1,))
TN = ((0,), (0,))
NEG = -1e30


def _dot(a, b, dims=NN, hi=False):
    if hi:
        return lax.dot_general(a, b, (dims, ((), ())), precision=HI, preferred_element_type=F32)
    return lax.dot_general(a.astype(BF16), b.astype(BF16), (dims, ((), ())), preferred_element_type=F32)


def _modnorm_kernel(x_ref, sh_ref, sc_ref, nw_ref, o_ref):
    x = x_ref[...]
    y = x * lax.rsqrt(jnp.mean(x * x, axis=-1, keepdims=True) + NORM_EPS) * nw_ref[...]
    o_ref[...] = (y * (1.0 + sc_ref[...]) + sh_ref[...]).astype(o_ref.dtype)


def modnorm(x, shift, scale, nw, gb, rb, dtype):
    G, R, D = x.shape
    xspec = pl.BlockSpec((gb, rb, D), lambda g, r: (g, r, 0))
    mspec = pl.BlockSpec((gb, 1, D), lambda g, r: (g, 0, 0))
    return pl.pallas_call(
        _modnorm_kernel, name="modnorm",
        grid=(G // gb, R // rb),
        in_specs=[xspec, mspec, mspec, pl.BlockSpec((1, 1, D), lambda g, r: (0, 0, 0))],
        out_specs=xspec,
        out_shape=jax.ShapeDtypeStruct(x.shape, dtype),
        compiler_params=_cparams(("parallel", "parallel")),
    )(x, shift, scale, nw.reshape(1, 1, D))


def _oproj_kernel(*refs, has_mul):
    if has_mul:
        y_ref, mul_ref, w_ref, x_ref, gt_ref, o_ref = refs
    else:
        y_ref, w_ref, x_ref, gt_ref, o_ref = refs
    gb, rb, kd = y_ref.shape
    y = y_ref[...]
    if has_mul:
        y = y * mul_ref[...]
    acc = jnp.dot(y.reshape(gb * rb, kd).astype(BF16), w_ref[...], preferred_element_type=F32)
    o_ref[...] = x_ref[...] + gt_ref[...] * acc.reshape(gb, rb, acc.shape[-1])


def oproj(y, mul, w_b, x, gate, gb, rb):
    G, R, Kd = y.shape
    D = x.shape[-1]
    yspec = pl.BlockSpec((gb, rb, Kd), lambda g, r: (g, r, 0))
    xspec = pl.BlockSpec((gb, rb, D), lambda g, r: (g, r, 0))
    ins = [y] + ([mul] if mul is not None else []) + [w_b, x, gate]
    specs = [yspec] + ([yspec] if mul is not None else []) + [
        pl.BlockSpec((Kd, D), lambda g, r: (0, 0)), xspec,
        pl.BlockSpec((gb, 1, D), lambda g, r: (g, 0, 0))]
    return pl.pallas_call(
        functools.partial(_oproj_kernel, has_mul=mul is not None), name="oproj",
        grid=(G // gb, R // rb),
        in_specs=specs,
        out_specs=xspec,
        out_shape=jax.ShapeDtypeStruct(x.shape, F32),
        compiler_params=_cparams(("parallel", "parallel")),
    )(*ins)


def _qkpost_kernel(q_ref, k_ref, cos_ref, sin_ref, qw_ref, kw_ref, qo_ref, ko_ref):
    gb, rb, w = q_ref.shape
    n = gb * rb
    lane = lax.broadcasted_iota(jnp.int32, (n, w), 1)
    lo = lane < MB_HEADDIM
    first = (lane % MB_HEADDIM) < (ROPE_DIM // 2)
    cos = jnp.broadcast_to(cos_ref[...][None], (gb, rb, w)).reshape(n, w)
    sin = jnp.broadcast_to(sin_ref[...][None], (gb, rb, w)).reshape(n, w)

    def norm_rope(x, wt):
        x2 = x * x
        s_lo = jnp.sum(jnp.where(lo, x2, 0.0), axis=-1, keepdims=True)
        s_hi = jnp.sum(jnp.where(lo, 0.0, x2), axis=-1, keepdims=True)
        ms = jnp.where(lo, s_lo, s_hi) * (1.0 / MB_HEADDIM)
        y = x * lax.rsqrt(ms + NORM_EPS) * wt
        up = pltpu.roll(y, w - ROPE_DIM // 2, 1)
        dn = pltpu.roll(y, ROPE_DIM // 2, 1)
        return y * cos + jnp.where(first, up, dn) * sin

    qo_ref[...] = norm_rope(q_ref[...].reshape(n, w), qw_ref[...]).reshape(gb, rb, w)
    ko_ref[...] = norm_rope(k_ref[...].reshape(n, w), kw_ref[...]).reshape(gb, rb, w)


def _rope_tables(pos):
    half = ROPE_DIM // 2
    inv = ROPE_THETA ** (-jnp.arange(half, dtype=F32) / half)
    ang = pos.astype(F32)[:, None] * inv
    c, s = jnp.cos(ang), jnp.sin(ang)
    ones = jnp.ones((pos.shape[0], MB_HEADDIM - ROPE_DIM), F32)
    ch = jnp.concatenate([c, c, ones], axis=-1)
    sh = jnp.concatenate([-s, s, 0.0 * ones], axis=-1)
    return jnp.concatenate([ch, ch], axis=-1), jnp.concatenate([sh, sh], axis=-1)


def qkpost(qkv, pos, q_norm, k_norm, gb, rb):
    G, R, _ = qkv.shape
    D = D_MODEL
    npair = D // LANE
    cos, sin = _rope_tables(pos)
    tspec = pl.BlockSpec((rb, LANE), lambda g, r, p: (r, 0))
    wspec = pl.BlockSpec((1, LANE), lambda g, r, p: (0, 0))
    ospec = pl.BlockSpec((gb, rb, LANE), lambda g, r, p: (g, r, p))
    return pl.pallas_call(
        _qkpost_kernel, name="qkpost",
        grid=(G // gb, R // rb, npair),
        in_specs=[pl.BlockSpec((gb, rb, LANE), lambda g, r, p: (g, r, p)),
                  pl.BlockSpec((gb, rb, LANE), lambda g, r, p: (g, r, p + npair)),
                  tspec, tspec, wspec, wspec],
        out_specs=[ospec, ospec],
        out_shape=[jax.ShapeDtypeStruct((G, R, D), F32)] * 2,
        compiler_params=_cparams(("parallel", "parallel", "parallel")),
    )(qkv, qkv, cos, sin, jnp.tile(q_norm, 2).reshape(1, LANE), jnp.tile(k_norm, 2).reshape(1, LANE))


def _top3_mask(g, idx, nb):
    sel = jnp.zeros(g.shape, jnp.bool_)
    for _ in range(MB_TOPK):
        mx = jnp.max(g, axis=-1, keepdims=True)
        first = jnp.min(jnp.where(g == mx, idx, nb), axis=-1, keepdims=True)
        pick = idx == first
        sel = jnp.logical_or(sel, pick)
        g = jnp.where(pick, -jnp.inf, g)
    return sel


def _moba_prompt_kernel(q_ref, k_ref, v_ref, o_ref, kmean_sc, selx_sc):
    c = pl.program_id(2)
    T = MB_BLOCK
    nb = k_ref.shape[1] // T
    scale = MB_HEADDIM ** -0.5

    @pl.when(c == 0)
    def _():
        for n in range(nb):
            kmean_sc[n:n + 1, :] = jnp.mean(k_ref[0, n * T:(n + 1) * T, :], axis=0, keepdims=True)

    q = q_ref[0]
    lo = lax.broadcasted_iota(jnp.int32, (T, LANE), 1) < MB_HEADDIM
    qh = (jnp.where(lo, q, 0.0), jnp.where(lo, 0.0, q))
    blk = lax.broadcasted_iota(jnp.int32, (T, nb), 1)
    for hh in range(2):
        g = _dot(qh[hh], kmean_sc[...], NT, hi=True)
        g = jnp.where(blk < c, g, -jnp.inf)
        sel = jnp.logical_and(_top3_mask(g, blk, nb), blk < c).astype(F32)
        for n in range(nb):
            selx_sc[hh, n] = jnp.broadcast_to(sel[:, n:n + 1], (T, LANE))
    qb = tuple((x * scale).astype(BF16) for x in qh)

    def blocks(n):
        st = pl.multiple_of(n * T, T)
        return k_ref[0, pl.ds(st, T), :].astype(BF16), v_ref[0, pl.ds(st, T), :].astype(BF16)

    kc, vc = blocks(c)
    causal = lax.broadcasted_iota(jnp.int32, (T, T), 0) >= lax.broadcasted_iota(jnp.int32, (T, T), 1)
    init = []
    for hh in range(2):
        s = jnp.where(causal, _dot(qb[hh], kc, NT), NEG)
        m = jnp.max(s, axis=-1, keepdims=True)
        p = jnp.exp(s - m)
        init += [m, jnp.sum(p, axis=-1, keepdims=True), _dot(p, vc)]

    def body(n, carry):
        kn, vn = blocks(n)
        out = []
        for hh in range(2):
            m, l, acc = carry[3 * hh:3 * hh + 3]
            keep = selx_sc[hh, n] > 0.5
            s = _dot(qb[hh], kn, NT)
            s0 = jnp.where(keep, s[:, :LANE], NEG)
            s1 = jnp.where(keep, s[:, LANE:], NEG)
            m_new = jnp.maximum(m, jnp.max(jnp.maximum(s0, s1), axis=-1, keepdims=True))
            alpha = jnp.exp(m - m_new)
            p0 = jnp.exp(s0 - m_new)
            p1 = jnp.exp(s1 - m_new)
            l = alpha * l + jnp.sum(p0 + p1, axis=-1, keepdims=True)
            acc = alpha * acc + _dot(p0, vn[:LANE]) + _dot(p1, vn[LANE:])
            out += [m_new, l, acc]
        return tuple(out)

    res = lax.fori_loop(0, c, body, tuple(init))
    o_ref[0] = jnp.where(lo, res[2] / res[1], res[5] / res[4])


def moba_prompt_attend(q, k, v):
    B, L, D = q.shape
    nb = L // MB_BLOCK
    kvspec = pl.BlockSpec((1, L, LANE), lambda b, p, c: (b, 0, p))
    qspec = pl.BlockSpec((1, MB_BLOCK, LANE), lambda b, p, c: (b, c, p))
    return pl.pallas_call(
        _moba_prompt_kernel, name="moba_prompt",
        grid=(B, D // LANE, nb),
        in_specs=[qspec, kvspec, kvspec],
        out_specs=qspec,
        out_shape=jax.ShapeDtypeStruct((B, L, D), F32),
        scratch_shapes=[pltpu.VMEM((nb, LANE), F32), pltpu.VMEM((2, nb, MB_BLOCK, LANE), F32)],
        compiler_params=_cparams(("parallel", "parallel", "arbitrary")),
    )(q, k, v)


def _moba_sample_kernel(pt_ref, q_ref, kn_ref, vn_ref, k0_ref, k1_ref, v0_ref, v1_ref, o_ref,
                        qbd_sc, qbdb_sc, kmean_sc, o_sc, m_sc, l_sc):
    del pt_ref
    n = pl.program_id(1)
    nblk = pl.num_programs(1)
    lq, d = q_ref.shape[1], q_ref.shape[2]
    rows = MB_HEADS * lq
    scale = MB_HEADDIM ** -0.5
    own = (lax.broadcasted_iota(jnp.int32, (MB_HEADS, lq, d), 0)
           == lax.broadcasted_iota(jnp.int32, (MB_HEADS, lq, d), 2) // MB_HEADDIM)

    @pl.when(n == 0)
    def _():
        qbd = jnp.where(own, q_ref[0][None], 0.0).reshape(rows, d)
        qbd_sc[...] = qbd
        qbdb_sc[...] = (qbd * scale).astype(BF16)

    qb = qbdb_sc[...]
    k0, k1 = k0_ref[0], k1_ref[0]
    s0 = _dot(qb, k0, NT)
    s1 = _dot(qb, k1, NT)
    m = jnp.maximum(jnp.max(s0, axis=-1, keepdims=True), jnp.max(s1, axis=-1, keepdims=True))
    p0 = jnp.exp(s0 - m)
    p1 = jnp.exp(s1 - m)
    o_sc[n] = _dot(p0, v0_ref[0]) + _dot(p1, v1_ref[0])
    m_sc[n] = m
    l_sc[n] = jnp.sum(p0, axis=-1, keepdims=True) + jnp.sum(p1, axis=-1, keepdims=True)
    kmean_sc[n] = (jnp.sum(k0, axis=0, keepdims=True) + jnp.sum(k1, axis=0, keepdims=True)) * (1.0 / MB_BLOCK)

    @pl.when(n == nblk - 1)
    def _():
        qbd = qbd_sc[...]
        nblocks = o_sc.shape[0]
        s = lax.dot_general(qbd * scale, kn_ref[0], (NT, ((), ())), preferred_element_type=F32)
        qi = lax.broadcasted_iota(jnp.int32, (rows, lq), 0) % lq
        s = jnp.where(lax.broadcasted_iota(jnp.int32, (rows, lq), 1) <= qi, s, NEG)
        m_own = jnp.max(s, axis=-1, keepdims=True)
        p = jnp.exp(s - m_own)
        l_own = jnp.sum(p, axis=-1, keepdims=True)
        o_own = jnp.dot(p, vn_ref[0], preferred_element_type=F32)
        g = [jnp.sum(qbd * kmean_sc[j], axis=-1, keepdims=True) for j in range(nblocks)]
        sel = []
        for j in range(nblocks):
            rank = jnp.zeros((rows, 1), jnp.int32)
            for i in range(nblocks):
                if i != j:
                    ahead = (g[i] >= g[j]) if i < j else (g[i] > g[j])
                    rank = rank + ahead.astype(jnp.int32)
            sel.append(rank < MB_TOPK)
        mx = m_own
        for j in range(nblocks):
            mx = jnp.where(sel[j], jnp.maximum(mx, m_sc[j]), mx)
        w_own = jnp.exp(m_own - mx)
        l = l_own * w_own
        o = o_own * w_own
        for j in range(nblocks):
            wj = jnp.where(sel[j], jnp.exp(m_sc[j] - mx), 0.0)
            l = l + wj * l_sc[j]
            o = o + wj * o_sc[j]
        res = (o / l).reshape(MB_HEADS, lq, d)
        o_ref[0] = jnp.sum(jnp.where(own, res, 0.0), axis=0)


def moba_sample_attend(q, k_new, v_new, cache_k, cache_v, page_table):
    Bs, Ls, D = q.shape
    page = cache_k.shape[1]
    assert MB_BLOCK == 2 * page and PAST_LEN % MB_BLOCK == 0 and Ls <= MB_BLOCK
    nblk = PAST_LEN // MB_BLOCK
    assert nblk >= MB_TOPK
    rows = MB_HEADS * Ls
    newspec = pl.BlockSpec((1, Ls, D), lambda b, n, pt: (b, 0, 0))
    pg0 = pl.BlockSpec((1, page, D), lambda b, n, pt: (pt[b, 2 * n], 0, 0))
    pg1 = pl.BlockSpec((1, page, D), lambda b, n, pt: (pt[b, 2 * n + 1], 0, 0))
    return pl.pallas_call(
        _moba_sample_kernel, name="moba_sample",
        grid_spec=pltpu.PrefetchScalarGridSpec(
            num_scalar_prefetch=1,
            grid=(Bs, nblk),
            in_specs=[newspec, newspec, newspec, pg0, pg1, pg0, pg1],
            out_specs=newspec,
            scratch_shapes=[pltpu.VMEM((rows, D), F32), pltpu.VMEM((rows, D), BF16),
                            pltpu.VMEM((nblk, 1, D), F32), pltpu.VMEM((nblk, rows, D), F32),
                            pltpu.VMEM((nblk, rows, 1), F32), pltpu.VMEM((nblk, rows, 1), F32)]),
        out_shape=jax.ShapeDtypeStruct((Bs, Ls, D), F32),
        compiler_params=_cparams(("parallel", "arbitrary")),
    )(page_table, q, k_new, v_new, cache_k, cache_k, cache_v, cache_v)


def _softplus(z):
    return jnp.maximum(z, 0.0) + jnp.log1p(jnp.exp(-jnp.abs(z)))


def _sigmoid(z):
    return 1.0 / (1.0 + jnp.exp(-z))


def _rwkv_pre_kernel(x_ref, halo_ref, sb_ref, sh_ref, sc_ref, nw_ref, mu_ref, w0_ref, a0_ref,
                     wr_ref, wk_ref, wv_ref, w1_ref, w2_ref, a1_ref, a2_ref, g1_ref, g2_ref,
                     r_o, lw_o, k_o, v_o, a_o, g_o, hl_o, hbuf):
    rr = pl.program_id(1)
    gb, rb, d = x_ref.shape
    n = gb * rb

    def hn(x):
        y = x * lax.rsqrt(jnp.mean(x * x, axis=-1, keepdims=True) + NORM_EPS) * nw_ref[...]
        return y * (1.0 + sc_ref[...]) + sh_ref[...]

    h = hn(x_ref[...])
    prev_row = jnp.where(rr == 0, sb_ref[...], hn(halo_ref[...])[:, 7:8, :])
    hbuf[:, 8:, :] = h
    hbuf[:, 7:8, :] = prev_row
    h2 = h.reshape(n, d)
    xx = hbuf[:, 7:7 + rb, :].reshape(n, d) - h2
    mix = lambda i: h2 + xx * mu_ref[i:i + 1, :]
    dot = lambda a, w_ref: jnp.dot(a.astype(BF16), w_ref[...], preferred_element_type=F32)
    out = lambda o_ref, val: o_ref.__setitem__(Ellipsis, val.reshape(gb, rb, d))
    out(r_o, dot(mix(0), wr_ref))
    out(k_o, dot(mix(2), wk_ref))
    out(v_o, dot(mix(3), wv_ref))
    w_raw = w0_ref[...] + dot(jnp.tanh(dot(mix(1), w1_ref)), w2_ref)
    out(lw_o, -jnp.exp(-_softplus(-w_raw) - 0.5))
    out(a_o, _sigmoid(a0_ref[...] + dot(dot(mix(4), a1_ref), a2_ref)))
    out(g_o, dot(_sigmoid(dot(mix(5), g1_ref)), g2_ref))
    hl_o[...] = h[:, rb - 1:rb, :]


def rwkv_pre(x, shift_buf, shift, scale, nw, mu, w0, a0, ws, gb, rb):
    G, R, D = x.shape
    assert gb == 1 or rb == R
    xspec = pl.BlockSpec((gb, rb, D), lambda g, r: (g, r, 0))
    halo = pl.BlockSpec((gb, 8, D), lambda g, r: (g, jnp.maximum(r * (rb // 8) - 1, 0), 0))
    mspec = pl.BlockSpec((gb, 1, D), lambda g, r: (g, 0, 0))
    vec = pl.BlockSpec((1, D), lambda g, r: (0, 0))
    full = lambda w: pl.BlockSpec(w.shape, lambda g, r: (0, 0))
    mu8 = jnp.pad(mu, ((0, 8 - mu.shape[0]), (0, 0)))
    big = jax.ShapeDtypeStruct((G, R, D), F32)
    return pl.pallas_call(
        _rwkv_pre_kernel, name="rwkv_pre",
        grid=(G // gb, R // rb),
        in_specs=[xspec, halo, mspec, mspec, mspec, vec, full(mu8), vec, vec] + [full(w) for w in ws],
        out_specs=[xspec] * 6 + [mspec],
        out_shape=[big] * 6 + [jax.ShapeDtypeStruct((G, 1, D), F32)],
        scratch_shapes=[pltpu.VMEM((gb, rb + 8, D), F32)],
        compiler_params=_cparams(("parallel", "arbitrary")),
    )(x, x, shift_buf.reshape(G, 1, D), shift, scale, nw.reshape(1, D), mu8, w0.reshape(1, D),
      a0.reshape(1, D), *ws)


def _rwkv_scan_kernel(r_ref, lw_ref, k_ref, v_ref, a_ref, kk_ref, ka_ref, rk_ref, lnw_ref, lnb_ref, s0_ref,
                      y_ref, so_ref, s_sc):
    c = pl.program_id(2)
    C = r_ref.shape[1]
    N = RW_HEADDIM

    @pl.when(c == 0)
    def _():
        s_sc[...] = s0_ref[0]

    ti = lax.broadcasted_iota(jnp.int32, (C, C), 0)
    si = lax.broadcasted_iota(jnp.int32, (C, C), 1)
    strict, incl = si < ti, si <= ti
    tri = incl.astype(F32)
    eye = (si == ti).astype(F32)
    hd = functools.partial(_dot, hi=True)
    ys = []
    for hh in range(2):
        sl = slice(hh * N, (hh + 1) * N)
        r, lw, kraw, v, a = (ref[0][:, sl] for ref in (r_ref, lw_ref, k_ref, v_ref, a_ref))
        S = s_sc[hh]
        kkr = kraw * kk_ref[:, sl]
        kk = kkr / jnp.maximum(jnp.sqrt(jnp.sum(kkr * kkr, axis=-1, keepdims=True)), 1e-12)
        k = kraw * (1.0 + (a - 1.0) * ka_ref[:, sl])
        cum = hd(tri, lw)
        e_in, e_ex, e_neg = jnp.exp(cum), jnp.exp(cum - lw), jnp.exp(-cum)
        ab, bb, kb, rb_ = -kk * e_ex, kk * a * e_neg, k * e_neg, r * e_in
        lab = jnp.where(strict, hd(ab, bb, NT), 0.0)
        lak = jnp.where(strict, hd(ab, kb, NT), 0.0)
        mrb = jnp.where(incl, hd(rb_, bb, NT), 0.0)
        mrk = jnp.where(incl, hd(rb_, kb, NT), 0.0)
        tinv, pw = eye + lab, lab
        for _ in range(int(math.log2(C)) - 1):
            pw = hd(pw, pw)
            tinv = tinv + hd(tinv, pw)
        u = hd(tinv, hd(ab, S, NT) + hd(lak, v))
        y = hd(rb_, S, NT) + hd(mrb, u) + hd(mrk, v)
        s_sc[hh] = (S + hd(u, bb, TN) + hd(v, kb, TN)) * e_in[C - 1:C, :]
        mean = jnp.mean(y, axis=-1, keepdims=True)
        var = jnp.mean(jnp.square(y - mean), axis=-1, keepdims=True)
        yn = (y - mean) * lax.rsqrt(var + RW_LNX_EPS) * lnw_ref[:, sl] + lnb_ref[:, sl]
        ys.append(yn + jnp.sum(r * k * rk_ref[:, sl], axis=-1, keepdims=True) * v)
    y_ref[0] = jnp.concatenate(ys, axis=-1)

    @pl.when(c == pl.num_programs(2) - 1)
    def _():
        so_ref[0] = s_sc[...]


def rwkv_scan(r, lw, k, v, a, k_k, k_a, r_k, lnx_w, lnx_b, s0, C):
    B, L, D = r.shape
    N = RW_HEADDIM
    assert L % C == 0 and C & (C - 1) == 0
    tspec = pl.BlockSpec((1, C, LANE), lambda b, p, c: (b, c, p))
    pspec = pl.BlockSpec((1, LANE), lambda b, p, c: (0, p))
    sspec = pl.BlockSpec((1, 2, N, N), lambda b, p, c: (b, p, 0, 0))
    vec = lambda t: t.reshape(1, D)
    return pl.pallas_call(
        _rwkv_scan_kernel, name="rwkv_scan",
        grid=(B, D // LANE, L // C),
        in_specs=[tspec] * 5 + [pspec] * 5 + [sspec],
        out_specs=[tspec, sspec],
        out_shape=[jax.ShapeDtypeStruct((B, L, D), F32), jax.ShapeDtypeStruct(s0.shape, F32)],
        scratch_shapes=[pltpu.VMEM((2, N, N), F32)],
        compiler_params=_cparams(("parallel", "parallel", "arbitrary")),
    )(r, lw, k, v, a, vec(k_k), vec(k_a), vec(r_k), vec(lnx_w), vec(lnx_b), s0)


def rwkv_layer(x, shift, scale, gate, nw, shift_buf, wkv0, mu, w_r, w_k, w_v, w0, w1, w2, a0, a1, a2, g1, g2,
               k_k, k_a, r_k, lnx_w, lnx_b, w_o, gb, rb, C):
    ws = [w.astype(BF16) for w in (w_r, w_k, w_v, w1, w2, a1, a2, g1, g2)]
    r, lw, k, v, a, g, hl = rwkv_pre(x, shift_buf, shift, scale, nw, mu, w0, a0, ws, gb, rb)
    y, s_fin = rwkv_scan(r, lw, k, v, a, k_k, k_a, r_k, lnx_w, lnx_b, wkv0, C)
    return oproj(y, g, w_o.astype(BF16), x, gate, gb, rb), hl[:, 0], s_fin


def _conv_kernel(x_ref, halo_ref, buf_ref, w_ref, b_ref, o_ref, sc):
    rr = pl.program_id(1)
    gb, rb, cw = x_ref.shape
    sc[:, :8, :] = jnp.where(rr == 0, buf_ref[...], halo_ref[...])
    sc[:, 8:, :] = x_ref[...]
    y = b_ref[...]
    for t in range(SSD_CONV):
        y = y + sc[:, 8 - (SSD_CONV - 1) + t:8 - (SSD_CONV - 1) + t + rb, :] * w_ref[t:t + 1, :]
    o_ref[...] = y * _sigmoid(y)


def ssd_conv(zx, buf, w, b, gb, rb, cw=512):
    G, R, _ = zx.shape
    assert gb == 1 or rb == R
    off = SSD_D_INNER // cw
    buf8 = jnp.pad(buf, ((0, 0), (8 - (SSD_CONV - 1), 0), (0, 0)))
    w8 = jnp.pad(w, ((0, 8 - SSD_CONV), (0, 0)))
    return pl.pallas_call(
        _conv_kernel, name="ssd_conv",
        grid=(G // gb, R // rb, SSD_CONV_DIM // cw),
        in_specs=[pl.BlockSpec((gb, rb, cw), lambda g, r, j: (g, r, off + j)),
                  pl.BlockSpec((gb, 8, cw), lambda g, r, j: (g, jnp.maximum(r * (rb // 8) - 1, 0), off + j)),
                  pl.BlockSpec((gb, 8, cw), lambda g, r, j: (g, 0, j)),
                  pl.BlockSpec((8, cw), lambda g, r, j: (0, j)),
                  pl.BlockSpec((1, cw), lambda g, r, j: (0, j))],
        out_specs=pl.BlockSpec((gb, rb, cw), lambda g, r, j: (g, r, j)),
        out_shape=jax.ShapeDtypeStruct((G, R, SSD_CONV_DIM), F32),
        scratch_shapes=[pltpu.VMEM((gb, rb + 8, cw), F32)],
        compiler_params=_cparams(("parallel", "arbitrary", "arbitrary")),
    )(zx, zx, buf8, w8, b.reshape(1, SSD_CONV_DIM))


def _col2row(col, eye):
    return jnp.sum(jnp.where(eye, col, 0.0), axis=0, keepdims=True)


def _ssd_scan_kernel(x_ref, bm_ref, cm_ref, z_ref, dt_ref, dtb_ref, alog_ref, dsk_ref, nw_ref, s0_ref,
                     y_ref, so_ref, s_sc):
    c = pl.program_id(2)
    Q = x_ref.shape[1]
    P, N, R = SSD_HEADDIM, SSD_STATE, SSD_HPG

    @pl.when(c == 0)
    def _():
        s_sc[...] = s0_ref[0]

    ti = lax.broadcasted_iota(jnp.int32, (Q, Q), 0)
    si = lax.broadcasted_iota(jnp.int32, (Q, Q), 1)
    causal, eye = si <= ti, si == ti
    lo = lax.broadcasted_iota(jnp.int32, (Q, LANE), 1) < P
    dt = _softplus(dt_ref[0] + dtb_ref[...])
    cs = _dot(causal.astype(F32), dt * (-jnp.exp(alog_ref[...])), hi=True)
    x, bm, cm = x_ref[0], bm_ref[0], cm_ref[0]
    cb = _dot(cm, bm, NT)
    s_all = s_sc[...].reshape(R * P, N)
    ys = _dot(cm, s_all, NT)
    y_pairs, xw_pairs, e_last = [], [], []
    for j in range(R // 2):
        xp = x[:, j * LANE:(j + 1) * LANE]
        yh, grow, wend = [], [], []
        for r in (2 * j, 2 * j + 1):
            cs_col, dt_col = cs[:, r:r + 1], dt[:, r:r + 1]
            cs_last = cs[Q - 1:Q, r:r + 1]
            dec = jnp.exp(jnp.where(causal, cs_col - _col2row(cs_col, eye), -jnp.inf))
            yh.append(_dot(cb * dec * _col2row(dt_col, eye), xp))
            grow.append(jnp.exp(cs_col))
            wend.append(jnp.exp(cs_last - cs_col) * dt_col)
            e_last.append(jnp.exp(cs_last))
        yp = jnp.where(lo, yh[0], yh[1]) + ys[:, j * LANE:(j + 1) * LANE] * jnp.where(lo, grow[0], grow[1])
        y_pairs.append(yp + xp * dsk_ref[:, j * LANE:(j + 1) * LANE])
        xw_pairs.append(xp * jnp.where(lo, wend[0], wend[1]))
    upd = _dot(jnp.concatenate(xw_pairs, axis=-1), bm, TN)
    for r in range(R):
        s_sc[r] = s_sc[r] * e_last[r] + upd[r * P:(r + 1) * P, :]
    z = z_ref[0]
    yz = jnp.concatenate(y_pairs, axis=-1) * (z * _sigmoid(z))
    y_ref[0] = yz * lax.rsqrt(jnp.mean(yz * yz, axis=-1, keepdims=True) + NORM_EPS) * nw_ref[...]

    @pl.when(c == pl.num_programs(2) - 1)
    def _():
        so_ref[0] = s_sc[...]


def ssd_chunk_scan(xbc, zx, dt_raw, dt_bias, a_log, d_skip, norm_w, s0, Q):
    B, L, _ = xbc.shape
    G, R, P, N = SSD_GROUPS, SSD_HPG, SSD_HEADDIM, SSD_STATE
    gw = R * P
    assert L % Q == 0 and N == LANE
    pad_heads = lambda t: jnp.pad(t.reshape(G, R), ((0, 0), (0, LANE - R))).reshape(1, G * LANE)
    xs = pl.BlockSpec((1, Q, gw), lambda b, g, c: (b, c, g))
    bs = pl.BlockSpec((1, Q, N), lambda b, g, c: (b, c, SSD_D_INNER // N + g))
    cs = pl.BlockSpec((1, Q, N), lambda b, g, c: (b, c, SSD_D_INNER // N + G + g))
    ds = pl.BlockSpec((1, Q, LANE), lambda b, g, c: (b, c, g))
    hs = pl.BlockSpec((1, LANE), lambda b, g, c: (0, g))
    ws = pl.BlockSpec((1, gw), lambda b, g, c: (0, g))
    ss = pl.BlockSpec((1, R, P, N), lambda b, g, c: (b, g, 0, 0))
    return pl.pallas_call(
        _ssd_scan_kernel, name="ssd_scan",
        grid=(B, G, L // Q),
        in_specs=[xs, bs, cs, xs, ds, hs, hs, ws, ws, ss],
        out_specs=[xs, ss],
        out_shape=[jax.ShapeDtypeStruct((B, L, SSD_D_INNER), F32), jax.ShapeDtypeStruct(s0.shape, F32)],
        scratch_shapes=[pltpu.VMEM((R, P, N), F32)],
        compiler_params=_cparams(("parallel", "parallel", "arbitrary")),
    )(xbc, xbc, xbc, zx, dt_raw, pad_heads(dt_bias), pad_heads(a_log),
      jnp.repeat(d_skip, P).reshape(1, SSD_D_INNER), norm_w.reshape(1, SSD_D_INNER), s0)


def ssd_layer(x, shift, scale, gate, nw, conv_buf, ssm0, w_in, conv_w, conv_b, dt_bias, a_log, d_skip, norm_w,
              w_out, gb, rb, Q):
    G, R, D = x.shape
    nzx = SSD_D_INNER + SSD_CONV_DIM
    h = modnorm(x, shift, scale, nw, gb, rb, BF16)
    zx = mm(h, w_in[:, :nzx])
    w_dt = jnp.pad(w_in[:, nzx:].reshape(D, SSD_GROUPS, SSD_HPG), ((0, 0), (0, 0), (0, LANE - SSD_HPG)))
    dt_raw = mm(h, w_dt.reshape(D, SSD_GROUPS * LANE))
    xbc = ssd_conv(zx, conv_buf, conv_w, conv_b, gb, rb)
    y, s_fin = ssd_chunk_scan(xbc, zx, dt_raw, dt_bias, a_log, d_skip, norm_w, ssm0, Q)
    new_conv = zx[:, R - (SSD_CONV - 1):, SSD_D_INNER:]
    return oproj(y, None, w_out.astype(BF16), x, gate, gb, rb), new_conv, s_fin


def _mlstm_kernel(q_ref, k_ref, v_ref, o_ref, g_ref, bif_ref, nw_ref, c0_ref, n0_ref, m0_ref,
                  y_ref, co_ref, no_ref, mo_ref, c_sc, n_sc, m_sc):
    c = pl.program_id(2)
    Q = q_ref.shape[1]
    DK, DV = ML_QK_DIM, ML_V_DIM

    @pl.when(c == 0)
    def _():
        c_sc[...] = c0_ref[0]
        n_sc[...] = n0_ref[0, 0]
        m_sc[...] = m0_ref[0, 0]

    ti = lax.broadcasted_iota(jnp.int32, (Q, Q), 0)
    si = lax.broadcasted_iota(jnp.int32, (Q, Q), 1)
    causal, eye = si <= ti, si == ti
    gates = g_ref[0] + bif_ref[...]
    bcum = _dot(causal.astype(F32), -_softplus(-gates), hi=True)
    ys = []
    for hh in range(2):
        i_col, b_col = gates[:, hh:hh + 1], bcum[:, 2 + hh:3 + hh]
        m_prev = m_sc[:, hh:hh + 1]
        q = q_ref[0][:, hh * DK:(hh + 1) * DK] * DK ** -0.5
        k = k_ref[0][:, hh * DK:(hh + 1) * DK]
        v = v_ref[0][:, hh * DV:(hh + 1) * DV]
        cst, nst = c_sc[hh], n_sc[hh:hh + 1, :]
        dmat = jnp.where(causal, b_col - _col2row(b_col, eye) + _col2row(i_col, eye), -jnp.inf)
        m_inter = b_col + m_prev
        m_t = jnp.maximum(m_inter, jnp.max(dmat, axis=-1, keepdims=True))
        w_inter = jnp.exp(m_inter - m_t)
        qk = _dot(q, k, NT) * jnp.exp(dmat - m_t)
        num = _dot(qk, v) + w_inter * _dot(q, cst, NT)
        den = jnp.sum(qk, axis=-1, keepdims=True) + w_inter * jnp.sum(q * nst, axis=-1, keepdims=True)
        hout = num / jnp.maximum(jnp.abs(den), jnp.exp(-m_t))
        m_last, b_last = m_t[Q - 1:Q], b_col[Q - 1:Q]
        w_state = jnp.exp(b_last - b_col + i_col - m_last)
        decay = jnp.exp(b_last + m_prev - m_last)
        c_sc[hh] = decay * cst + _dot(v * w_state, k, TN)
        n_sc[hh:hh + 1, :] = decay * nst + jnp.sum(w_state * k, axis=0, keepdims=True)
        m_sc[:, hh:hh + 1] = m_last
        hn = hout * lax.rsqrt(jnp.mean(hout * hout, axis=-1, keepdims=True) + NORM_EPS)
        ys.append(_sigmoid(o_ref[0][:, hh * DV:(hh + 1) * DV]) * (hn * nw_ref[:, hh * DV:(hh + 1) * DV]))
    y_ref[0] = jnp.concatenate(ys, axis=-1)

    @pl.when(c == pl.num_programs(2) - 1)
    def _():
        co_ref[0] = c_sc[...]
        no_ref[0, 0] = n_sc[...]
        mo_ref[0, 0] = m_sc[...]


def mlstm_scan(proj, gates, b_if, norm_w, c0, n0, m0, Q):
    B, L, _ = proj.shape
    H, DK, DV = ML_HEADS, ML_QK_DIM, ML_V_DIM
    assert L % Q == 0 and 2 * DK == LANE
    npair = H // 2
    qs = pl.BlockSpec((1, Q, LANE), lambda b, j, c: (b, c, j))
    ks = pl.BlockSpec((1, Q, LANE), lambda b, j, c: (b, c, npair + j))
    vs = pl.BlockSpec((1, Q, 2 * DV), lambda b, j, c: (b, c, H * DK // DV + j))
    os_ = pl.BlockSpec((1, Q, 2 * DV), lambda b, j, c: (b, c, H * DK // DV + npair + j))
    ys = pl.BlockSpec((1, Q, 2 * DV), lambda b, j, c: (b, c, j))
    cs = pl.BlockSpec((1, 2, DV, DK), lambda b, j, c: (b, j, 0, 0))
    ns = pl.BlockSpec((1, 1, 2, DK), lambda b, j, c: (b, j, 0, 0))
    ms = pl.BlockSpec((1, 1, 1, 2), lambda b, j, c: (b, j, 0, 0))
    bi, bf = b_if[:H].reshape(npair, 2), b_if[H:].reshape(npair, 2)
    bias = jnp.pad(jnp.concatenate([bi, bf], axis=-1), ((0, 0), (0, LANE - 4))).reshape(1, npair * LANE)
    y, c, n, m = pl.pallas_call(
        _mlstm_kernel, name="mlstm",
        grid=(B, npair, L // Q),
        in_specs=[qs, ks, vs, os_, qs, pl.BlockSpec((1, LANE), lambda b, j, c: (0, j)),
                  pl.BlockSpec((1, 2 * DV), lambda b, j, c: (0, j)), cs, ns, ms],
        out_specs=[ys, cs, ns, ms],
        out_shape=[jax.ShapeDtypeStruct((B, L, H * DV), F32), jax.ShapeDtypeStruct((B, H, DV, DK), F32),
                   jax.ShapeDtypeStruct((B, npair, 2, DK), F32), jax.ShapeDtypeStruct((B, npair, 1, 2), F32)],
        scratch_shapes=[pltpu.VMEM((2, DV, DK), F32), pltpu.VMEM((2, DK), F32), pltpu.VMEM((1, 2), F32)],
        compiler_params=_cparams(("parallel", "parallel", "arbitrary")),
    )(proj, proj, proj, proj, gates, bias, norm_w.reshape(1, H * DV), c0,
      n0.reshape(B, npair, 2, DK), m0.reshape(B, npair, 1, 2))
    return y, c, n.reshape(B, H, DK), m.reshape(B, H)


def mlstm_layer(x, shift, scale, gate, nw, c0, n0, m0, w_in, b_if, norm_w, w_o, gb, rb, Q):
    D = x.shape[-1]
    H = ML_HEADS
    nmain = 2 * H * ML_QK_DIM + 2 * H * ML_V_DIM
    h = modnorm(x, shift, scale, nw, gb, rb, BF16)
    proj = mm(h, w_in[:, :nmain])
    wi, wf = w_in[:, nmain:nmain + H].reshape(D, H // 2, 2), w_in[:, nmain + H:].reshape(D, H // 2, 2)
    wg = jnp.pad(jnp.concatenate([wi, wf], axis=-1), ((0, 0), (0, 0), (0, LANE - 4)))
    gates = mm(h, wg.reshape(D, (H // 2) * LANE))
    y, c, n, m = mlstm_scan(proj, gates, b_if, norm_w, c0, n0, m0, Q)
    return oproj(y, None, w_o.astype(BF16), x, gate, gb, rb), c, n, m


def rmsnorm(x, w):
    xf = x.astype(jnp.float32)
    y = xf * lax.rsqrt(jnp.mean(xf * xf, axis=-1, keepdims=True) + NORM_EPS)
    return (y * w).astype(x.dtype)


def sublayer_in(x, g, shift, scale):
    return rmsnorm(x, g) * (1 + scale) + shift


def causal_dwconv(x, buf, w, b):
    L, K = x.shape[1], w.shape[0]
    xp = jnp.concatenate([buf.astype(x.dtype), x], axis=1)
    y = b + sum(xp[:, k:k + L] * w[k] for k in range(K))
    return y, xp[:, -(K - 1):]


def ssd_scan(x, dt, a, bm, cm, s0):
    B, L = x.shape[0], x.shape[1]
    Q = math.gcd(L, SSD_CHUNK)
    nc = L // Q
    chunks = lambda t: jnp.moveaxis(t.reshape(B, nc, Q, *t.shape[2:]), 1, 0)
    causal = jnp.tril(jnp.ones((Q, Q), bool))[None, :, :, None, None]

    def step(s, inp):
        xc, dtc, bc, cc = inp
        xf, bf, cf = xc.astype(jnp.float32), bc.astype(jnp.float32), cc.astype(jnp.float32)
        cs = jnp.cumsum(dtc * a, axis=1)
        decay = jnp.exp(jnp.where(causal, cs[:, :, None] - cs[:, None], -jnp.inf))
        cb = jnp.einsum('btgn,bsgn->btsg', cf, bf)
        wgt = cb[..., None] * decay * dtc[:, None]
        y = jnp.einsum('btsgr,bsgrp->btgrp', wgt, xf)
        y = y + jnp.einsum('btgn,bgrpn->btgrp', cf, s) * jnp.exp(cs)[..., None]
        w_end = jnp.exp(cs[:, -1:] - cs) * dtc
        s_new = s * jnp.exp(cs[:, -1])[..., None, None] + jnp.einsum('bsgr,bsgn,bsgrp->bgrpn', w_end, bf, xf)
        return s_new, y

    s_fin, y = lax.scan(step, s0, (chunks(x), chunks(dt), chunks(bm), chunks(cm)))
    y = jnp.moveaxis(y, 0, 1).reshape(x.shape)
    return y, s_fin


def ssd_mixer(h, conv_buf, ssm0, w_in, conv_w, conv_b, dt_bias, a_log, d_skip, norm_w, w_out):
    B, L, _ = h.shape
    G, R, P, N = SSD_GROUPS, SSD_HPG, SSD_HEADDIM, SSD_STATE
    zxbcdt = mm(h, w_in)
    z = zxbcdt[..., :SSD_D_INNER]
    xbc = zxbcdt[..., SSD_D_INNER:SSD_D_INNER + SSD_CONV_DIM]
    dt_raw = zxbcdt[..., SSD_D_INNER + SSD_CONV_DIM:]
    xbc, new_conv = causal_dwconv(xbc, conv_buf, conv_w, conv_b)
    xbc = jax.nn.silu(xbc)
    xs = xbc[..., :SSD_D_INNER].reshape(B, L, G, R, P)
    bm = xbc[..., SSD_D_INNER:SSD_D_INNER + G * N].reshape(B, L, G, N)
    cm = xbc[..., SSD_D_INNER + G * N:].reshape(B, L, G, N)
    dt = jax.nn.softplus(dt_raw.astype(jnp.float32) + dt_bias).reshape(B, L, G, R)
    a = -jnp.exp(a_log.astype(jnp.float32)).reshape(G, R)
    y, s_fin = ssd_scan(xs, dt, a, bm, cm, ssm0.astype(jnp.float32).reshape(B, G, R, P, N))
    y = y + xs.astype(jnp.float32) * d_skip.reshape(G, R, 1)
    yz = y.reshape(B, L, SSD_D_INNER) * jax.nn.silu(z.astype(jnp.float32))
    yz = rmsnorm(yz.reshape(B, L, G, -1), norm_w.reshape(G, -1)).reshape(B, L, SSD_D_INNER)
    return mm(yz.astype(h.dtype), w_out), new_conv, s_fin.reshape(B, SSD_HEADS, P, N)


def rwkv7_mixer(h, shift_buf, wkv0, mu, w_r, w_k, w_v, w0, w1, w2, a0, a1, a2, g1, g2,
                k_k, k_a, r_k, lnx_w, lnx_b, w_o):
    B, L, D = h.shape
    H, N = RW_HEADS, RW_HEADDIM
    prev = jnp.concatenate([shift_buf[:, None, :].astype(h.dtype), h[:, :-1]], axis=1)
    xx = prev - h
    xr, xw, xk = h + xx * mu[0], h + xx * mu[1], h + xx * mu[2]
    xv, xa, xg = h + xx * mu[3], h + xx * mu[4], h + xx * mu[5]
    r, k, v = mm(xr, w_r), mm(xk, w_k), mm(xv, w_v)
    w_raw = (w0 + mm(jnp.tanh(mm(xw, w1)), w2)).astype(jnp.float32)
    decay = jnp.exp(-jnp.exp(-jax.nn.softplus(-w_raw) - 0.5))
    a = jax.nn.sigmoid(a0 + mm(mm(xa, a1), a2))
    g = mm(jax.nn.sigmoid(mm(xg, g1)), g2)
    heads = lambda t: t.reshape(B, L, H, N).astype(jnp.float32)
    kk = heads(k * k_k)
    kk = kk / jnp.maximum(jnp.sqrt(jnp.sum(kk * kk, axis=-1, keepdims=True)), 1e-12)
    k = k * (1 + (a - 1) * k_a)
    rh, kh, vh, ah, wh = heads(r), heads(k), heads(v), heads(a), heads(decay)
    tm = lambda t: jnp.moveaxis(t, 1, 0)

    def step(S, inp):
        r_t, w_t, k_t, v_t, kk_t, a_t = inp
        sa = jnp.einsum('bhvk,bhk->bhv', S, -kk_t)
        S = S * w_t[:, :, None, :] + sa[..., None] * (kk_t * a_t)[:, :, None, :] + v_t[..., None] * k_t[:, :, None, :]
        return S, jnp.einsum('bhvk,bhk->bhv', S, r_t)

    S_fin, y = lax.scan(step, wkv0.astype(jnp.float32), (tm(rh), tm(wh), tm(kh), tm(vh), tm(kk), tm(ah)))
    y = jnp.moveaxis(y, 0, 1)
    mean = jnp.mean(y, axis=-1, keepdims=True)
    var = jnp.mean(jnp.square(y - mean), axis=-1, keepdims=True)
    y = ((y - mean) * lax.rsqrt(var + RW_LNX_EPS)).reshape(B, L, D) * lnx_w + lnx_b
    bonus = jnp.sum(rh * kh * r_k, axis=-1, keepdims=True) * vh
    y = (y + bonus.reshape(B, L, D)).astype(h.dtype)
    return mm(y * g, w_o), h[:, -1], S_fin


def rope_partial(x, pos):
    half = ROPE_DIM // 2
    inv = ROPE_THETA ** (-jnp.arange(half, dtype=jnp.float32) / half)
    ang = pos.astype(jnp.float32)[:, None] * inv
    cos, sin = jnp.cos(ang)[:, None, :], jnp.sin(ang)[:, None, :]
    x1 = x[..., :half].astype(jnp.float32)
    x2 = x[..., half:ROPE_DIM].astype(jnp.float32)
    rot = jnp.concatenate([x1 * cos - x2 * sin, x2 * cos + x1 * sin], axis=-1).astype(x.dtype)
    return jnp.concatenate([rot, x[..., ROPE_DIM:]], axis=-1)


def moba_qkv(h, pos, w_qkv, q_norm, k_norm):
    B, L, _ = h.shape
    qkv = mm(h, w_qkv).reshape(B, L, 3, MB_HEADS, MB_HEADDIM)
    q = rope_partial(rmsnorm(qkv[:, :, 0], q_norm), pos)
    k = rope_partial(rmsnorm(qkv[:, :, 1], k_norm), pos)
    return q, k, qkv[:, :, 2]


def moba_attend_one(qpos, q, k_all, v_all):
    Lq, Lk = q.shape[0], k_all.shape[0]
    H, Dh, K = MB_HEADS, MB_HEADDIM, MB_TOPK
    nb = max(-(-Lk // MB_BLOCK), K)
    pad = nb * MB_BLOCK - Lk
    blocks = lambda t: jnp.pad(t, ((0, pad), (0, 0), (0, 0))).reshape(nb, MB_BLOCK, H, Dh).transpose(2, 0, 1, 3)
    kb, vb = blocks(k_all), blocks(v_all)
    kmean = jnp.mean(kb.astype(jnp.float32), axis=2)
    cur = qpos // MB_BLOCK
    gate = jnp.einsum('qhd,hnd->hqn', q.astype(jnp.float32), kmean)
    fully_past = jnp.arange(nb)[None, :] < cur[:, None]
    gate = jnp.where(fully_past[None], gate, -jnp.inf)
    _, top = lax.top_k(gate, K)
    sel_ok = jnp.arange(K)[None, :] < cur[:, None]
    idx = jnp.concatenate([top, jnp.broadcast_to(cur[None, :, None], (H, Lq, 1))], axis=-1)
    qb = math.gcd(Lq, MB_QBLOCK)
    nq = Lq // qb
    hidx = jnp.arange(H)[:, None, None]
    scale = MB_HEADDIM ** -0.5

    def one_block(args):
        qs, ps, ids, ok = args
        kg, vg = kb[hidx, ids], vb[hidx, ids]
        s = jnp.einsum('qhd,hqsbd->hqsb', qs, kg).astype(jnp.float32) * scale
        kpos = ids[..., None] * MB_BLOCK + jnp.arange(MB_BLOCK)
        valid = jnp.concatenate([ok, jnp.ones((qb, 1), bool)], axis=-1)
        mask = valid[None, :, :, None] & (kpos <= ps[None, :, None, None])
        s = jnp.where(mask, s, -jnp.inf)
        p = jax.nn.softmax(s.reshape(H, qb, -1), axis=-1).reshape(s.shape)
        return jnp.einsum('hqsb,hqsbd->qhd', p.astype(vg.dtype), vg)

    out = lax.map(one_block, (q.reshape(nq, qb, H, Dh), qpos.reshape(nq, qb),
                              idx.reshape(H, nq, qb, K + 1).transpose(1, 0, 2, 3), sel_ok.reshape(nq, qb, K)))
    return out.reshape(Lq, H, Dh)


def moba_sample_one(qpos, ck, cv, args):
    q, k_new, v_new, pt = args
    k_all = jnp.concatenate([ck[pt].reshape(-1, MB_HEADS, MB_HEADDIM).astype(k_new.dtype), k_new], axis=0)
    v_all = jnp.concatenate([cv[pt].reshape(-1, MB_HEADS, MB_HEADDIM).astype(v_new.dtype), v_new], axis=0)
    return moba_attend_one(qpos, q, k_all, v_all)


def mlstm_chunked(q, k, v, i_pre, log_f, c0, n0, m0):
    B, L = q.shape[0], q.shape[1]
    Q = math.gcd(L, ML_CHUNK)
    nc = L // Q
    chunks = lambda t: jnp.moveaxis(t.reshape(B, nc, Q, *t.shape[2:]), 1, 0)
    causal = jnp.tril(jnp.ones((Q, Q), bool))[None, :, :, None]

    def step(carry, inp):
        C, n, m = carry
        qc, kc, vc, ic, fc = inp
        b = jnp.cumsum(fc, axis=1)
        dmat = jnp.where(causal, b[:, :, None] - b[:, None] + ic[:, None], -jnp.inf)
        m_inter = b + m[:, None]
        m_t = jnp.maximum(m_inter, jnp.max(dmat, axis=2))
        w_intra = jnp.exp(dmat - m_t[:, :, None])
        w_inter = jnp.exp(m_inter - m_t)
        qk = jnp.einsum('bthd,bshd->btsh', qc, kc) * w_intra
        num = jnp.einsum('btsh,bshv->bthv', qk, vc) + w_inter[..., None] * jnp.einsum('bhvd,bthd->bthv', C, qc)
        den = jnp.sum(qk, axis=2) + w_inter * jnp.einsum('bhd,bthd->bth', n, qc)
        hout = num / jnp.maximum(jnp.abs(den), jnp.exp(-m_t))[..., None]
        m_last = m_t[:, -1]
        w_state = jnp.exp(b[:, -1:] - b + ic - m_last[:, None])
        decay = jnp.exp(b[:, -1] + m - m_last)
        C = decay[..., None, None] * C + jnp.einsum('bsh,bshv,bshd->bhvd', w_state, vc, kc)
        n = decay[..., None] * n + jnp.einsum('bsh,bshd->bhd', w_state, kc)
        return (C, n, m_last), hout

    carry0 = (c0.astype(jnp.float32), n0.astype(jnp.float32), m0.astype(jnp.float32))
    (C, n, m), hout = lax.scan(step, carry0, (chunks(q), chunks(k), chunks(v), chunks(i_pre), chunks(log_f)))
    return jnp.moveaxis(hout, 0, 1).reshape(v.shape), C, n, m


def mlstm_mixer(h, c0, n0, m0, w_in, b_if, norm_w, w_o):
    B, L, _ = h.shape
    H, DK, DV = ML_HEADS, ML_QK_DIM, ML_V_DIM
    proj = mm(h, w_in)
    o1, o2, o3, o4 = H * DK, 2 * H * DK, 2 * H * DK + H * DV, 2 * H * DK + 2 * H * DV
    q = proj[..., :o1].reshape(B, L, H, DK).astype(jnp.float32) * DK ** -0.5
    k = proj[..., o1:o2].reshape(B, L, H, DK).astype(jnp.float32)
    v = proj[..., o2:o3].reshape(B, L, H, DV).astype(jnp.float32)
    o = jax.nn.sigmoid(proj[..., o3:o4])
    gates = proj[..., o4:].astype(jnp.float32) + b_if
    i_pre, log_f = gates[..., :H], jax.nn.log_sigmoid(gates[..., H:])
    hh, C, n, m = mlstm_chunked(q, k, v, i_pre, log_f, c0, n0, m0)
    hh = rmsnorm(hh, norm_w.reshape(H, DV)).reshape(B, L, H * DV)
    return mm(o * hh.astype(h.dtype), w_o), C, n, m


def kernel(x_prompt, x_sample, state_ssd_conv, state_ssd, state_rwkv_shift, state_rwkv, cache_k, cache_v, state_mlstm_c, state_mlstm_n, state_mlstm_m, page_table, c_prompt, c_sample, ada_w, ada_b, norm_w, ffn_w_in, ffn_w_out, ssd_w_in, ssd_conv_w, ssd_conv_b, ssd_dt_bias, ssd_a_log, ssd_d, ssd_norm_w, ssd_w_out, rw_mu, rw_w_r, rw_w_k, rw_w_v, rw_w0, rw_w1, rw_w2, rw_a0, rw_a1, rw_a2, rw_g1, rw_g2, rw_k_k, rw_k_a, rw_r_k, rw_lnx_w, rw_lnx_b, rw_w_o, mb_w_qkv, mb_q_norm, mb_k_norm, mb_w_o, ml_w_in, ml_b_if, ml_norm_w, ml_w_o):
    B, S = x_prompt.shape[0], x_prompt.shape[1]
    Bs, Ls = x_sample.shape[0], x_sample.shape[1]
    D = D_MODEL
    pos_p = jnp.arange(S, dtype=jnp.int32)
    pos_s = PAST_LEN + jnp.arange(Ls, dtype=jnp.int32)
    xp, xs = x_prompt, x_sample
    outs = {}
    c_all = jnp.concatenate([c_prompt, c_sample], axis=0)
    sc_all = jax.nn.silu(c_all)
    for i in range(DEPTH):
        kind, j = i % N_MIXERS, i // N_MIXERS
        mod = (mm(sc_all, ada_w[i]) + ada_b[i]).reshape(B + Bs, N_MOD, 1, D)
        mp, ms = mod[:B], mod[B:]
        wi0, wo0 = ffn_w_in[i, 0].astype(BF16), ffn_w_out[i, 0].astype(BF16)
        wi1, wo1 = ffn_w_in[i, 1].astype(BF16), ffn_w_out[i, 1].astype(BF16)
        xp = ffn(xp, mp[:, 0], mp[:, 1], mp[:, 2], norm_w[i, 0], wi0, wo0, 1, 512)
        xs = ffn(xs, ms[:, 0], ms[:, 1], ms[:, 2], norm_w[i, 0], wi0, wo0, 64, Ls)
        m1p, m1s = (mp[:, 3], mp[:, 4], mp[:, 5], norm_w[i, 1]), (ms[:, 3], ms[:, 4], ms[:, 5], norm_w[i, 1])
        if kind == 0:
            w = (ssd_w_in[j], ssd_conv_w[j], ssd_conv_b[j], ssd_dt_bias[j], ssd_a_log[j], ssd_d[j], ssd_norm_w[j], ssd_w_out[j])
            xp, cp, sp = ssd_layer(xp, *m1p, jnp.zeros((B, SSD_CONV - 1, SSD_CONV_DIM), F32),
                                   jnp.zeros((B, SSD_HEADS, SSD_HEADDIM, SSD_STATE), F32), *w, 1, 512, SSD_CHUNK)
            xs, cs_, ss = ssd_layer(xs, *m1s, state_ssd_conv[j], state_ssd[j], *w, 64, Ls, Ls)
            outs[2], outs[3], outs[4], outs[5] = cp[None], cs_[None], sp[None], ss[None]
        elif kind == 1:
            w = (rw_mu[j], rw_w_r[j], rw_w_k[j], rw_w_v[j], rw_w0[j], rw_w1[j], rw_w2[j], rw_a0[j], rw_a1[j],
                 rw_a2[j], rw_g1[j], rw_g2[j], rw_k_k[j], rw_k_a[j], rw_r_k[j], rw_lnx_w[j], rw_lnx_b[j], rw_w_o[j])
            xp, shp, sp = rwkv_layer(xp, *m1p, jnp.zeros((B, D), F32),
                                     jnp.zeros((B, RW_HEADS, RW_HEADDIM, RW_HEADDIM), F32), *w, 1, 256, 64)
            xs, shs, ss = rwkv_layer(xs, *m1s, state_rwkv_shift[j], state_rwkv[j], *w, 32, Ls, Ls)
            outs[6], outs[7], outs[8], outs[9] = shp[None], shs[None], sp[None], ss[None]
        elif kind == 2:
            hp = modnorm(xp, mp[:, 3], mp[:, 4], norm_w[i, 1], 1, 512, BF16)
            hs = modnorm(xs, ms[:, 3], ms[:, 4], norm_w[i, 1], 64, Ls, BF16)
            qkv_p, qkv_s = mm(hp, mb_w_qkv[j]), mm(hs, mb_w_qkv[j])
            qp, kp = qkpost(qkv_p, pos_p, mb_q_norm[j], mb_k_norm[j], 1, 512)
            qs, ks_ = qkpost(qkv_s, pos_s, mb_q_norm[j], mb_k_norm[j], 16, Ls)
            vp, vs = qkv_p[..., 2 * D:], qkv_s[..., 2 * D:]
            op = moba_prompt_attend(qp, kp, vp)
            npool, page = cache_k.shape[1], cache_k.shape[2]
            os_ = moba_sample_attend(qs, ks_, vs, cache_k[j].reshape(npool, page, D),
                                     cache_v[j].reshape(npool, page, D), page_table)
            w_o = mb_w_o[j].astype(BF16)
            xp = oproj(op, None, w_o, xp, mp[:, 5], 1, 512)
            xs = oproj(os_, None, w_o, xs, ms[:, 5], 64, Ls)
            hd = (MB_HEADS, MB_HEADDIM)
            outs[10], outs[11] = kp.reshape(1, B, S, *hd), ks_.reshape(1, Bs, Ls, *hd)
            outs[12], outs[13] = vp.reshape(1, B, S, *hd), vs.reshape(1, Bs, Ls, *hd)
        else:
            w = (ml_w_in[j], ml_b_if[j], ml_norm_w[j], ml_w_o[j])
            xp, cp, np_, mp_ = mlstm_layer(xp, *m1p, jnp.zeros((B, ML_HEADS, ML_V_DIM, ML_QK_DIM), F32),
                                           jnp.zeros((B, ML_HEADS, ML_QK_DIM), F32), jnp.zeros((B, ML_HEADS), F32),
                                           *w, 1, 512, 128)
            xs, cs_, ns_, ms_ = mlstm_layer(xs, *m1s, state_mlstm_c[j], state_mlstm_n[j], state_mlstm_m[j],
                                            *w, 64, Ls, Ls)
            outs[14], outs[15], outs[16], outs[17], outs[18], outs[19] = cp[None], cs_[None], np_[None], ns_[None], mp_[None], ms_[None]
        xp = ffn(xp, mp[:, 6], mp[:, 7], mp[:, 8], norm_w[i, 2], wi1, wo1, 1, 512)
        xs = ffn(xs, ms[:, 6], ms[:, 7], ms[:, 8], norm_w[i, 2], wi1, wo1, 64, Ls)
    outs[0], outs[1] = xp, xs
    return tuple(outs[i] for i in range(20))
```

```python
import functools
import math

import jax
import jax.numpy as jnp
from jax import lax
from jax.experimental import pallas as pl
from jax.experimental.pallas import tpu as pltpu

D_MODEL = 1024
DEPTH = 4
PAST_LEN = 2048
N_MIXERS = 4
NORM_EPS = 1e-6
N_MOD = 9
D_FF = 2816

SSD_D_INNER = 2 * D_MODEL
SSD_HEADDIM = 64
SSD_HEADS = SSD_D_INNER // SSD_HEADDIM
SSD_GROUPS = 4
SSD_HPG = SSD_HEADS // SSD_GROUPS
SSD_STATE = 128
SSD_CONV = 4
SSD_CHUNK = 128
SSD_CONV_DIM = SSD_D_INNER + 2 * SSD_GROUPS * SSD_STATE

RW_HEADDIM = 64
RW_HEADS = D_MODEL // RW_HEADDIM
RW_LNX_EPS = 64e-5

MB_HEADS = 16
MB_HEADDIM = D_MODEL // MB_HEADS
MB_BLOCK = 256
MB_TOPK = 3
MB_QBLOCK = 16
ROPE_DIM = MB_HEADDIM // 4
ROPE_THETA = 500000.0

ML_HEADS = 8
ML_QK_DIM = D_MODEL // 2 // ML_HEADS
ML_V_DIM = D_MODEL // ML_HEADS
ML_CHUNK = 64

LANE = 128
VMEM_LIMIT = 48 * 1024 * 1024

BF16 = jnp.bfloat16
F32 = jnp.float32


def _cparams(sem):
    return pltpu.CompilerParams(dimension_semantics=sem, vmem_limit_bytes=VMEM_LIMIT)


def _mm_kernel(a_ref, w_ref, o_ref):
    o_ref[...] = jnp.dot(a_ref[...].astype(BF16), w_ref[...], preferred_element_type=F32)


def _pick(n, cands):
    for c in cands:
        if n % c == 0:
            return c
    raise ValueError(n)


def mm(a, w):
    lead = a.shape[:-1]
    K = a.shape[-1]
    N = w.shape[-1]
    a2 = a.reshape(-1, K)
    M = a2.shape[0]
    Mp = -(-M // 8) * 8
    if Mp != M:
        a2 = jnp.pad(a2, ((0, Mp - M), (0, 0)))
    Np = -(-N // LANE) * LANE
    wb = w.astype(BF16)
    if Np != N:
        wb = jnp.pad(wb, ((0, 0), (0, Np - N)))
    tm = _pick(Mp, (512, 256, 128, 64, 32, 16, 8))
    tn = _pick(Np, (1024, 512, 256, 128))
    out = pl.pallas_call(
        _mm_kernel, name="mm",
        grid=(Mp // tm, Np // tn),
        in_specs=[pl.BlockSpec((tm, K), lambda i, j: (i, 0)),
                  pl.BlockSpec((K, tn), lambda i, j: (0, j))],
        out_specs=pl.BlockSpec((tm, tn), lambda i, j: (i, j)),
        out_shape=jax.ShapeDtypeStruct((Mp, Np), F32),
        compiler_params=_cparams(("parallel", "arbitrary")),
    )(a2, wb)
    return out[:M, :N].reshape(*lead, N)


def _ffn_kernel(x_ref, sh_ref, sc_ref, gt_ref, nw_ref, wg_ref, wu_ref, wo_ref, o_ref, h_sc, acc_sc):
    k = pl.program_id(2)
    gb, rb, d = x_ref.shape

    @pl.when(k == 0)
    def _():
        x = x_ref[...]
        y = x * lax.rsqrt(jnp.mean(x * x, axis=-1, keepdims=True) + NORM_EPS) * nw_ref[...]
        h = y * (1.0 + sc_ref[...]) + sh_ref[...]
        h_sc[...] = h.reshape(gb * rb, d).astype(BF16)

    h = h_sc[...]
    g = jnp.dot(h, wg_ref[...], preferred_element_type=F32)
    u = jnp.dot(h, wu_ref[...], preferred_element_type=F32)
    a = (g * jax.nn.sigmoid(g) * u).astype(BF16)
    part = jnp.dot(a, wo_ref[...], preferred_element_type=F32)

    @pl.when(k == 0)
    def _():
        acc_sc[...] = part

    @pl.when(k > 0)
    def _():
        acc_sc[...] += part

    @pl.when(k == pl.num_programs(2) - 1)
    def _():
        o_ref[...] = x_ref[...] + 0.5 * gt_ref[...] * acc_sc[...].reshape(gb, rb, d)


def ffn(x, shift, scale, gate, nw, w_in_b, w_out_b, gb, rb):
    G, R, D = x.shape
    tf = D_FF // 2
    nk = D_FF // tf
    xspec = pl.BlockSpec((gb, rb, D), lambda g, r, k: (g, r, 0))
    mspec = pl.BlockSpec((gb, 1, D), lambda g, r, k: (g, 0, 0))
    return pl.pallas_call(
        _ffn_kernel, name="ffn",
        grid=(G // gb, R // rb, nk),
        in_specs=[xspec, mspec, mspec, mspec,
                  pl.BlockSpec((1, 1, D), lambda g, r, k: (0, 0, 0)),
                  pl.BlockSpec((D, tf), lambda g, r, k: (0, k)),
                  pl.BlockSpec((D, tf), lambda g, r, k: (0, k + nk)),
                  pl.BlockSpec((tf, D), lambda g, r, k: (k, 0))],
        out_specs=xspec,
        out_shape=jax.ShapeDtypeStruct(x.shape, F32),
        scratch_shapes=[pltpu.VMEM((gb * rb, D), BF16), pltpu.VMEM((gb * rb, D), F32)],
        compiler_params=_cparams(("parallel", "parallel", "arbitrary")),
    )(x, shift, scale, gate, nw.reshape(1, 1, D), w_in_b, w_in_b, w_out_b)


HI = lax.Precision.HIGHEST
NN = ((1,), (0,))
NT = ((1,), (1,))
TN = ((0,), (0,))
NEG = -1e30


def _dot(a, b, dims=NN, hi=False):
    if hi:
        return lax.dot_general(a, b, (dims, ((), ())), precision=HI, preferred_element_type=F32)
    return lax.dot_general(a.astype(BF16), b.astype(BF16), (dims, ((), ())), preferred_element_type=F32)


def _split(a):
    hi = a.astype(BF16)
    return hi, (a - hi.astype(F32)).astype(BF16)


def _dot3s(sa, sb, dims=NN):
    d = lambda x, y: lax.dot_general(x, y, (dims, ((), ())), preferred_element_type=F32)
    return d(sa[0], sb[0]) + d(sa[0], sb[1]) + d(sa[1], sb[0])


def _dot3(a, b, dims=NN):
    return _dot3s(_split(a), _split(b), dims)


def _modnorm_kernel(x_ref, sh_ref, sc_ref, nw_ref, o_ref):
    x = x_ref[...]
    y = x * lax.rsqrt(jnp.mean(x * x, axis=-1, keepdims=True) + NORM_EPS) * nw_ref[...]
    o_ref[...] = (y * (1.0 + sc_ref[...]) + sh_ref[...]).astype(o_ref.dtype)


def modnorm(x, shift, scale, nw, gb, rb, dtype):
    G, R, D = x.shape
    xspec = pl.BlockSpec((gb, rb, D), lambda g, r: (g, r, 0))
    mspec = pl.BlockSpec((gb, 1, D), lambda g, r: (g, 0, 0))
    return pl.pallas_call(
        _modnorm_kernel, name="modnorm",
        grid=(G // gb, R // rb),
        in_specs=[xspec, mspec, mspec, pl.BlockSpec((1, 1, D), lambda g, r: (0, 0, 0))],
        out_specs=xspec,
        out_shape=jax.ShapeDtypeStruct(x.shape, dtype),
        compiler_params=_cparams(("parallel", "parallel")),
    )(x, shift, scale, nw.reshape(1, 1, D))


def _oproj_kernel(*refs, has_mul):
    if has_mul:
        y_ref, mul_ref, w_ref, x_ref, gt_ref, o_ref = refs
    else:
        y_ref, w_ref, x_ref, gt_ref, o_ref = refs
    gb, rb, kd = y_ref.shape
    y = y_ref[...]
    if has_mul:
        y = y * mul_ref[...]
    acc = jnp.dot(y.reshape(gb * rb, kd).astype(BF16), w_ref[...], preferred_element_type=F32)
    o_ref[...] = x_ref[...] + gt_ref[...] * acc.reshape(gb, rb, acc.shape[-1])


def oproj(y, mul, w_b, x, gate, gb, rb):
    G, R, Kd = y.shape
    D = x.shape[-1]
    yspec = pl.BlockSpec((gb, rb, Kd), lambda g, r: (g, r, 0))
    xspec = pl.BlockSpec((gb, rb, D), lambda g, r: (g, r, 0))
    ins = [y] + ([mul] if mul is not None else []) + [w_b, x, gate]
    specs = [yspec] + ([yspec] if mul is not None else []) + [
        pl.BlockSpec((Kd, D), lambda g, r: (0, 0)), xspec,
        pl.BlockSpec((gb, 1, D), lambda g, r: (g, 0, 0))]
    return pl.pallas_call(
        functools.partial(_oproj_kernel, has_mul=mul is not None), name="oproj",
        grid=(G // gb, R // rb),
        in_specs=specs,
        out_specs=xspec,
        out_shape=jax.ShapeDtypeStruct(x.shape, F32),
        compiler_params=_cparams(("parallel", "parallel")),
    )(*ins)


def _qkpost_kernel(q_ref, k_ref, cos_ref, sin_ref, qw_ref, kw_ref, qo_ref, ko_ref):
    gb, rb, w = q_ref.shape
    n = gb * rb
    lane = lax.broadcasted_iota(jnp.int32, (n, w), 1)
    lo = lane < MB_HEADDIM
    first = (lane % MB_HEADDIM) < (ROPE_DIM // 2)
    cos = jnp.broadcast_to(cos_ref[...][None], (gb, rb, w)).reshape(n, w)
    sin = jnp.broadcast_to(sin_ref[...][None], (gb, rb, w)).reshape(n, w)

    def norm_rope(x, wt):
        x2 = x * x
        s_lo = jnp.sum(jnp.where(lo, x2, 0.0), axis=-1, keepdims=True)
        s_hi = jnp.sum(jnp.where(lo, 0.0, x2), axis=-1, keepdims=True)
        ms = jnp.where(lo, s_lo, s_hi) * (1.0 / MB_HEADDIM)
        y = x * lax.rsqrt(ms + NORM_EPS) * wt
        up = pltpu.roll(y, w - ROPE_DIM // 2, 1)
        dn = pltpu.roll(y, ROPE_DIM // 2, 1)
        return y * cos + jnp.where(first, up, dn) * sin

    qo_ref[...] = norm_rope(q_ref[...].reshape(n, w), qw_ref[...]).reshape(gb, rb, w)
    ko_ref[...] = norm_rope(k_ref[...].reshape(n, w), kw_ref[...]).reshape(gb, rb, w)


def _rope_tables(pos):
    half = ROPE_DIM // 2
    inv = ROPE_THETA ** (-jnp.arange(half, dtype=F32) / half)
    ang = pos.astype(F32)[:, None] * inv
    c, s = jnp.cos(ang), jnp.sin(ang)
    ones = jnp.ones((pos.shape[0], MB_HEADDIM - ROPE_DIM), F32)
    ch = jnp.concatenate([c, c, ones], axis=-1)
    sh = jnp.concatenate([-s, s, 0.0 * ones], axis=-1)
    return jnp.concatenate([ch, ch], axis=-1), jnp.concatenate([sh, sh], axis=-1)


def qkpost(qkv, pos, q_norm, k_norm, gb, rb):
    G, R, _ = qkv.shape
    D = D_MODEL
    npair = D // LANE
    cos, sin = _rope_tables(pos)
    tspec = pl.BlockSpec((rb, LANE), lambda g, r, p: (r, 0))
    wspec = pl.BlockSpec((1, LANE), lambda g, r, p: (0, 0))
    ospec = pl.BlockSpec((gb, rb, LANE), lambda g, r, p: (g, r, p))
    return pl.pallas_call(
        _qkpost_kernel, name="qkpost",
        grid=(G // gb, R // rb, npair),
        in_specs=[pl.BlockSpec((gb, rb, LANE), lambda g, r, p: (g, r, p)),
                  pl.BlockSpec((gb, rb, LANE), lambda g, r, p: (g, r, p + npair)),
                  tspec, tspec, wspec, wspec],
        out_specs=[ospec, ospec],
        out_shape=[jax.ShapeDtypeStruct((G, R, D), F32)] * 2,
        compiler_params=_cparams(("parallel", "parallel", "parallel")),
    )(qkv, qkv, cos, sin, jnp.tile(q_norm, 2).reshape(1, LANE), jnp.tile(k_norm, 2).reshape(1, LANE))


def _top3_mask(g, idx, nb):
    sel = jnp.zeros(g.shape, jnp.bool_)
    for _ in range(MB_TOPK):
        mx = jnp.max(g, axis=-1, keepdims=True)
        first = jnp.min(jnp.where(g == mx, idx, nb), axis=-1, keepdims=True)
        pick = idx == first
        sel = jnp.logical_or(sel, pick)
        g = jnp.where(pick, -jnp.inf, g)
    return sel


def _moba_prompt_kernel(q_ref, k_ref, v_ref, o_ref, kmean_sc, selx_sc):
    c = pl.program_id(2)
    T = MB_BLOCK
    nb = k_ref.shape[1] // T
    scale = MB_HEADDIM ** -0.5

    @pl.when(c == 0)
    def _():
        for n in range(nb):
            kmean_sc[n:n + 1, :] = jnp.mean(k_ref[0, n * T:(n + 1) * T, :], axis=0, keepdims=True)

    q = q_ref[0]
    lo = lax.broadcasted_iota(jnp.int32, (T, LANE), 1) < MB_HEADDIM
    qh = (jnp.where(lo, q, 0.0), jnp.where(lo, 0.0, q))
    blk = lax.broadcasted_iota(jnp.int32, (T, nb), 1)
    for hh in range(2):
        g = _dot(qh[hh], kmean_sc[...], NT, hi=True)
        g = jnp.where(blk < c, g, -jnp.inf)
        sel = jnp.logical_and(_top3_mask(g, blk, nb), blk < c).astype(F32)
        for n in range(nb):
            selx_sc[hh, n] = jnp.broadcast_to(sel[:, n:n + 1], (T, LANE))
    qb = tuple((x * scale).astype(BF16) for x in qh)

    def blocks(n):
        st = pl.multiple_of(n * T, T)
        return k_ref[0, pl.ds(st, T), :].astype(BF16), v_ref[0, pl.ds(st, T), :].astype(BF16)

    kc, vc = blocks(c)
    causal = lax.broadcasted_iota(jnp.int32, (T, T), 0) >= lax.broadcasted_iota(jnp.int32, (T, T), 1)
    init = []
    for hh in range(2):
        s = jnp.where(causal, _dot(qb[hh], kc, NT), NEG)
        m = jnp.max(s, axis=-1, keepdims=True)
        p = jnp.exp(s - m)
        init += [m, jnp.sum(p, axis=-1, keepdims=True), _dot(p, vc)]

    def body(n, carry):
        kn, vn = blocks(n)
        out = []
        for hh in range(2):
            m, l, acc = carry[3 * hh:3 * hh + 3]
            keep = selx_sc[hh, n] > 0.5
            s = _dot(qb[hh], kn, NT)
            s0 = jnp.where(keep, s[:, :LANE], NEG)
            s1 = jnp.where(keep, s[:, LANE:], NEG)
            m_new = jnp.maximum(m, jnp.max(jnp.maximum(s0, s1), axis=-1, keepdims=True))
            alpha = jnp.exp(m - m_new)
            p0 = jnp.exp(s0 - m_new)
            p1 = jnp.exp(s1 - m_new)
            l = alpha * l + jnp.sum(p0 + p1, axis=-1, keepdims=True)
            acc = alpha * acc + _dot(p0, vn[:LANE]) + _dot(p1, vn[LANE:])
            out += [m_new, l, acc]
        return tuple(out)

    res = lax.fori_loop(0, c, body, tuple(init))
    o_ref[0] = jnp.where(lo, res[2] / res[1], res[5] / res[4])


def moba_prompt_attend(q, k, v):
    B, L, D = q.shape
    nb = L // MB_BLOCK
    kvspec = pl.BlockSpec((1, L, LANE), lambda b, p, c: (b, 0, p))
    qspec = pl.BlockSpec((1, MB_BLOCK, LANE), lambda b, p, c: (b, c, p))
    return pl.pallas_call(
        _moba_prompt_kernel, name="moba_prompt",
        grid=(B, D // LANE, nb),
        in_specs=[qspec, kvspec, kvspec],
        out_specs=qspec,
        out_shape=jax.ShapeDtypeStruct((B, L, D), F32),
        scratch_shapes=[pltpu.VMEM((nb, LANE), F32), pltpu.VMEM((2, nb, MB_BLOCK, LANE), F32)],
        compiler_params=_cparams(("parallel", "parallel", "arbitrary")),
    )(q, k, v)


def _top3_mask0(g, idx, nb):
    sel = jnp.zeros(g.shape, jnp.bool_)
    for _ in range(MB_TOPK):
        mx = jnp.max(g, axis=0, keepdims=True)
        first = jnp.min(jnp.where(g == mx, idx, nb), axis=0, keepdims=True)
        pick = idx == first
        sel = jnp.logical_or(sel, pick)
        g = jnp.where(pick, -jnp.inf, g)
    return sel


def _moba_prompt_t_kernel(q_ref, k_ref, v_ref, o_ref, kmean_sc, kb_sc, vt_sc, selt_sc):
    c = pl.program_id(2)
    T = MB_BLOCK
    P = MB_HEADDIM
    nb = k_ref.shape[1] // T
    scale = MB_HEADDIM ** -0.5

    @pl.when(c == 0)
    def _():
        for n in range(nb):
            kn = k_ref[0, n * T:(n + 1) * T, :]
            kmean_sc[n:n + 1, :] = jnp.mean(kn, axis=0, keepdims=True)
            kb_sc[n] = kn.astype(BF16)
            vt_sc[n] = v_ref[0, n * T:(n + 1) * T, :].T.astype(BF16)

    q = q_ref[0]
    lo = lax.broadcasted_iota(jnp.int32, (T, LANE), 1) < P
    qh = (jnp.where(lo, q, 0.0), jnp.where(lo, 0.0, q))
    blk = lax.broadcasted_iota(jnp.int32, (nb, T), 0)
    for hh in range(2):
        g = _dot(kmean_sc[...], qh[hh], NT, hi=True)
        g = jnp.where(blk < c, g, -jnp.inf)
        selt_sc[hh] = jnp.logical_and(_top3_mask0(g, blk, nb), blk < c).astype(F32)
    qb = tuple((x * scale).astype(BF16) for x in qh)
    causal = lax.broadcasted_iota(jnp.int32, (T, T), 0) <= lax.broadcasted_iota(jnp.int32, (T, T), 1)
    kc, vc = kb_sc[c], vt_sc[c]
    init = []
    for hh in range(2):
        s = jnp.where(causal, _dot(kc, qb[hh], NT), NEG)
        m = jnp.max(s, axis=0, keepdims=True)
        p = jnp.exp(s - m)
        init += [m, jnp.sum(p, axis=0, keepdims=True), _dot(vc[hh * P:(hh + 1) * P, :], p)]

    def body(i, carry):
        ns = (2 * i, 2 * i + 1)
        out = []
        for hh in range(2):
            m, l, acc = carry[3 * hh:3 * hh + 3]
            ss = [jnp.where(selt_sc[hh, pl.ds(n, 1), :] > 0.5, _dot(kb_sc[n], qb[hh], NT), NEG) for n in ns]
            m_new = jnp.maximum(m, jnp.max(jnp.maximum(ss[0], ss[1]), axis=0, keepdims=True))
            alpha = jnp.exp(m - m_new)
            ps = [jnp.exp(s - m_new) for s in ss]
            l = alpha * l + jnp.sum(ps[0] + ps[1], axis=0, keepdims=True)
            acc = alpha * acc + sum(_dot(vt_sc[n, hh * P:(hh + 1) * P, :], p) for n, p in zip(ns, ps))
            out += [m_new, l, acc]
        return tuple(out)

    res = lax.fori_loop(0, (c + 1) // 2, body, tuple(init))
    o_ref[0] = jnp.concatenate([res[2] / res[1], res[5] / res[4]], axis=0).T


def moba_prompt_attend_t(q, k, v):
    B, L, D = q.shape
    nb = L // MB_BLOCK
    kvspec = pl.BlockSpec((1, L, LANE), lambda b, p, c: (b, 0, p))
    qspec = pl.BlockSpec((1, MB_BLOCK, LANE), lambda b, p, c: (b, c, p))
    return pl.pallas_call(
        _moba_prompt_t_kernel, name="moba_prompt",
        grid=(B, D // LANE, nb),
        in_specs=[qspec, kvspec, kvspec],
        out_specs=qspec,
        out_shape=jax.ShapeDtypeStruct((B, L, D), F32),
        scratch_shapes=[pltpu.VMEM((nb, LANE), F32), pltpu.VMEM((nb, MB_BLOCK, LANE), BF16),
                        pltpu.VMEM((nb, LANE, MB_BLOCK), BF16), pltpu.VMEM((2, nb, MB_BLOCK), F32)],
        compiler_params=_cparams(("parallel", "parallel", "arbitrary")),
    )(q, k, v)


def _moba_sample_kernel(pt_ref, q_ref, kn_ref, vn_ref, k0_ref, k1_ref, v0_ref, v1_ref, o_ref,
                        qbd_sc, qbdb_sc, kmean_sc, o_sc, m_sc, l_sc):
    del pt_ref
    n = pl.program_id(1)
    nblk = pl.num_programs(1)
    lq, d = q_ref.shape[1], q_ref.shape[2]
    rows = MB_HEADS * lq
    scale = MB_HEADDIM ** -0.5
    own = (lax.broadcasted_iota(jnp.int32, (MB_HEADS, lq, d), 0)
           == lax.broadcasted_iota(jnp.int32, (MB_HEADS, lq, d), 2) // MB_HEADDIM)

    @pl.when(n == 0)
    def _():
        qbd = jnp.where(own, q_ref[0][None], 0.0).reshape(rows, d)
        qbd_sc[...] = qbd
        qbdb_sc[...] = (qbd * scale).astype(BF16)

    qb = qbdb_sc[...]
    k0, k1 = k0_ref[0], k1_ref[0]
    s0 = _dot(qb, k0, NT)
    s1 = _dot(qb, k1, NT)
    m = jnp.maximum(jnp.max(s0, axis=-1, keepdims=True), jnp.max(s1, axis=-1, keepdims=True))
    p0 = jnp.exp(s0 - m)
    p1 = jnp.exp(s1 - m)
    o_sc[n] = _dot(p0, v0_ref[0]) + _dot(p1, v1_ref[0])
    m_sc[n] = m
    l_sc[n] = jnp.sum(p0, axis=-1, keepdims=True) + jnp.sum(p1, axis=-1, keepdims=True)
    kmean_sc[n] = (jnp.sum(k0, axis=0, keepdims=True) + jnp.sum(k1, axis=0, keepdims=True)) * (1.0 / MB_BLOCK)

    @pl.when(n == nblk - 1)
    def _():
        qbd = qbd_sc[...]
        nblocks = o_sc.shape[0]
        s = lax.dot_general(qbd * scale, kn_ref[0], (NT, ((), ())), preferred_element_type=F32)
        qi = lax.broadcasted_iota(jnp.int32, (rows, lq), 0) % lq
        s = jnp.where(lax.broadcasted_iota(jnp.int32, (rows, lq), 1) <= qi, s, NEG)
        m_own = jnp.max(s, axis=-1, keepdims=True)
        p = jnp.exp(s - m_own)
        l_own = jnp.sum(p, axis=-1, keepdims=True)
        o_own = jnp.dot(p, vn_ref[0], preferred_element_type=F32)
        g = [jnp.sum(qbd * kmean_sc[j], axis=-1, keepdims=True) for j in range(nblocks)]
        sel = []
        for j in range(nblocks):
            rank = jnp.zeros((rows, 1), jnp.int32)
            for i in range(nblocks):
                if i != j:
                    ahead = (g[i] >= g[j]) if i < j else (g[i] > g[j])
                    rank = rank + ahead.astype(jnp.int32)
            sel.append(rank < MB_TOPK)
        mx = m_own
        for j in range(nblocks):
            mx = jnp.where(sel[j], jnp.maximum(mx, m_sc[j]), mx)
        w_own = jnp.exp(m_own - mx)
        l = l_own * w_own
        o = o_own * w_own
        for j in range(nblocks):
            wj = jnp.where(sel[j], jnp.exp(m_sc[j] - mx), 0.0)
            l = l + wj * l_sc[j]
            o = o + wj * o_sc[j]
        res = (o / l).reshape(MB_HEADS, lq, d)
        o_ref[0] = jnp.sum(jnp.where(own, res, 0.0), axis=0)


def moba_sample_attend(q, k_new, v_new, cache_k, cache_v, page_table):
    Bs, Ls, D = q.shape
    page = cache_k.shape[1]
    assert MB_BLOCK == 2 * page and PAST_LEN % MB_BLOCK == 0 and Ls <= MB_BLOCK
    nblk = PAST_LEN // MB_BLOCK
    assert nblk >= MB_TOPK
    rows = MB_HEADS * Ls
    newspec = pl.BlockSpec((1, Ls, D), lambda b, n, pt: (b, 0, 0))
    pg0 = pl.BlockSpec((1, page, D), lambda b, n, pt: (pt[b, 2 * n], 0, 0))
    pg1 = pl.BlockSpec((1, page, D), lambda b, n, pt: (pt[b, 2 * n + 1], 0, 0))
    return pl.pallas_call(
        _moba_sample_kernel, name="moba_sample",
        grid_spec=pltpu.PrefetchScalarGridSpec(
            num_scalar_prefetch=1,
            grid=(Bs, nblk),
            in_specs=[newspec, newspec, newspec, pg0, pg1, pg0, pg1],
            out_specs=newspec,
            scratch_shapes=[pltpu.VMEM((rows, D), F32), pltpu.VMEM((rows, D), BF16),
                            pltpu.VMEM((nblk, 1, D), F32), pltpu.VMEM((nblk, rows, D), F32),
                            pltpu.VMEM((nblk, rows, 1), F32), pltpu.VMEM((nblk, rows, 1), F32)]),
        out_shape=jax.ShapeDtypeStruct((Bs, Ls, D), F32),
        compiler_params=_cparams(("parallel", "arbitrary")),
    )(page_table, q, k_new, v_new, cache_k, cache_k, cache_v, cache_v)


def _softplus(z):
    return jnp.maximum(z, 0.0) + jnp.log1p(jnp.exp(-jnp.abs(z)))


def _sigmoid(z):
    return 1.0 / (1.0 + jnp.exp(-z))


def _rwkv_pre_kernel(x_ref, halo_ref, sb_ref, sh_ref, sc_ref, nw_ref, mu_ref, w0_ref, a0_ref,
                     wr_ref, wk_ref, wv_ref, w1_ref, w2_ref, a1_ref, a2_ref, g1_ref, g2_ref,
                     r_o, lw_o, k_o, v_o, a_o, g_o, hl_o, hbuf):
    rr = pl.program_id(1)
    gb, rb, d = x_ref.shape
    n = gb * rb

    def hn(x):
        y = x * lax.rsqrt(jnp.mean(x * x, axis=-1, keepdims=True) + NORM_EPS) * nw_ref[...]
        return y * (1.0 + sc_ref[...]) + sh_ref[...]

    h = hn(x_ref[...])
    prev_row = jnp.where(rr == 0, sb_ref[...], hn(halo_ref[...])[:, 7:8, :])
    hbuf[:, 8:, :] = h
    hbuf[:, 7:8, :] = prev_row
    h2 = h.reshape(n, d)
    xx = hbuf[:, 7:7 + rb, :].reshape(n, d) - h2
    mix = lambda i: h2 + xx * mu_ref[i:i + 1, :]
    dot = lambda a, w_ref: jnp.dot(a.astype(BF16), w_ref[...], preferred_element_type=F32)
    out = lambda o_ref, val: o_ref.__setitem__(Ellipsis, val.reshape(gb, rb, d))
    out(r_o, dot(mix(0), wr_ref))
    out(k_o, dot(mix(2), wk_ref))
    out(v_o, dot(mix(3), wv_ref))
    w_raw = w0_ref[...] + dot(jnp.tanh(dot(mix(1), w1_ref)), w2_ref)
    out(lw_o, -jnp.exp(-_softplus(-w_raw) - 0.5))
    out(a_o, _sigmoid(a0_ref[...] + dot(dot(mix(4), a1_ref), a2_ref)))
    out(g_o, dot(_sigmoid(dot(mix(5), g1_ref)), g2_ref))
    hl_o[...] = h[:, rb - 1:rb, :]


def rwkv_pre(x, shift_buf, shift, scale, nw, mu, w0, a0, ws, gb, rb):
    G, R, D = x.shape
    assert gb == 1 or rb == R
    xspec = pl.BlockSpec((gb, rb, D), lambda g, r: (g, r, 0))
    halo = pl.BlockSpec((gb, 8, D), lambda g, r: (g, jnp.maximum(r * (rb // 8) - 1, 0), 0))
    mspec = pl.BlockSpec((gb, 1, D), lambda g, r: (g, 0, 0))
    vec = pl.BlockSpec((1, D), lambda g, r: (0, 0))
    full = lambda w: pl.BlockSpec(w.shape, lambda g, r: (0, 0))
    mu8 = jnp.pad(mu, ((0, 8 - mu.shape[0]), (0, 0)))
    big = jax.ShapeDtypeStruct((G, R, D), F32)
    return pl.pallas_call(
        _rwkv_pre_kernel, name="rwkv_pre",
        grid=(G // gb, R // rb),
        in_specs=[xspec, halo, mspec, mspec, mspec, vec, full(mu8), vec, vec] + [full(w) for w in ws],
        out_specs=[xspec] * 6 + [mspec],
        out_shape=[big] * 6 + [jax.ShapeDtypeStruct((G, 1, D), F32)],
        scratch_shapes=[pltpu.VMEM((gb, rb + 8, D), F32)],
        compiler_params=_cparams(("parallel", "arbitrary")),
    )(x, x, shift_buf.reshape(G, 1, D), shift, scale, nw.reshape(1, D), mu8, w0.reshape(1, D),
      a0.reshape(1, D), *ws)


def _rwkv_scan_kernel(r_ref, lw_ref, k_ref, v_ref, a_ref, kk_ref, ka_ref, rk_ref, lnw_ref, lnb_ref, s0_ref,
                      y_ref, so_ref, s_sc):
    c = pl.program_id(2)
    C = r_ref.shape[1]
    N = RW_HEADDIM

    @pl.when(c == 0)
    def _():
        s_sc[...] = s0_ref[0]

    hg = s_sc.shape[0]
    rows = hg * C
    stack = lambda ref: jnp.concatenate([ref[0][:, h * N:(h + 1) * N] for h in range(hg)], axis=0)
    pstack = lambda ref: jnp.concatenate(
        [jnp.broadcast_to(ref[:, h * N:(h + 1) * N], (C, N)) for h in range(hg)], axis=0)
    r, lw, kraw, v, a = (stack(ref) for ref in (r_ref, lw_ref, k_ref, v_ref, a_ref))
    kkr = kraw * pstack(kk_ref)
    kk = kkr / jnp.maximum(jnp.sqrt(jnp.sum(kkr * kkr, axis=-1, keepdims=True)), 1e-12)
    k = kraw * (1.0 + (a - 1.0) * pstack(ka_ref))
    tpos = lax.broadcasted_iota(jnp.int32, (rows, N), 0) % C
    cum, sh = lw, 1
    while sh < C:
        cum = cum + jnp.where(tpos >= sh, pltpu.roll(cum, sh, 0), 0.0)
        sh *= 2
    e_in, e_ex, e_neg = jnp.exp(cum), jnp.exp(cum - lw), jnp.exp(-cum)
    ab, bb, kb, rb_ = -kk * e_ex, kk * a * e_neg, k * e_neg, r * e_in
    ri = lax.broadcasted_iota(jnp.int32, (rows, rows), 0)
    ci = lax.broadcasted_iota(jnp.int32, (rows, rows), 1)
    same = (ri // C) == (ci // C)
    strict, incl = jnp.logical_and(same, ci < ri), jnp.logical_and(same, ci <= ri)
    sab, sbb, skb, srb = _split(ab), _split(bb), _split(kb), _split(rb_)
    lab = jnp.where(strict, _dot3s(sab, sbb, NT), 0.0)
    lak = jnp.where(strict, _dot3s(sab, skb, NT), 0.0)
    mrb = jnp.where(incl, _dot3s(srb, sbb, NT), 0.0)
    mrk = jnp.where(incl, _dot3s(srb, skb, NT), 0.0)
    tinv = (ri == ci).astype(F32) + lab
    spw = _split(lab)
    for _ in range(int(math.log2(C)) - 1):
        spw = _split(_dot3s(spw, spw))
        tinv = tinv + _dot3s(_split(tinv), spw)
    hsl = lambda t, h: t[h * C:(h + 1) * C]
    xs = [_dot3(jnp.concatenate([hsl(ab, h), hsl(rb_, h)], axis=0), s_sc[h], NT) for h in range(hg)]
    ab_s = jnp.concatenate([x[:C] for x in xs], axis=0)
    rb_s = jnp.concatenate([x[C:] for x in xs], axis=0)
    u = _dot3(tinv, ab_s + _dot3(lak, v))
    y = rb_s + _dot3(jnp.concatenate([mrb, mrk], axis=1), jnp.concatenate([u, v], axis=0))
    for h in range(hg):
        upd = _dot3(jnp.concatenate([hsl(u, h), hsl(v, h)], axis=0),
                    jnp.concatenate([hsl(bb, h), hsl(kb, h)], axis=0), TN)
        s_sc[h] = (s_sc[h] + upd) * e_in[(h + 1) * C - 1:(h + 1) * C, :]
    mean = jnp.mean(y, axis=-1, keepdims=True)
    var = jnp.mean(jnp.square(y - mean), axis=-1, keepdims=True)
    yn = (y - mean) * lax.rsqrt(var + RW_LNX_EPS) * pstack(lnw_ref) + pstack(lnb_ref)
    out = yn + jnp.sum(r * k * pstack(rk_ref), axis=-1, keepdims=True) * v
    y_ref[0] = jnp.concatenate([hsl(out, h) for h in range(hg)], axis=-1)

    @pl.when(c == pl.num_programs(2) - 1)
    def _():
        so_ref[0] = s_sc[...]


def rwkv_scan(r, lw, k, v, a, k_k, k_a, r_k, lnx_w, lnx_b, s0, C, hg):
    B, L, D = r.shape
    N = RW_HEADDIM
    assert L % C == 0 and C & (C - 1) == 0 and RW_HEADS % hg == 0 and (hg * N) % LANE == 0
    tspec = pl.BlockSpec((1, C, hg * N), lambda b, p, c: (b, c, p))
    pspec = pl.BlockSpec((1, hg * N), lambda b, p, c: (0, p))
    sspec = pl.BlockSpec((1, hg, N, N), lambda b, p, c: (b, p, 0, 0))
    vec = lambda t: t.reshape(1, D)
    return pl.pallas_call(
        _rwkv_scan_kernel, name="rwkv_scan",
        grid=(B, RW_HEADS // hg, L // C),
        in_specs=[tspec] * 5 + [pspec] * 5 + [sspec],
        out_specs=[tspec, sspec],
        out_shape=[jax.ShapeDtypeStruct((B, L, D), F32), jax.ShapeDtypeStruct(s0.shape, F32)],
        scratch_shapes=[pltpu.VMEM((hg, N, N), F32)],
        compiler_params=_cparams(("parallel", "parallel", "arbitrary")),
    )(r, lw, k, v, a, vec(k_k), vec(k_a), vec(r_k), vec(lnx_w), vec(lnx_b), s0)


def rwkv_layer(x, shift, scale, gate, nw, shift_buf, wkv0, mu, w_r, w_k, w_v, w0, w1, w2, a0, a1, a2, g1, g2,
               k_k, k_a, r_k, lnx_w, lnx_b, w_o, gb, rb, C, hg):
    ws = [w.astype(BF16) for w in (w_r, w_k, w_v, w1, w2, a1, a2, g1, g2)]
    r, lw, k, v, a, g, hl = rwkv_pre(x, shift_buf, shift, scale, nw, mu, w0, a0, ws, gb, rb)
    y, s_fin = rwkv_scan(r, lw, k, v, a, k_k, k_a, r_k, lnx_w, lnx_b, wkv0, C, hg)
    return oproj(y, g, w_o.astype(BF16), x, gate, gb, rb), hl[:, 0], s_fin


def _conv_kernel(x_ref, halo_ref, buf_ref, w_ref, b_ref, o_ref, sc):
    rr = pl.program_id(1)
    gb, rb, cw = x_ref.shape
    sc[:, :8, :] = jnp.where(rr == 0, buf_ref[...], halo_ref[...])
    sc[:, 8:, :] = x_ref[...]
    y = b_ref[...]
    for t in range(SSD_CONV):
        y = y + sc[:, 8 - (SSD_CONV - 1) + t:8 - (SSD_CONV - 1) + t + rb, :] * w_ref[t:t + 1, :]
    o_ref[...] = y * _sigmoid(y)


def ssd_conv(zx, buf, w, b, gb, rb, cw=512):
    G, R, _ = zx.shape
    assert gb == 1 or rb == R
    off = SSD_D_INNER // cw
    buf8 = jnp.pad(buf, ((0, 0), (8 - (SSD_CONV - 1), 0), (0, 0)))
    w8 = jnp.pad(w, ((0, 8 - SSD_CONV), (0, 0)))
    return pl.pallas_call(
        _conv_kernel, name="ssd_conv",
        grid=(G // gb, R // rb, SSD_CONV_DIM // cw),
        in_specs=[pl.BlockSpec((gb, rb, cw), lambda g, r, j: (g, r, off + j)),
                  pl.BlockSpec((gb, 8, cw), lambda g, r, j: (g, jnp.maximum(r * (rb // 8) - 1, 0), off + j)),
                  pl.BlockSpec((gb, 8, cw), lambda g, r, j: (g, 0, j)),
                  pl.BlockSpec((8, cw), lambda g, r, j: (0, j)),
                  pl.BlockSpec((1, cw), lambda g, r, j: (0, j))],
        out_specs=pl.BlockSpec((gb, rb, cw), lambda g, r, j: (g, r, j)),
        out_shape=jax.ShapeDtypeStruct((G, R, SSD_CONV_DIM), F32),
        scratch_shapes=[pltpu.VMEM((gb, rb + 8, cw), F32)],
        compiler_params=_cparams(("parallel", "arbitrary", "arbitrary")),
    )(zx, zx, buf8, w8, b.reshape(1, SSD_CONV_DIM))


def _col2row(col, eye):
    return jnp.sum(jnp.where(eye, col, 0.0), axis=0, keepdims=True)


def _ssd_scan_kernel(x_ref, bm_ref, cm_ref, z_ref, dt_ref, dtb_ref, alog_ref, dsk_ref, nw_ref, s0_ref,
                     y_ref, so_ref, s_sc):
    c = pl.program_id(2)
    Q = x_ref.shape[1]
    P, N, R = SSD_HEADDIM, SSD_STATE, SSD_HPG

    @pl.when(c == 0)
    def _():
        s_sc[...] = s0_ref[0]

    ti = lax.broadcasted_iota(jnp.int32, (Q, Q), 0)
    si = lax.broadcasted_iota(jnp.int32, (Q, Q), 1)
    causal, eye = si <= ti, si == ti
    lo = lax.broadcasted_iota(jnp.int32, (Q, LANE), 1) < P
    dt = _softplus(dt_ref[0] + dtb_ref[...])
    cs = _dot(causal.astype(F32), dt * (-jnp.exp(alog_ref[...])), hi=True)
    x, bm, cm = x_ref[0], bm_ref[0], cm_ref[0]
    cb = _dot(cm, bm, NT)
    s_all = s_sc[...].reshape(R * P, N)
    ys = _dot(cm, s_all, NT)
    y_pairs, xw_pairs, e_last = [], [], []
    for j in range(R // 2):
        xp = x[:, j * LANE:(j + 1) * LANE]
        yh, grow, wend = [], [], []
        for r in (2 * j, 2 * j + 1):
            cs_col, dt_col = cs[:, r:r + 1], dt[:, r:r + 1]
            cs_last = cs[Q - 1:Q, r:r + 1]
            dec = jnp.exp(jnp.where(causal, cs_col - _col2row(cs_col, eye), -jnp.inf))
            yh.append(_dot(cb * dec * _col2row(dt_col, eye), xp))
            grow.append(jnp.exp(cs_col))
            wend.append(jnp.exp(cs_last - cs_col) * dt_col)
            e_last.append(jnp.exp(cs_last))
        yp = jnp.where(lo, yh[0], yh[1]) + ys[:, j * LANE:(j + 1) * LANE] * jnp.where(lo, grow[0], grow[1])
        y_pairs.append(yp + xp * dsk_ref[:, j * LANE:(j + 1) * LANE])
        xw_pairs.append(xp * jnp.where(lo, wend[0], wend[1]))
    upd = _dot(jnp.concatenate(xw_pairs, axis=-1), bm, TN)
    for r in range(R):
        s_sc[r] = s_sc[r] * e_last[r] + upd[r * P:(r + 1) * P, :]
    z = z_ref[0]
    yz = jnp.concatenate(y_pairs, axis=-1) * (z * _sigmoid(z))
    y_ref[0] = yz * lax.rsqrt(jnp.mean(yz * yz, axis=-1, keepdims=True) + NORM_EPS) * nw_ref[...]

    @pl.when(c == pl.num_programs(2) - 1)
    def _():
        so_ref[0] = s_sc[...]


def ssd_chunk_scan(xbc, zx, dt_raw, dt_bias, a_log, d_skip, norm_w, s0, Q):
    B, L, _ = xbc.shape
    G, R, P, N = SSD_GROUPS, SSD_HPG, SSD_HEADDIM, SSD_STATE
    gw = R * P
    assert L % Q == 0 and N == LANE
    pad_heads = lambda t: jnp.pad(t.reshape(G, R), ((0, 0), (0, LANE - R))).reshape(1, G * LANE)
    xs = pl.BlockSpec((1, Q, gw), lambda b, g, c: (b, c, g))
    bs = pl.BlockSpec((1, Q, N), lambda b, g, c: (b, c, SSD_D_INNER // N + g))
    cs = pl.BlockSpec((1, Q, N), lambda b, g, c: (b, c, SSD_D_INNER // N + G + g))
    ds = pl.BlockSpec((1, Q, LANE), lambda b, g, c: (b, c, g))
    hs = pl.BlockSpec((1, LANE), lambda b, g, c: (0, g))
    ws = pl.BlockSpec((1, gw), lambda b, g, c: (0, g))
    ss = pl.BlockSpec((1, R, P, N), lambda b, g, c: (b, g, 0, 0))
    return pl.pallas_call(
        _ssd_scan_kernel, name="ssd_scan",
        grid=(B, G, L // Q),
        in_specs=[xs, bs, cs, xs, ds, hs, hs, ws, ws, ss],
        out_specs=[xs, ss],
        out_shape=[jax.ShapeDtypeStruct((B, L, SSD_D_INNER), F32), jax.ShapeDtypeStruct(s0.shape, F32)],
        scratch_shapes=[pltpu.VMEM((R, P, N), F32)],
        compiler_params=_cparams(("parallel", "parallel", "arbitrary")),
    )(xbc, xbc, xbc, zx, dt_raw, pad_heads(dt_bias), pad_heads(a_log),
      jnp.repeat(d_skip, P).reshape(1, SSD_D_INNER), norm_w.reshape(1, SSD_D_INNER), s0)


def ssd_layer(x, shift, scale, gate, nw, conv_buf, ssm0, w_in, conv_w, conv_b, dt_bias, a_log, d_skip, norm_w,
              w_out, gb, rb, Q):
    G, R, D = x.shape
    nzx = SSD_D_INNER + SSD_CONV_DIM
    h = modnorm(x, shift, scale, nw, gb, rb, BF16)
    zx = mm(h, w_in[:, :nzx])
    w_dt = jnp.pad(w_in[:, nzx:].reshape(D, SSD_GROUPS, SSD_HPG), ((0, 0), (0, 0), (0, LANE - SSD_HPG)))
    dt_raw = mm(h, w_dt.reshape(D, SSD_GROUPS * LANE))
    xbc = ssd_conv(zx, conv_buf, conv_w, conv_b, gb, rb)
    y, s_fin = ssd_chunk_scan(xbc, zx, dt_raw, dt_bias, a_log, d_skip, norm_w, ssm0, Q)
    new_conv = zx[:, R - (SSD_CONV - 1):, SSD_D_INNER:]
    return oproj(y, None, w_out.astype(BF16), x, gate, gb, rb), new_conv, s_fin


def _mlstm_kernel(q_ref, k_ref, v_ref, o_ref, g_ref, bif_ref, nw_ref, c0_ref, n0_ref, m0_ref,
                  y_ref, co_ref, no_ref, mo_ref, c_sc, n_sc, m_sc):
    c = pl.program_id(2)
    Q = q_ref.shape[1]
    DK, DV = ML_QK_DIM, ML_V_DIM

    @pl.when(c == 0)
    def _():
        c_sc[...] = c0_ref[0]
        n_sc[...] = n0_ref[0, 0]
        m_sc[...] = m0_ref[0, 0]

    ti = lax.broadcasted_iota(jnp.int32, (Q, Q), 0)
    si = lax.broadcasted_iota(jnp.int32, (Q, Q), 1)
    causal, eye = si <= ti, si == ti
    gates = g_ref[0] + bif_ref[...]
    bcum = _dot(causal.astype(F32), -_softplus(-gates), hi=True)
    ys = []
    for hh in range(2):
        i_col, b_col = gates[:, hh:hh + 1], bcum[:, 2 + hh:3 + hh]
        m_prev = m_sc[:, hh:hh + 1]
        q = q_ref[0][:, hh * DK:(hh + 1) * DK] * DK ** -0.5
        k = k_ref[0][:, hh * DK:(hh + 1) * DK]
        v = v_ref[0][:, hh * DV:(hh + 1) * DV]
        cst, nst = c_sc[hh], n_sc[hh:hh + 1, :]
        dmat = jnp.where(causal, b_col - _col2row(b_col, eye) + _col2row(i_col, eye), -jnp.inf)
        m_inter = b_col + m_prev
        m_t = jnp.maximum(m_inter, jnp.max(dmat, axis=-1, keepdims=True))
        w_inter = jnp.exp(m_inter - m_t)
        qk = _dot(q, k, NT) * jnp.exp(dmat - m_t)
        num = _dot(qk, v) + w_inter * _dot(q, cst, NT)
        den = jnp.sum(qk, axis=-1, keepdims=True) + w_inter * jnp.sum(q * nst, axis=-1, keepdims=True)
        hout = num / jnp.maximum(jnp.abs(den), jnp.exp(-m_t))
        m_last, b_last = m_t[Q - 1:Q], b_col[Q - 1:Q]
        w_state = jnp.exp(b_last - b_col + i_col - m_last)
        decay = jnp.exp(b_last + m_prev - m_last)
        c_sc[hh] = decay * cst + _dot(v * w_state, k, TN)
        n_sc[hh:hh + 1, :] = decay * nst + jnp.sum(w_state * k, axis=0, keepdims=True)
        m_sc[:, hh:hh + 1] = m_last
        hn = hout * lax.rsqrt(jnp.mean(hout * hout, axis=-1, keepdims=True) + NORM_EPS)
        ys.append(_sigmoid(o_ref[0][:, hh * DV:(hh + 1) * DV]) * (hn * nw_ref[:, hh * DV:(hh + 1) * DV]))
    y_ref[0] = jnp.concatenate(ys, axis=-1)

    @pl.when(c == pl.num_programs(2) - 1)
    def _():
        co_ref[0] = c_sc[...]
        no_ref[0, 0] = n_sc[...]
        mo_ref[0, 0] = m_sc[...]


def mlstm_scan(proj, gates, b_if, norm_w, c0, n0, m0, Q):
    B, L, _ = proj.shape
    H, DK, DV = ML_HEADS, ML_QK_DIM, ML_V_DIM
    assert L % Q == 0 and 2 * DK == LANE
    npair = H // 2
    qs = pl.BlockSpec((1, Q, LANE), lambda b, j, c: (b, c, j))
    ks = pl.BlockSpec((1, Q, LANE), lambda b, j, c: (b, c, npair + j))
    vs = pl.BlockSpec((1, Q, 2 * DV), lambda b, j, c: (b, c, H * DK // DV + j))
    os_ = pl.BlockSpec((1, Q, 2 * DV), lambda b, j, c: (b, c, H * DK // DV + npair + j))
    ys = pl.BlockSpec((1, Q, 2 * DV), lambda b, j, c: (b, c, j))
    cs = pl.BlockSpec((1, 2, DV, DK), lambda b, j, c: (b, j, 0, 0))
    ns = pl.BlockSpec((1, 1, 2, DK), lambda b, j, c: (b, j, 0, 0))
    ms = pl.BlockSpec((1, 1, 1, 2), lambda b, j, c: (b, j, 0, 0))
    bi, bf = b_if[:H].reshape(npair, 2), b_if[H:].reshape(npair, 2)
    bias = jnp.pad(jnp.concatenate([bi, bf], axis=-1), ((0, 0), (0, LANE - 4))).reshape(1, npair * LANE)
    y, c, n, m = pl.pallas_call(
        _mlstm_kernel, name="mlstm",
        grid=(B, npair, L // Q),
        in_specs=[qs, ks, vs, os_, qs, pl.BlockSpec((1, LANE), lambda b, j, c: (0, j)),
                  pl.BlockSpec((1, 2 * DV), lambda b, j, c: (0, j)), cs, ns, ms],
        out_specs=[ys, cs, ns, ms],
        out_shape=[jax.ShapeDtypeStruct((B, L, H * DV), F32), jax.ShapeDtypeStruct((B, H, DV, DK), F32),
                   jax.ShapeDtypeStruct((B, npair, 2, DK), F32), jax.ShapeDtypeStruct((B, npair, 1, 2), F32)],
        scratch_shapes=[pltpu.VMEM((2, DV, DK), F32), pltpu.VMEM((2, DK), F32), pltpu.VMEM((1, 2), F32)],
        compiler_params=_cparams(("parallel", "parallel", "arbitrary")),
    )(proj, proj, proj, proj, gates, bias, norm_w.reshape(1, H * DV), c0,
      n0.reshape(B, npair, 2, DK), m0.reshape(B, npair, 1, 2))
    return y, c, n.reshape(B, H, DK), m.reshape(B, H)


def mlstm_layer(x, shift, scale, gate, nw, c0, n0, m0, w_in, b_if, norm_w, w_o, gb, rb, Q):
    D = x.shape[-1]
    H = ML_HEADS
    nmain = 2 * H * ML_QK_DIM + 2 * H * ML_V_DIM
    h = modnorm(x, shift, scale, nw, gb, rb, BF16)
    proj = mm(h, w_in[:, :nmain])
    wi, wf = w_in[:, nmain:nmain + H].reshape(D, H // 2, 2), w_in[:, nmain + H:].reshape(D, H // 2, 2)
    wg = jnp.pad(jnp.concatenate([wi, wf], axis=-1), ((0, 0), (0, 0), (0, LANE - 4)))
    gates = mm(h, wg.reshape(D, (H // 2) * LANE))
    y, c, n, m = mlstm_scan(proj, gates, b_if, norm_w, c0, n0, m0, Q)
    return oproj(y, None, w_o.astype(BF16), x, gate, gb, rb), c, n, m


def rmsnorm(x, w):
    xf = x.astype(jnp.float32)
    y = xf * lax.rsqrt(jnp.mean(xf * xf, axis=-1, keepdims=True) + NORM_EPS)
    return (y * w).astype(x.dtype)


def sublayer_in(x, g, shift, scale):
    return rmsnorm(x, g) * (1 + scale) + shift


def causal_dwconv(x, buf, w, b):
    L, K = x.shape[1], w.shape[0]
    xp = jnp.concatenate([buf.astype(x.dtype), x], axis=1)
    y = b + sum(xp[:, k:k + L] * w[k] for k in range(K))
    return y, xp[:, -(K - 1):]


def ssd_scan(x, dt, a, bm, cm, s0):
    B, L = x.shape[0], x.shape[1]
    Q = math.gcd(L, SSD_CHUNK)
    nc = L // Q
    chunks = lambda t: jnp.moveaxis(t.reshape(B, nc, Q, *t.shape[2:]), 1, 0)
    causal = jnp.tril(jnp.ones((Q, Q), bool))[None, :, :, None, None]

    def step(s, inp):
        xc, dtc, bc, cc = inp
        xf, bf, cf = xc.astype(jnp.float32), bc.astype(jnp.float32), cc.astype(jnp.float32)
        cs = jnp.cumsum(dtc * a, axis=1)
        decay = jnp.exp(jnp.where(causal, cs[:, :, None] - cs[:, None], -jnp.inf))
        cb = jnp.einsum('btgn,bsgn->btsg', cf, bf)
        wgt = cb[..., None] * decay * dtc[:, None]
        y = jnp.einsum('btsgr,bsgrp->btgrp', wgt, xf)
        y = y + jnp.einsum('btgn,bgrpn->btgrp', cf, s) * jnp.exp(cs)[..., None]
        w_end = jnp.exp(cs[:, -1:] - cs) * dtc
        s_new = s * jnp.exp(cs[:, -1])[..., None, None] + jnp.einsum('bsgr,bsgn,bsgrp->bgrpn', w_end, bf, xf)
        return s_new, y

    s_fin, y = lax.scan(step, s0, (chunks(x), chunks(dt), chunks(bm), chunks(cm)))
    y = jnp.moveaxis(y, 0, 1).reshape(x.shape)
    return y, s_fin


def ssd_mixer(h, conv_buf, ssm0, w_in, conv_w, conv_b, dt_bias, a_log, d_skip, norm_w, w_out):
    B, L, _ = h.shape
    G, R, P, N = SSD_GROUPS, SSD_HPG, SSD_HEADDIM, SSD_STATE
    zxbcdt = mm(h, w_in)
    z = zxbcdt[..., :SSD_D_INNER]
    xbc = zxbcdt[..., SSD_D_INNER:SSD_D_INNER + SSD_CONV_DIM]
    dt_raw = zxbcdt[..., SSD_D_INNER + SSD_CONV_DIM:]
    xbc, new_conv = causal_dwconv(xbc, conv_buf, conv_w, conv_b)
    xbc = jax.nn.silu(xbc)
    xs = xbc[..., :SSD_D_INNER].reshape(B, L, G, R, P)
    bm = xbc[..., SSD_D_INNER:SSD_D_INNER + G * N].reshape(B, L, G, N)
    cm = xbc[..., SSD_D_INNER + G * N:].reshape(B, L, G, N)
    dt = jax.nn.softplus(dt_raw.astype(jnp.float32) + dt_bias).reshape(B, L, G, R)
    a = -jnp.exp(a_log.astype(jnp.float32)).reshape(G, R)
    y, s_fin = ssd_scan(xs, dt, a, bm, cm, ssm0.astype(jnp.float32).reshape(B, G, R, P, N))
    y = y + xs.astype(jnp.float32) * d_skip.reshape(G, R, 1)
    yz = y.reshape(B, L, SSD_D_INNER) * jax.nn.silu(z.astype(jnp.float32))
    yz = rmsnorm(yz.reshape(B, L, G, -1), norm_w.reshape(G, -1)).reshape(B, L, SSD_D_INNER)
    return mm(yz.astype(h.dtype), w_out), new_conv, s_fin.reshape(B, SSD_HEADS, P, N)


def rwkv7_mixer(h, shift_buf, wkv0, mu, w_r, w_k, w_v, w0, w1, w2, a0, a1, a2, g1, g2,
                k_k, k_a, r_k, lnx_w, lnx_b, w_o):
    B, L, D = h.shape
    H, N = RW_HEADS, RW_HEADDIM
    prev = jnp.concatenate([shift_buf[:, None, :].astype(h.dtype), h[:, :-1]], axis=1)
    xx = prev - h
    xr, xw, xk = h + xx * mu[0], h + xx * mu[1], h + xx * mu[2]
    xv, xa, xg = h + xx * mu[3], h + xx * mu[4], h + xx * mu[5]
    r, k, v = mm(xr, w_r), mm(xk, w_k), mm(xv, w_v)
    w_raw = (w0 + mm(jnp.tanh(mm(xw, w1)), w2)).astype(jnp.float32)
    decay = jnp.exp(-jnp.exp(-jax.nn.softplus(-w_raw) - 0.5))
    a = jax.nn.sigmoid(a0 + mm(mm(xa, a1), a2))
    g = mm(jax.nn.sigmoid(mm(xg, g1)), g2)
    heads = lambda t: t.reshape(B, L, H, N).astype(jnp.float32)
    kk = heads(k * k_k)
    kk = kk / jnp.maximum(jnp.sqrt(jnp.sum(kk * kk, axis=-1, keepdims=True)), 1e-12)
    k = k * (1 + (a - 1) * k_a)
    rh, kh, vh, ah, wh = heads(r), heads(k), heads(v), heads(a), heads(decay)
    tm = lambda t: jnp.moveaxis(t, 1, 0)

    def step(S, inp):
        r_t, w_t, k_t, v_t, kk_t, a_t = inp
        sa = jnp.einsum('bhvk,bhk->bhv', S, -kk_t)
        S = S * w_t[:, :, None, :] + sa[..., None] * (kk_t * a_t)[:, :, None, :] + v_t[..., None] * k_t[:, :, None, :]
        return S, jnp.einsum('bhvk,bhk->bhv', S, r_t)

    S_fin, y = lax.scan(step, wkv0.astype(jnp.float32), (tm(rh), tm(wh), tm(kh), tm(vh), tm(kk), tm(ah)))
    y = jnp.moveaxis(y, 0, 1)
    mean = jnp.mean(y, axis=-1, keepdims=True)
    var = jnp.mean(jnp.square(y - mean), axis=-1, keepdims=True)
    y = ((y - mean) * lax.rsqrt(var + RW_LNX_EPS)).reshape(B, L, D) * lnx_w + lnx_b
    bonus = jnp.sum(rh * kh * r_k, axis=-1, keepdims=True) * vh
    y = (y + bonus.reshape(B, L, D)).astype(h.dtype)
    return mm(y * g, w_o), h[:, -1], S_fin


def rope_partial(x, pos):
    half = ROPE_DIM // 2
    inv = ROPE_THETA ** (-jnp.arange(half, dtype=jnp.float32) / half)
    ang = pos.astype(jnp.float32)[:, None] * inv
    cos, sin = jnp.cos(ang)[:, None, :], jnp.sin(ang)[:, None, :]
    x1 = x[..., :half].astype(jnp.float32)
    x2 = x[..., half:ROPE_DIM].astype(jnp.float32)
    rot = jnp.concatenate([x1 * cos - x2 * sin, x2 * cos + x1 * sin], axis=-1).astype(x.dtype)
    return jnp.concatenate([rot, x[..., ROPE_DIM:]], axis=-1)


def moba_qkv(h, pos, w_qkv, q_norm, k_norm):
    B, L, _ = h.shape
    qkv = mm(h, w_qkv).reshape(B, L, 3, MB_HEADS, MB_HEADDIM)
    q = rope_partial(rmsnorm(qkv[:, :, 0], q_norm), pos)
    k = rope_partial(rmsnorm(qkv[:, :, 1], k_norm), pos)
    return q, k, qkv[:, :, 2]


def moba_attend_one(qpos, q, k_all, v_all):
    Lq, Lk = q.shape[0], k_all.shape[0]
    H, Dh, K = MB_HEADS, MB_HEADDIM, MB_TOPK
    nb = max(-(-Lk // MB_BLOCK), K)
    pad = nb * MB_BLOCK - Lk
    blocks = lambda t: jnp.pad(t, ((0, pad), (0, 0), (0, 0))).reshape(nb, MB_BLOCK, H, Dh).transpose(2, 0, 1, 3)
    kb, vb = blocks(k_all), blocks(v_all)
    kmean = jnp.mean(kb.astype(jnp.float32), axis=2)
    cur = qpos // MB_BLOCK
    gate = jnp.einsum('qhd,hnd->hqn', q.astype(jnp.float32), kmean)
    fully_past = jnp.arange(nb)[None, :] < cur[:, None]
    gate = jnp.where(fully_past[None], gate, -jnp.inf)
    _, top = lax.top_k(gate, K)
    sel_ok = jnp.arange(K)[None, :] < cur[:, None]
    idx = jnp.concatenate([top, jnp.broadcast_to(cur[None, :, None], (H, Lq, 1))], axis=-1)
    qb = math.gcd(Lq, MB_QBLOCK)
    nq = Lq // qb
    hidx = jnp.arange(H)[:, None, None]
    scale = MB_HEADDIM ** -0.5

    def one_block(args):
        qs, ps, ids, ok = args
        kg, vg = kb[hidx, ids], vb[hidx, ids]
        s = jnp.einsum('qhd,hqsbd->hqsb', qs, kg).astype(jnp.float32) * scale
        kpos = ids[..., None] * MB_BLOCK + jnp.arange(MB_BLOCK)
        valid = jnp.concatenate([ok, jnp.ones((qb, 1), bool)], axis=-1)
        mask = valid[None, :, :, None] & (kpos <= ps[None, :, None, None])
        s = jnp.where(mask, s, -jnp.inf)
        p = jax.nn.softmax(s.reshape(H, qb, -1), axis=-1).reshape(s.shape)
        return jnp.einsum('hqsb,hqsbd->qhd', p.astype(vg.dtype), vg)

    out = lax.map(one_block, (q.reshape(nq, qb, H, Dh), qpos.reshape(nq, qb),
                              idx.reshape(H, nq, qb, K + 1).transpose(1, 0, 2, 3), sel_ok.reshape(nq, qb, K)))
    return out.reshape(Lq, H, Dh)


def moba_sample_one(qpos, ck, cv, args):
    q, k_new, v_new, pt = args
    k_all = jnp.concatenate([ck[pt].reshape(-1, MB_HEADS, MB_HEADDIM).astype(k_new.dtype), k_new], axis=0)
    v_all = jnp.concatenate([cv[pt].reshape(-1, MB_HEADS, MB_HEADDIM).astype(v_new.dtype), v_new], axis=0)
    return moba_attend_one(qpos, q, k_all, v_all)


def mlstm_chunked(q, k, v, i_pre, log_f, c0, n0, m0):
    B, L = q.shape[0], q.shape[1]
    Q = math.gcd(L, ML_CHUNK)
    nc = L // Q
    chunks = lambda t: jnp.moveaxis(t.reshape(B, nc, Q, *t.shape[2:]), 1, 0)
    causal = jnp.tril(jnp.ones((Q, Q), bool))[None, :, :, None]

    def step(carry, inp):
        C, n, m = carry
        qc, kc, vc, ic, fc = inp
        b = jnp.cumsum(fc, axis=1)
        dmat = jnp.where(causal, b[:, :, None] - b[:, None] + ic[:, None], -jnp.inf)
        m_inter = b + m[:, None]
        m_t = jnp.maximum(m_inter, jnp.max(dmat, axis=2))
        w_intra = jnp.exp(dmat - m_t[:, :, None])
        w_inter = jnp.exp(m_inter - m_t)
        qk = jnp.einsum('bthd,bshd->btsh', qc, kc) * w_intra
        num = jnp.einsum('btsh,bshv->bthv', qk, vc) + w_inter[..., None] * jnp.einsum('bhvd,bthd->bthv', C, qc)
        den = jnp.sum(qk, axis=2) + w_inter * jnp.einsum('bhd,bthd->bth', n, qc)
        hout = num / jnp.maximum(jnp.abs(den), jnp.exp(-m_t))[..., None]
        m_last = m_t[:, -1]
        w_state = jnp.exp(b[:, -1:] - b + ic - m_last[:, None])
        decay = jnp.exp(b[:, -1] + m - m_last)
        C = decay[..., None, None] * C + jnp.einsum('bsh,bshv,bshd->bhvd', w_state, vc, kc)
        n = decay[..., None] * n + jnp.einsum('bsh,bshd->bhd', w_state, kc)
        return (C, n, m_last), hout

    carry0 = (c0.astype(jnp.float32), n0.astype(jnp.float32), m0.astype(jnp.float32))
    (C, n, m), hout = lax.scan(step, carry0, (chunks(q), chunks(k), chunks(v), chunks(i_pre), chunks(log_f)))
    return jnp.moveaxis(hout, 0, 1).reshape(v.shape), C, n, m


def mlstm_mixer(h, c0, n0, m0, w_in, b_if, norm_w, w_o):
    B, L, _ = h.shape
    H, DK, DV = ML_HEADS, ML_QK_DIM, ML_V_DIM
    proj = mm(h, w_in)
    o1, o2, o3, o4 = H * DK, 2 * H * DK, 2 * H * DK + H * DV, 2 * H * DK + 2 * H * DV
    q = proj[..., :o1].reshape(B, L, H, DK).astype(jnp.float32) * DK ** -0.5
    k = proj[..., o1:o2].reshape(B, L, H, DK).astype(jnp.float32)
    v = proj[..., o2:o3].reshape(B, L, H, DV).astype(jnp.float32)
    o = jax.nn.sigmoid(proj[..., o3:o4])
    gates = proj[..., o4:].astype(jnp.float32) + b_if
    i_pre, log_f = gates[..., :H], jax.nn.log_sigmoid(gates[..., H:])
    hh, C, n, m = mlstm_chunked(q, k, v, i_pre, log_f, c0, n0, m0)
    hh = rmsnorm(hh, norm_w.reshape(H, DV)).reshape(B, L, H * DV)
    return mm(o * hh.astype(h.dtype), w_o), C, n, m


def kernel(x_prompt, x_sample, state_ssd_conv, state_ssd, state_rwkv_shift, state_rwkv, cache_k, cache_v, state_mlstm_c, state_mlstm_n, state_mlstm_m, page_table, c_prompt, c_sample, ada_w, ada_b, norm_w, ffn_w_in, ffn_w_out, ssd_w_in, ssd_conv_w, ssd_conv_b, ssd_dt_bias, ssd_a_log, ssd_d, ssd_norm_w, ssd_w_out, rw_mu, rw_w_r, rw_w_k, rw_w_v, rw_w0, rw_w1, rw_w2, rw_a0, rw_a1, rw_a2, rw_g1, rw_g2, rw_k_k, rw_k_a, rw_r_k, rw_lnx_w, rw_lnx_b, rw_w_o, mb_w_qkv, mb_q_norm, mb_k_norm, mb_w_o, ml_w_in, ml_b_if, ml_norm_w, ml_w_o):
    B, S = x_prompt.shape[0], x_prompt.shape[1]
    Bs, Ls = x_sample.shape[0], x_sample.shape[1]
    D = D_MODEL
    pos_p = jnp.arange(S, dtype=jnp.int32)
    pos_s = PAST_LEN + jnp.arange(Ls, dtype=jnp.int32)
    xp, xs = x_prompt, x_sample
    outs = {}
    c_all = jnp.concatenate([c_prompt, c_sample], axis=0)
    sc_all = jax.nn.silu(c_all)
    for i in range(DEPTH):
        kind, j = i % N_MIXERS, i // N_MIXERS
        mod = (mm(sc_all, ada_w[i]) + ada_b[i]).reshape(B + Bs, N_MOD, 1, D)
        mp, ms = mod[:B], mod[B:]
        wi0, wo0 = ffn_w_in[i, 0].astype(BF16), ffn_w_out[i, 0].astype(BF16)
        wi1, wo1 = ffn_w_in[i, 1].astype(BF16), ffn_w_out[i, 1].astype(BF16)
        xp = ffn(xp, mp[:, 0], mp[:, 1], mp[:, 2], norm_w[i, 0], wi0, wo0, 1, 512)
        xs = ffn(xs, ms[:, 0], ms[:, 1], ms[:, 2], norm_w[i, 0], wi0, wo0, 64, Ls)
        m1p, m1s = (mp[:, 3], mp[:, 4], mp[:, 5], norm_w[i, 1]), (ms[:, 3], ms[:, 4], ms[:, 5], norm_w[i, 1])
        if kind == 0:
            w = (ssd_w_in[j], ssd_conv_w[j], ssd_conv_b[j], ssd_dt_bias[j], ssd_a_log[j], ssd_d[j], ssd_norm_w[j], ssd_w_out[j])
            xp, cp, sp = ssd_layer(xp, *m1p, jnp.zeros((B, SSD_CONV - 1, SSD_CONV_DIM), F32),
                                   jnp.zeros((B, SSD_HEADS, SSD_HEADDIM, SSD_STATE), F32), *w, 1, 512, SSD_CHUNK)
            xs, cs_, ss = ssd_layer(xs, *m1s, state_ssd_conv[j], state_ssd[j], *w, 64, Ls, Ls)
            outs[2], outs[3], outs[4], outs[5] = cp[None], cs_[None], sp[None], ss[None]
        elif kind == 1:
            w = (rw_mu[j], rw_w_r[j], rw_w_k[j], rw_w_v[j], rw_w0[j], rw_w1[j], rw_w2[j], rw_a0[j], rw_a1[j],
                 rw_a2[j], rw_g1[j], rw_g2[j], rw_k_k[j], rw_k_a[j], rw_r_k[j], rw_lnx_w[j], rw_lnx_b[j], rw_w_o[j])
            xp, shp, sp = rwkv_layer(xp, *m1p, jnp.zeros((B, D), F32),
                                     jnp.zeros((B, RW_HEADS, RW_HEADDIM, RW_HEADDIM), F32), *w, 1, 256, 64, 4)
            xs, shs, ss = rwkv_layer(xs, *m1s, state_rwkv_shift[j], state_rwkv[j], *w, 32, Ls, Ls, RW_HEADS)
            outs[6], outs[7], outs[8], outs[9] = shp[None], shs[None], sp[None], ss[None]
        elif kind == 2:
            hp = modnorm(xp, mp[:, 3], mp[:, 4], norm_w[i, 1], 1, 512, BF16)
            hs = modnorm(xs, ms[:, 3], ms[:, 4], norm_w[i, 1], 64, Ls, BF16)
            qkv_p, qkv_s = mm(hp, mb_w_qkv[j]), mm(hs, mb_w_qkv[j])
            qp, kp = qkpost(qkv_p, pos_p, mb_q_norm[j], mb_k_norm[j], 1, 512)
            qs, ks_ = qkpost(qkv_s, pos_s, mb_q_norm[j], mb_k_norm[j], 16, Ls)
            vp, vs = qkv_p[..., 2 * D:], qkv_s[..., 2 * D:]
            op = moba_prompt_attend_t(qp, kp, vp)
            nl, npool, page = cache_k.shape[0], cache_k.shape[1], cache_k.shape[2]
            os_ = moba_sample_attend(qs, ks_, vs, cache_k.reshape(nl * npool, page, D),
                                     cache_v.reshape(nl * npool, page, D), page_table + j * npool)
            w_o = mb_w_o[j].astype(BF16)
            xp = oproj(op, None, w_o, xp, mp[:, 5], 1, 512)
            xs = oproj(os_, None, w_o, xs, ms[:, 5], 64, Ls)
            hd = (MB_HEADS, MB_HEADDIM)
            outs[10], outs[11] = kp.reshape(1, B, S, *hd), ks_.reshape(1, Bs, Ls, *hd)
            outs[12], outs[13] = vp.reshape(1, B, S, *hd), vs.reshape(1, Bs, Ls, *hd)
        else:
            w = (ml_w_in[j], ml_b_if[j], ml_norm_w[j], ml_w_o[j])
            xp, cp, np_, mp_ = mlstm_layer(xp, *m1p, jnp.zeros((B, ML_HEADS, ML_V_DIM, ML_QK_DIM), F32),
                                           jnp.zeros((B, ML_HEADS, ML_QK_DIM), F32), jnp.zeros((B, ML_HEADS), F32),
                                           *w, 1, 512, 128)
            xs, cs_, ns_, ms_ = mlstm_layer(xs, *m1s, state_mlstm_c[j], state_mlstm_n[j], state_mlstm_m[j],
                                            *w, 64, Ls, Ls)
            outs[14], outs[15], outs[16], outs[17], outs[18], outs[19] = cp[None], cs_[None], np_[None], ns_[None], mp_[None], ms_[None]
        xp = ffn(xp, mp[:, 6], mp[:, 7], mp[:, 8], norm_w[i, 2], wi1, wo1, 1, 512)
        xs = ffn(xs, ms[:, 6], ms[:, 7], ms[:, 8], norm_w[i, 2], wi1, wo1, 64, Ls)
    outs[0], outs[1] = xp, xs
    return tuple(outs[i] for i in range(20))
```

```python
import functools
import math

import jax
import jax.numpy as jnp
from jax import lax
from jax.experimental import pallas as pl
from jax.experimental.pallas import tpu as pltpu

D_MODEL = 1024
DEPTH = 4
PAST_LEN = 2048
N_MIXERS = 4
NORM_EPS = 1e-6
N_MOD = 9
D_FF = 2816

SSD_D_INNER = 2 * D_MODEL
SSD_HEADDIM = 64
SSD_HEADS = SSD_D_INNER // SSD_HEADDIM
SSD_GROUPS = 4
SSD_HPG = SSD_HEADS // SSD_GROUPS
SSD_STATE = 128
SSD_CONV = 4
SSD_CHUNK = 128
SSD_CONV_DIM = SSD_D_INNER + 2 * SSD_GROUPS * SSD_STATE

RW_HEADDIM = 64
RW_HEADS = D_MODEL // RW_HEADDIM
RW_LNX_EPS = 64e-5

MB_HEADS = 16
MB_HEADDIM = D_MODEL // MB_HEADS
MB_BLOCK = 256
MB_TOPK = 3
MB_QBLOCK = 16
ROPE_DIM = MB_HEADDIM // 4
ROPE_THETA = 500000.0

ML_HEADS = 8
ML_QK_DIM = D_MODEL // 2 // ML_HEADS
ML_V_DIM = D_MODEL // ML_HEADS
ML_CHUNK = 64

LANE = 128
VMEM_LIMIT = 48 * 1024 * 1024

BF16 = jnp.bfloat16
F32 = jnp.float32


def _cparams(sem):
    return pltpu.CompilerParams(dimension_semantics=sem, vmem_limit_bytes=VMEM_LIMIT)


def _mm_kernel(a_ref, w_ref, o_ref):
    o_ref[...] = jnp.dot(a_ref[...].astype(BF16), w_ref[...], preferred_element_type=F32)


def _pick(n, cands):
    for c in cands:
        if n % c == 0:
            return c
    raise ValueError(n)


def mm(a, w):
    lead = a.shape[:-1]
    K = a.shape[-1]
    N = w.shape[-1]
    a2 = a.reshape(-1, K)
    M = a2.shape[0]
    Mp = -(-M // 8) * 8
    if Mp != M:
        a2 = jnp.pad(a2, ((0, Mp - M), (0, 0)))
    Np = -(-N // LANE) * LANE
    wb = w.astype(BF16)
    if Np != N:
        wb = jnp.pad(wb, ((0, 0), (0, Np - N)))
    tm = _pick(Mp, (512, 256, 128, 64, 32, 16, 8))
    tn = _pick(Np, (1024, 512, 256, 128))
    out = pl.pallas_call(
        _mm_kernel, name="mm",
        grid=(Mp // tm, Np // tn),
        in_specs=[pl.BlockSpec((tm, K), lambda i, j: (i, 0)),
                  pl.BlockSpec((K, tn), lambda i, j: (0, j))],
        out_specs=pl.BlockSpec((tm, tn), lambda i, j: (i, j)),
        out_shape=jax.ShapeDtypeStruct((Mp, Np), F32),
        compiler_params=_cparams(("parallel", "arbitrary")),
    )(a2, wb)
    return out[:M, :N].reshape(*lead, N)


def _ffn_kernel(x_ref, sh_ref, sc_ref, gt_ref, nw_ref, wg_ref, wu_ref, wo_ref, o_ref, h_sc, acc_sc):
    k = pl.program_id(2)
    gb, rb, d = x_ref.shape

    @pl.when(k == 0)
    def _():
        x = x_ref[...]
        y = x * lax.rsqrt(jnp.mean(x * x, axis=-1, keepdims=True) + NORM_EPS) * nw_ref[...]
        h = y * (1.0 + sc_ref[...]) + sh_ref[...]
        h_sc[...] = h.reshape(gb * rb, d).astype(BF16)

    h = h_sc[...]
    g = jnp.dot(h, wg_ref[...], preferred_element_type=F32)
    u = jnp.dot(h, wu_ref[...], preferred_element_type=F32)
    a = (g * jax.nn.sigmoid(g) * u).astype(BF16)
    part = jnp.dot(a, wo_ref[...], preferred_element_type=F32)

    @pl.when(k == 0)
    def _():
        acc_sc[...] = part

    @pl.when(k > 0)
    def _():
        acc_sc[...] += part

    @pl.when(k == pl.num_programs(2) - 1)
    def _():
        o_ref[...] = x_ref[...] + 0.5 * gt_ref[...] * acc_sc[...].reshape(gb, rb, d)


def ffn(x, shift, scale, gate, nw, w_in_b, w_out_b, gb, rb):
    G, R, D = x.shape
    tf = D_FF // 2
    nk = D_FF // tf
    xspec = pl.BlockSpec((gb, rb, D), lambda g, r, k: (g, r, 0))
    mspec = pl.BlockSpec((gb, 1, D), lambda g, r, k: (g, 0, 0))
    return pl.pallas_call(
        _ffn_kernel, name="ffn",
        grid=(G // gb, R // rb, nk),
        in_specs=[xspec, mspec, mspec, mspec,
                  pl.BlockSpec((1, 1, D), lambda g, r, k: (0, 0, 0)),
                  pl.BlockSpec((D, tf), lambda g, r, k: (0, k)),
                  pl.BlockSpec((D, tf), lambda g, r, k: (0, k + nk)),
                  pl.BlockSpec((tf, D), lambda g, r, k: (k, 0))],
        out_specs=xspec,
        out_shape=jax.ShapeDtypeStruct(x.shape, F32),
        scratch_shapes=[pltpu.VMEM((gb * rb, D), BF16), pltpu.VMEM((gb * rb, D), F32)],
        compiler_params=_cparams(("parallel", "parallel", "arbitrary")),
    )(x, shift, scale, gate, nw.reshape(1, 1, D), w_in_b, w_in_b, w_out_b)


HI = lax.Precision.HIGHEST
NN = ((1,), (0,))
NT = ((1,), (1,))
TN = ((0,), (0,))
NEG = -1e30


def _dot(a, b, dims=NN, hi=False):
    if hi:
        return lax.dot_general(a, b, (dims, ((), ())), precision=HI, preferred_element_type=F32)
    return lax.dot_general(a.astype(BF16), b.astype(BF16), (dims, ((), ())), preferred_element_type=F32)


def _split(a):
    hi = a.astype(BF16)
    return hi, (a - hi.astype(F32)).astype(BF16)


def _dot3s(sa, sb, dims=NN):
    d = lambda x, y: lax.dot_general(x, y, (dims, ((), ())), preferred_element_type=F32)
    return d(sa[0], sb[0]) + d(sa[0], sb[1]) + d(sa[1], sb[0])


def _dot3(a, b, dims=NN):
    return _dot3s(_split(a), _split(b), dims)


def _modnorm_kernel(x_ref, sh_ref, sc_ref, nw_ref, o_ref):
    x = x_ref[...]
    y = x * lax.rsqrt(jnp.mean(x * x, axis=-1, keepdims=True) + NORM_EPS) * nw_ref[...]
    o_ref[...] = (y * (1.0 + sc_ref[...]) + sh_ref[...]).astype(o_ref.dtype)


def modnorm(x, shift, scale, nw, gb, rb, dtype):
    G, R, D = x.shape
    xspec = pl.BlockSpec((gb, rb, D), lambda g, r: (g, r, 0))
    mspec = pl.BlockSpec((gb, 1, D), lambda g, r: (g, 0, 0))
    return pl.pallas_call(
        _modnorm_kernel, name="modnorm",
        grid=(G // gb, R // rb),
        in_specs=[xspec, mspec, mspec, pl.BlockSpec((1, 1, D), lambda g, r: (0, 0, 0))],
        out_specs=xspec,
        out_shape=jax.ShapeDtypeStruct(x.shape, dtype),
        compiler_params=_cparams(("parallel", "parallel")),
    )(x, shift, scale, nw.reshape(1, 1, D))


def _oproj_kernel(*refs, has_mul):
    if has_mul:
        y_ref, mul_ref, w_ref, x_ref, gt_ref, o_ref = refs
    else:
        y_ref, w_ref, x_ref, gt_ref, o_ref = refs
    gb, rb, kd = y_ref.shape
    y = y_ref[...]
    if has_mul:
        y = y * mul_ref[...]
    acc = jnp.dot(y.reshape(gb * rb, kd).astype(BF16), w_ref[...], preferred_element_type=F32)
    o_ref[...] = x_ref[...] + gt_ref[...] * acc.reshape(gb, rb, acc.shape[-1])


def oproj(y, mul, w_b, x, gate, gb, rb):
    G, R, Kd = y.shape
    D = x.shape[-1]
    yspec = pl.BlockSpec((gb, rb, Kd), lambda g, r: (g, r, 0))
    xspec = pl.BlockSpec((gb, rb, D), lambda g, r: (g, r, 0))
    ins = [y] + ([mul] if mul is not None else []) + [w_b, x, gate]
    specs = [yspec] + ([yspec] if mul is not None else []) + [
        pl.BlockSpec((Kd, D), lambda g, r: (0, 0)), xspec,
        pl.BlockSpec((gb, 1, D), lambda g, r: (g, 0, 0))]
    return pl.pallas_call(
        functools.partial(_oproj_kernel, has_mul=mul is not None), name="oproj",
        grid=(G // gb, R // rb),
        in_specs=specs,
        out_specs=xspec,
        out_shape=jax.ShapeDtypeStruct(x.shape, F32),
        compiler_params=_cparams(("parallel", "parallel")),
    )(*ins)


def _qkpost_kernel(q_ref, k_ref, cos_ref, sin_ref, qw_ref, kw_ref, qo_ref, ko_ref):
    gb, rb, w = q_ref.shape
    n = gb * rb
    lane = lax.broadcasted_iota(jnp.int32, (n, w), 1)
    lo = lane < MB_HEADDIM
    first = (lane % MB_HEADDIM) < (ROPE_DIM // 2)
    cos = jnp.broadcast_to(cos_ref[...][None], (gb, rb, w)).reshape(n, w)
    sin = jnp.broadcast_to(sin_ref[...][None], (gb, rb, w)).reshape(n, w)

    def norm_rope(x, wt):
        x2 = x * x
        s_lo = jnp.sum(jnp.where(lo, x2, 0.0), axis=-1, keepdims=True)
        s_hi = jnp.sum(jnp.where(lo, 0.0, x2), axis=-1, keepdims=True)
        ms = jnp.where(lo, s_lo, s_hi) * (1.0 / MB_HEADDIM)
        y = x * lax.rsqrt(ms + NORM_EPS) * wt
        up = pltpu.roll(y, w - ROPE_DIM // 2, 1)
        dn = pltpu.roll(y, ROPE_DIM // 2, 1)
        return y * cos + jnp.where(first, up, dn) * sin

    qo_ref[...] = norm_rope(q_ref[...].reshape(n, w), qw_ref[...]).reshape(gb, rb, w)
    ko_ref[...] = norm_rope(k_ref[...].reshape(n, w), kw_ref[...]).reshape(gb, rb, w)


def _rope_tables(pos):
    half = ROPE_DIM // 2
    inv = ROPE_THETA ** (-jnp.arange(half, dtype=F32) / half)
    ang = pos.astype(F32)[:, None] * inv
    c, s = jnp.cos(ang), jnp.sin(ang)
    ones = jnp.ones((pos.shape[0], MB_HEADDIM - ROPE_DIM), F32)
    ch = jnp.concatenate([c, c, ones], axis=-1)
    sh = jnp.concatenate([-s, s, 0.0 * ones], axis=-1)
    return jnp.concatenate([ch, ch], axis=-1), jnp.concatenate([sh, sh], axis=-1)


def qkpost(qkv, pos, q_norm, k_norm, gb, rb):
    G, R, _ = qkv.shape
    D = D_MODEL
    npair = D // LANE
    cos, sin = _rope_tables(pos)
    tspec = pl.BlockSpec((rb, LANE), lambda g, r, p: (r, 0))
    wspec = pl.BlockSpec((1, LANE), lambda g, r, p: (0, 0))
    ospec = pl.BlockSpec((gb, rb, LANE), lambda g, r, p: (g, r, p))
    return pl.pallas_call(
        _qkpost_kernel, name="qkpost",
        grid=(G // gb, R // rb, npair),
        in_specs=[pl.BlockSpec((gb, rb, LANE), lambda g, r, p: (g, r, p)),
                  pl.BlockSpec((gb, rb, LANE), lambda g, r, p: (g, r, p + npair)),
                  tspec, tspec, wspec, wspec],
        out_specs=[ospec, ospec],
        out_shape=[jax.ShapeDtypeStruct((G, R, D), F32)] * 2,
        compiler_params=_cparams(("parallel", "parallel", "parallel")),
    )(qkv, qkv, cos, sin, jnp.tile(q_norm, 2).reshape(1, LANE), jnp.tile(k_norm, 2).reshape(1, LANE))


def _top3_mask(g, idx, nb):
    sel = jnp.zeros(g.shape, jnp.bool_)
    for _ in range(MB_TOPK):
        mx = jnp.max(g, axis=-1, keepdims=True)
        first = jnp.min(jnp.where(g == mx, idx, nb), axis=-1, keepdims=True)
        pick = idx == first
        sel = jnp.logical_or(sel, pick)
        g = jnp.where(pick, -jnp.inf, g)
    return sel


def _moba_prompt_kernel(q_ref, k_ref, v_ref, o_ref, kmean_sc, selx_sc):
    c = pl.program_id(2)
    T = MB_BLOCK
    nb = k_ref.shape[1] // T
    scale = MB_HEADDIM ** -0.5

    @pl.when(c == 0)
    def _():
        for n in range(nb):
            kmean_sc[n:n + 1, :] = jnp.mean(k_ref[0, n * T:(n + 1) * T, :], axis=0, keepdims=True)

    q = q_ref[0]
    lo = lax.broadcasted_iota(jnp.int32, (T, LANE), 1) < MB_HEADDIM
    qh = (jnp.where(lo, q, 0.0), jnp.where(lo, 0.0, q))
    blk = lax.broadcasted_iota(jnp.int32, (T, nb), 1)
    for hh in range(2):
        g = _dot(qh[hh], kmean_sc[...], NT, hi=True)
        g = jnp.where(blk < c, g, -jnp.inf)
        sel = jnp.logical_and(_top3_mask(g, blk, nb), blk < c).astype(F32)
        for n in range(nb):
            selx_sc[hh, n] = jnp.broadcast_to(sel[:, n:n + 1], (T, LANE))
    qb = tuple((x * scale).astype(BF16) for x in qh)

    def blocks(n):
        st = pl.multiple_of(n * T, T)
        return k_ref[0, pl.ds(st, T), :].astype(BF16), v_ref[0, pl.ds(st, T), :].astype(BF16)

    kc, vc = blocks(c)
    causal = lax.broadcasted_iota(jnp.int32, (T, T), 0) >= lax.broadcasted_iota(jnp.int32, (T, T), 1)
    init = []
    for hh in range(2):
        s = jnp.where(causal, _dot(qb[hh], kc, NT), NEG)
        m = jnp.max(s, axis=-1, keepdims=True)
        p = jnp.exp(s - m)
        init += [m, jnp.sum(p, axis=-1, keepdims=True), _dot(p, vc)]

    def body(n, carry):
        kn, vn = blocks(n)
        out = []
        for hh in range(2):
            m, l, acc = carry[3 * hh:3 * hh + 3]
            keep = selx_sc[hh, n] > 0.5
            s = _dot(qb[hh], kn, NT)
            s0 = jnp.where(keep, s[:, :LANE], NEG)
            s1 = jnp.where(keep, s[:, LANE:], NEG)
            m_new = jnp.maximum(m, jnp.max(jnp.maximum(s0, s1), axis=-1, keepdims=True))
            alpha = jnp.exp(m - m_new)
            p0 = jnp.exp(s0 - m_new)
            p1 = jnp.exp(s1 - m_new)
            l = alpha * l + jnp.sum(p0 + p1, axis=-1, keepdims=True)
            acc = alpha * acc + _dot(p0, vn[:LANE]) + _dot(p1, vn[LANE:])
            out += [m_new, l, acc]
        return tuple(out)

    res = lax.fori_loop(0, c, body, tuple(init))
    o_ref[0] = jnp.where(lo, res[2] / res[1], res[5] / res[4])


def moba_prompt_attend(q, k, v):
    B, L, D = q.shape
    nb = L // MB_BLOCK
    kvspec = pl.BlockSpec((1, L, LANE), lambda b, p, c: (b, 0, p))
    qspec = pl.BlockSpec((1, MB_BLOCK, LANE), lambda b, p, c: (b, c, p))
    return pl.pallas_call(
        _moba_prompt_kernel, name="moba_prompt",
        grid=(B, D // LANE, nb),
        in_specs=[qspec, kvspec, kvspec],
        out_specs=qspec,
        out_shape=jax.ShapeDtypeStruct((B, L, D), F32),
        scratch_shapes=[pltpu.VMEM((nb, LANE), F32), pltpu.VMEM((2, nb, MB_BLOCK, LANE), F32)],
        compiler_params=_cparams(("parallel", "parallel", "arbitrary")),
    )(q, k, v)


def _top3_mask0(g, idx, nb):
    sel = jnp.zeros(g.shape, jnp.bool_)
    for _ in range(MB_TOPK):
        mx = jnp.max(g, axis=0, keepdims=True)
        first = jnp.min(jnp.where(g == mx, idx, nb), axis=0, keepdims=True)
        pick = idx == first
        sel = jnp.logical_or(sel, pick)
        g = jnp.where(pick, -jnp.inf, g)
    return sel


def _moba_prompt_t_kernel(q_ref, k_ref, v_ref, o_ref, kmean_sc, kb_sc, vt_sc, selt_sc):
    c = pl.program_id(2)
    T = MB_BLOCK
    P = MB_HEADDIM
    nb = k_ref.shape[1] // T
    scale = MB_HEADDIM ** -0.5

    @pl.when(c == 0)
    def _():
        for n in range(nb):
            kn = k_ref[0, n * T:(n + 1) * T, :]
            kmean_sc[n:n + 1, :] = jnp.mean(kn, axis=0, keepdims=True)
            kb_sc[n] = kn.astype(BF16)
            vt_sc[n] = v_ref[0, n * T:(n + 1) * T, :].T.astype(BF16)

    q = q_ref[0]
    lo = lax.broadcasted_iota(jnp.int32, (T, LANE), 1) < P
    qh = (jnp.where(lo, q, 0.0), jnp.where(lo, 0.0, q))
    blk = lax.broadcasted_iota(jnp.int32, (nb, T), 0)
    for hh in range(2):
        g = _dot(kmean_sc[...], qh[hh], NT, hi=True)
        g = jnp.where(blk < c, g, -jnp.inf)
        selt_sc[hh] = jnp.logical_and(_top3_mask0(g, blk, nb), blk < c).astype(F32)
    qb = tuple((x * scale).astype(BF16) for x in qh)
    causal = lax.broadcasted_iota(jnp.int32, (T, T), 0) <= lax.broadcasted_iota(jnp.int32, (T, T), 1)
    kc, vc = kb_sc[c], vt_sc[c]
    init = []
    for hh in range(2):
        s = jnp.where(causal, _dot(kc, qb[hh], NT), NEG)
        m = jnp.max(s, axis=0, keepdims=True)
        p = jnp.exp(s - m)
        init += [m, jnp.sum(p, axis=0, keepdims=True), _dot(vc[hh * P:(hh + 1) * P, :], p)]

    def scores(i):
        return tuple(_dot(kb_sc[jnp.minimum(2 * i + d, nb - 1)], qb[hh], NT) for hh in range(2) for d in range(2))

    def body(i, carry):
        state, s_cur = carry
        s_next = scores(i + 1)
        ns = (2 * i, 2 * i + 1)
        out = []
        for hh in range(2):
            m, l, acc = state[3 * hh:3 * hh + 3]
            ss = [jnp.where(selt_sc[hh, pl.ds(n, 1), :] > 0.5, s_cur[2 * hh + d], NEG) for d, n in enumerate(ns)]
            m_new = jnp.maximum(m, jnp.max(jnp.maximum(ss[0], ss[1]), axis=0, keepdims=True))
            alpha = jnp.exp(m - m_new)
            ps = [jnp.exp(s - m_new) for s in ss]
            l = alpha * l + jnp.sum(ps[0] + ps[1], axis=0, keepdims=True)
            acc = alpha * acc + sum(_dot(vt_sc[n, hh * P:(hh + 1) * P, :], p) for n, p in zip(ns, ps))
            out += [m_new, l, acc]
        return tuple(out), s_next

    res, _ = lax.fori_loop(0, (c + 1) // 2, body, (tuple(init), scores(0)))
    o_ref[0] = jnp.concatenate([res[2] / res[1], res[5] / res[4]], axis=0).T


def moba_prompt_attend_t(q, k, v):
    B, L, D = q.shape
    nb = L // MB_BLOCK
    kvspec = pl.BlockSpec((1, L, LANE), lambda b, p, c: (b, 0, p))
    qspec = pl.BlockSpec((1, MB_BLOCK, LANE), lambda b, p, c: (b, c, p))
    return pl.pallas_call(
        _moba_prompt_t_kernel, name="moba_prompt",
        grid=(B, D // LANE, nb),
        in_specs=[qspec, kvspec, kvspec],
        out_specs=qspec,
        out_shape=jax.ShapeDtypeStruct((B, L, D), F32),
        scratch_shapes=[pltpu.VMEM((nb, LANE), F32), pltpu.VMEM((nb, MB_BLOCK, LANE), BF16),
                        pltpu.VMEM((nb, LANE, MB_BLOCK), BF16), pltpu.VMEM((2, nb, MB_BLOCK), F32)],
        compiler_params=_cparams(("parallel", "parallel", "arbitrary")),
    )(q, k, v)


def _moba_sample_kernel(pt_ref, q_ref, kn_ref, vn_ref, k0_ref, k1_ref, v0_ref, v1_ref, o_ref,
                        qbd_sc, qbdb_sc, kmean_sc, o_sc, m_sc, l_sc):
    del pt_ref
    n = pl.program_id(1)
    nblk = pl.num_programs(1)
    lq, d = q_ref.shape[1], q_ref.shape[2]
    rows = MB_HEADS * lq
    scale = MB_HEADDIM ** -0.5
    own = (lax.broadcasted_iota(jnp.int32, (MB_HEADS, lq, d), 0)
           == lax.broadcasted_iota(jnp.int32, (MB_HEADS, lq, d), 2) // MB_HEADDIM)

    @pl.when(n == 0)
    def _():
        qbd = jnp.where(own, q_ref[0][None], 0.0).reshape(rows, d)
        qbd_sc[...] = qbd
        qbdb_sc[...] = (qbd * scale).astype(BF16)

    qb = qbdb_sc[...]
    page2d = lambda ref: jnp.concatenate([ref[0, :, h, :] for h in range(MB_HEADS)], axis=-1)
    k0, k1 = page2d(k0_ref), page2d(k1_ref)
    s0 = _dot(qb, k0, NT)
    s1 = _dot(qb, k1, NT)
    m = jnp.maximum(jnp.max(s0, axis=-1, keepdims=True), jnp.max(s1, axis=-1, keepdims=True))
    p0 = jnp.exp(s0 - m)
    p1 = jnp.exp(s1 - m)
    o_sc[n] = _dot(p0, page2d(v0_ref)) + _dot(p1, page2d(v1_ref))
    m_sc[n] = m
    l_sc[n] = jnp.sum(p0, axis=-1, keepdims=True) + jnp.sum(p1, axis=-1, keepdims=True)
    kmean_sc[n] = (jnp.sum(k0, axis=0, keepdims=True) + jnp.sum(k1, axis=0, keepdims=True)) * (1.0 / MB_BLOCK)

    @pl.when(n == nblk - 1)
    def _():
        qbd = qbd_sc[...]
        nblocks = o_sc.shape[0]
        s = lax.dot_general(qbd * scale, kn_ref[0], (NT, ((), ())), preferred_element_type=F32)
        qi = lax.broadcasted_iota(jnp.int32, (rows, lq), 0) % lq
        s = jnp.where(lax.broadcasted_iota(jnp.int32, (rows, lq), 1) <= qi, s, NEG)
        m_own = jnp.max(s, axis=-1, keepdims=True)
        p = jnp.exp(s - m_own)
        l_own = jnp.sum(p, axis=-1, keepdims=True)
        o_own = jnp.dot(p, vn_ref[0], preferred_element_type=F32)
        g = [jnp.sum(qbd * kmean_sc[j], axis=-1, keepdims=True) for j in range(nblocks)]
        sel = []
        for j in range(nblocks):
            rank = jnp.zeros((rows, 1), jnp.int32)
            for i in range(nblocks):
                if i != j:
                    ahead = (g[i] >= g[j]) if i < j else (g[i] > g[j])
                    rank = rank + ahead.astype(jnp.int32)
            sel.append(rank < MB_TOPK)
        mx = m_own
        for j in range(nblocks):
            mx = jnp.where(sel[j], jnp.maximum(mx, m_sc[j]), mx)
        w_own = jnp.exp(m_own - mx)
        l = l_own * w_own
        o = o_own * w_own
        for j in range(nblocks):
            wj = jnp.where(sel[j], jnp.exp(m_sc[j] - mx), 0.0)
            l = l + wj * l_sc[j]
            o = o + wj * o_sc[j]
        res = (o / l).reshape(MB_HEADS, lq, d)
        o_ref[0] = jnp.sum(jnp.where(own, res, 0.0), axis=0)


def moba_sample_attend(q, k_new, v_new, cache_k, cache_v, page_table):
    Bs, Ls, D = q.shape
    page = cache_k.shape[1]
    assert MB_BLOCK == 2 * page and PAST_LEN % MB_BLOCK == 0 and Ls <= MB_BLOCK
    nblk = PAST_LEN // MB_BLOCK
    assert nblk >= MB_TOPK
    rows = MB_HEADS * Ls
    newspec = pl.BlockSpec((1, Ls, D), lambda b, n, pt: (b, 0, 0))
    pshape = (1, page, MB_HEADS, MB_HEADDIM)
    pg0 = pl.BlockSpec(pshape, lambda b, n, pt: (pt[b, 2 * n], 0, 0, 0))
    pg1 = pl.BlockSpec(pshape, lambda b, n, pt: (pt[b, 2 * n + 1], 0, 0, 0))
    return pl.pallas_call(
        _moba_sample_kernel, name="moba_sample",
        grid_spec=pltpu.PrefetchScalarGridSpec(
            num_scalar_prefetch=1,
            grid=(Bs, nblk),
            in_specs=[newspec, newspec, newspec, pg0, pg1, pg0, pg1],
            out_specs=newspec,
            scratch_shapes=[pltpu.VMEM((rows, D), F32), pltpu.VMEM((rows, D), BF16),
                            pltpu.VMEM((nblk, 1, D), F32), pltpu.VMEM((nblk, rows, D), F32),
                            pltpu.VMEM((nblk, rows, 1), F32), pltpu.VMEM((nblk, rows, 1), F32)]),
        out_shape=jax.ShapeDtypeStruct((Bs, Ls, D), F32),
        compiler_params=_cparams(("parallel", "arbitrary")),
    )(page_table, q, k_new, v_new, cache_k, cache_k, cache_v, cache_v)


def _softplus(z):
    return jnp.maximum(z, 0.0) + jnp.log1p(jnp.exp(-jnp.abs(z)))


def _sigmoid(z):
    return 1.0 / (1.0 + jnp.exp(-z))


def _rwkv_pre_kernel(x_ref, halo_ref, sb_ref, sh_ref, sc_ref, nw_ref, mu_ref, w0_ref, a0_ref,
                     wr_ref, wk_ref, wv_ref, w1_ref, w2_ref, a1_ref, a2_ref, g1_ref, g2_ref,
                     r_o, lw_o, k_o, v_o, a_o, g_o, hl_o, hbuf):
    rr = pl.program_id(1)
    gb, rb, d = x_ref.shape
    n = gb * rb

    def hn(x):
        y = x * lax.rsqrt(jnp.mean(x * x, axis=-1, keepdims=True) + NORM_EPS) * nw_ref[...]
        return y * (1.0 + sc_ref[...]) + sh_ref[...]

    h = hn(x_ref[...])
    prev_row = jnp.where(rr == 0, sb_ref[...], hn(halo_ref[...])[:, 7:8, :])
    hbuf[:, 8:, :] = h
    hbuf[:, 7:8, :] = prev_row
    h2 = h.reshape(n, d)
    xx = hbuf[:, 7:7 + rb, :].reshape(n, d) - h2
    mix = lambda i: h2 + xx * mu_ref[i:i + 1, :]
    dot = lambda a, w_ref: jnp.dot(a.astype(BF16), w_ref[...], preferred_element_type=F32)
    out = lambda o_ref, val: o_ref.__setitem__(Ellipsis, val.reshape(gb, rb, d))
    out(r_o, dot(mix(0), wr_ref))
    out(k_o, dot(mix(2), wk_ref))
    out(v_o, dot(mix(3), wv_ref))
    w_raw = w0_ref[...] + dot(jnp.tanh(dot(mix(1), w1_ref)), w2_ref)
    out(lw_o, -jnp.exp(-_softplus(-w_raw) - 0.5))
    out(a_o, _sigmoid(a0_ref[...] + dot(dot(mix(4), a1_ref), a2_ref)))
    out(g_o, dot(_sigmoid(dot(mix(5), g1_ref)), g2_ref))
    hl_o[...] = h[:, rb - 1:rb, :]


def rwkv_pre(x, shift_buf, shift, scale, nw, mu, w0, a0, ws, gb, rb):
    G, R, D = x.shape
    assert gb == 1 or rb == R
    xspec = pl.BlockSpec((gb, rb, D), lambda g, r: (g, r, 0))
    halo = pl.BlockSpec((gb, 8, D), lambda g, r: (g, jnp.maximum(r * (rb // 8) - 1, 0), 0))
    mspec = pl.BlockSpec((gb, 1, D), lambda g, r: (g, 0, 0))
    vec = pl.BlockSpec((1, D), lambda g, r: (0, 0))
    full = lambda w: pl.BlockSpec(w.shape, lambda g, r: (0, 0))
    mu8 = jnp.pad(mu, ((0, 8 - mu.shape[0]), (0, 0)))
    big = jax.ShapeDtypeStruct((G, R, D), F32)
    return pl.pallas_call(
        _rwkv_pre_kernel, name="rwkv_pre",
        grid=(G // gb, R // rb),
        in_specs=[xspec, halo, mspec, mspec, mspec, vec, full(mu8), vec, vec] + [full(w) for w in ws],
        out_specs=[xspec] * 6 + [mspec],
        out_shape=[big] * 6 + [jax.ShapeDtypeStruct((G, 1, D), F32)],
        scratch_shapes=[pltpu.VMEM((gb, rb + 8, D), F32)],
        compiler_params=_cparams(("parallel", "arbitrary")),
    )(x, x, shift_buf.reshape(G, 1, D), shift, scale, nw.reshape(1, D), mu8, w0.reshape(1, D),
      a0.reshape(1, D), *ws)


def _rwkv_scan_kernel(r_ref, lw_ref, k_ref, v_ref, a_ref, kk_ref, ka_ref, rk_ref, lnw_ref, lnb_ref, s0_ref,
                      y_ref, so_ref, s_sc):
    c = pl.program_id(2)
    C = r_ref.shape[1]
    N = RW_HEADDIM

    @pl.when(c == 0)
    def _():
        s_sc[...] = s0_ref[0]

    hg = s_sc.shape[0]
    rows = hg * C
    stack = lambda ref: jnp.concatenate([ref[0][:, h * N:(h + 1) * N] for h in range(hg)], axis=0)
    pstack = lambda ref: jnp.concatenate(
        [jnp.broadcast_to(ref[:, h * N:(h + 1) * N], (C, N)) for h in range(hg)], axis=0)
    r, lw, kraw, v, a = (stack(ref) for ref in (r_ref, lw_ref, k_ref, v_ref, a_ref))
    kkr = kraw * pstack(kk_ref)
    kk = kkr / jnp.maximum(jnp.sqrt(jnp.sum(kkr * kkr, axis=-1, keepdims=True)), 1e-12)
    k = kraw * (1.0 + (a - 1.0) * pstack(ka_ref))
    tpos = lax.broadcasted_iota(jnp.int32, (rows, N), 0) % C
    cum, sh = lw, 1
    while sh < C:
        cum = cum + jnp.where(tpos >= sh, pltpu.roll(cum, sh, 0), 0.0)
        sh *= 2
    e_in, e_ex, e_neg = jnp.exp(cum), jnp.exp(cum - lw), jnp.exp(-cum)
    ab, bb, kb, rb_ = -kk * e_ex, kk * a * e_neg, k * e_neg, r * e_in
    ri = lax.broadcasted_iota(jnp.int32, (rows, rows), 0)
    ci = lax.broadcasted_iota(jnp.int32, (rows, rows), 1)
    same = (ri // C) == (ci // C)
    strict, incl = jnp.logical_and(same, ci < ri), jnp.logical_and(same, ci <= ri)
    sab, sbb, skb, srb = _split(ab), _split(bb), _split(kb), _split(rb_)
    d1 = lambda x, y, dims=NN: lax.dot_general(x, y, (dims, ((), ())), preferred_element_type=F32)
    lab = jnp.where(strict, d1(sab[0], sbb[0], NT), 0.0)
    lak = jnp.where(strict, d1(sab[0], skb[0], NT), 0.0)
    mrb = jnp.where(incl, d1(srb[0], sbb[0], NT), 0.0)
    mrk = jnp.where(incl, d1(srb[0], skb[0], NT), 0.0)
    tinv = (ri == ci).astype(F32) + lab
    spw = _split(lab)
    for _ in range(int(math.log2(C)) - 1):
        spw = _split(_dot3s(spw, spw))
        tinv = tinv + _dot3s(_split(tinv), spw)
    hsl = lambda t, h: t[h * C:(h + 1) * C]
    xs = [_dot3(jnp.concatenate([hsl(ab, h), hsl(rb_, h)], axis=0), s_sc[h], NT) for h in range(hg)]
    ab_s = jnp.concatenate([x[:C] for x in xs], axis=0)
    rb_s = jnp.concatenate([x[C:] for x in xs], axis=0)
    u = _dot3(tinv, ab_s + _dot3(lak, v))
    y = rb_s + _dot3(jnp.concatenate([mrb, mrk], axis=1), jnp.concatenate([u, v], axis=0))
    for h in range(hg):
        upd = _dot3(jnp.concatenate([hsl(u, h), hsl(v, h)], axis=0),
                    jnp.concatenate([hsl(bb, h), hsl(kb, h)], axis=0), TN)
        s_sc[h] = (s_sc[h] + upd) * e_in[(h + 1) * C - 1:(h + 1) * C, :]
    mean = jnp.mean(y, axis=-1, keepdims=True)
    var = jnp.mean(jnp.square(y - mean), axis=-1, keepdims=True)
    yn = (y - mean) * lax.rsqrt(var + RW_LNX_EPS) * pstack(lnw_ref) + pstack(lnb_ref)
    out = yn + jnp.sum(r * k * pstack(rk_ref), axis=-1, keepdims=True) * v
    y_ref[0] = jnp.concatenate([hsl(out, h) for h in range(hg)], axis=-1)

    @pl.when(c == pl.num_programs(2) - 1)
    def _():
        so_ref[0] = s_sc[...]


def rwkv_scan(r, lw, k, v, a, k_k, k_a, r_k, lnx_w, lnx_b, s0, C, hg):
    B, L, D = r.shape
    N = RW_HEADDIM
    assert L % C == 0 and C & (C - 1) == 0 and RW_HEADS % hg == 0 and (hg * N) % LANE == 0
    tspec = pl.BlockSpec((1, C, hg * N), lambda b, p, c: (b, c, p))
    pspec = pl.BlockSpec((1, hg * N), lambda b, p, c: (0, p))
    sspec = pl.BlockSpec((1, hg, N, N), lambda b, p, c: (b, p, 0, 0))
    vec = lambda t: t.reshape(1, D)
    return pl.pallas_call(
        _rwkv_scan_kernel, name="rwkv_scan",
        grid=(B, RW_HEADS // hg, L // C),
        in_specs=[tspec] * 5 + [pspec] * 5 + [sspec],
        out_specs=[tspec, sspec],
        out_shape=[jax.ShapeDtypeStruct((B, L, D), F32), jax.ShapeDtypeStruct(s0.shape, F32)],
        scratch_shapes=[pltpu.VMEM((hg, N, N), F32)],
        compiler_params=_cparams(("parallel", "parallel", "arbitrary")),
    )(r, lw, k, v, a, vec(k_k), vec(k_a), vec(r_k), vec(lnx_w), vec(lnx_b), s0)


def rwkv_layer(x, shift, scale, gate, nw, shift_buf, wkv0, mu, w_r, w_k, w_v, w0, w1, w2, a0, a1, a2, g1, g2,
               k_k, k_a, r_k, lnx_w, lnx_b, w_o, gb, rb, C, hg):
    ws = [w.astype(BF16) for w in (w_r, w_k, w_v, w1, w2, a1, a2, g1, g2)]
    r, lw, k, v, a, g, hl = rwkv_pre(x, shift_buf, shift, scale, nw, mu, w0, a0, ws, gb, rb)
    y, s_fin = rwkv_scan(r, lw, k, v, a, k_k, k_a, r_k, lnx_w, lnx_b, wkv0, C, hg)
    return oproj(y, g, w_o.astype(BF16), x, gate, gb, rb), hl[:, 0], s_fin


def _conv_kernel(x_ref, halo_ref, buf_ref, w_ref, b_ref, o_ref, sc):
    rr = pl.program_id(1)
    gb, rb, cw = x_ref.shape
    sc[:, :8, :] = jnp.where(rr == 0, buf_ref[...], halo_ref[...])
    sc[:, 8:, :] = x_ref[...]
    y = b_ref[...]
    for t in range(SSD_CONV):
        y = y + sc[:, 8 - (SSD_CONV - 1) + t:8 - (SSD_CONV - 1) + t + rb, :] * w_ref[t:t + 1, :]
    o_ref[...] = y * _sigmoid(y)


def ssd_conv(zx, buf, w, b, gb, rb, cw=512):
    G, R, _ = zx.shape
    assert gb == 1 or rb == R
    off = SSD_D_INNER // cw
    buf8 = jnp.pad(buf, ((0, 0), (8 - (SSD_CONV - 1), 0), (0, 0)))
    w8 = jnp.pad(w, ((0, 8 - SSD_CONV), (0, 0)))
    return pl.pallas_call(
        _conv_kernel, name="ssd_conv",
        grid=(G // gb, R // rb, SSD_CONV_DIM // cw),
        in_specs=[pl.BlockSpec((gb, rb, cw), lambda g, r, j: (g, r, off + j)),
                  pl.BlockSpec((gb, 8, cw), lambda g, r, j: (g, jnp.maximum(r * (rb // 8) - 1, 0), off + j)),
                  pl.BlockSpec((gb, 8, cw), lambda g, r, j: (g, 0, j)),
                  pl.BlockSpec((8, cw), lambda g, r, j: (0, j)),
                  pl.BlockSpec((1, cw), lambda g, r, j: (0, j))],
        out_specs=pl.BlockSpec((gb, rb, cw), lambda g, r, j: (g, r, j)),
        out_shape=jax.ShapeDtypeStruct((G, R, SSD_CONV_DIM), F32),
        scratch_shapes=[pltpu.VMEM((gb, rb + 8, cw), F32)],
        compiler_params=_cparams(("parallel", "arbitrary", "arbitrary")),
    )(zx, zx, buf8, w8, b.reshape(1, SSD_CONV_DIM))


def _col2row(col, eye):
    return jnp.sum(jnp.where(eye, col, 0.0), axis=0, keepdims=True)


def _ssd_scan_kernel(x_ref, bm_ref, cm_ref, z_ref, dt_ref, dtb_ref, alog_ref, dsk_ref, nw_ref, s0_ref,
                     y_ref, so_ref, s_sc):
    c = pl.program_id(2)
    Q = x_ref.shape[1]
    P, N, R = SSD_HEADDIM, SSD_STATE, SSD_HPG

    @pl.when(c == 0)
    def _():
        s_sc[...] = s0_ref[0]

    ti = lax.broadcasted_iota(jnp.int32, (Q, Q), 0)
    si = lax.broadcasted_iota(jnp.int32, (Q, Q), 1)
    causal, eye = si <= ti, si == ti
    lo = lax.broadcasted_iota(jnp.int32, (Q, LANE), 1) < P
    dt = _softplus(dt_ref[0] + dtb_ref[...])
    cs = _dot(causal.astype(F32), dt * (-jnp.exp(alog_ref[...])), hi=True)
    x, bm, cm = x_ref[0], bm_ref[0], cm_ref[0]
    cb = _dot(cm, bm, NT)
    s_all = s_sc[...].reshape(R * P, N)
    ys = _dot(cm, s_all, NT)
    y_pairs, xw_pairs, e_last = [], [], []
    for j in range(R // 2):
        xp = x[:, j * LANE:(j + 1) * LANE]
        yh, grow, wend = [], [], []
        for r in (2 * j, 2 * j + 1):
            cs_col, dt_col = cs[:, r:r + 1], dt[:, r:r + 1]
            cs_last = cs[Q - 1:Q, r:r + 1]
            dec = jnp.exp(jnp.where(causal, cs_col - _col2row(cs_col, eye), -jnp.inf))
            yh.append(_dot(cb * dec * _col2row(dt_col, eye), xp))
            grow.append(jnp.exp(cs_col))
            wend.append(jnp.exp(cs_last - cs_col) * dt_col)
            e_last.append(jnp.exp(cs_last))
        yp = jnp.where(lo, yh[0], yh[1]) + ys[:, j * LANE:(j + 1) * LANE] * jnp.where(lo, grow[0], grow[1])
        y_pairs.append(yp + xp * dsk_ref[:, j * LANE:(j + 1) * LANE])
        xw_pairs.append(xp * jnp.where(lo, wend[0], wend[1]))
    upd = _dot(jnp.concatenate(xw_pairs, axis=-1), bm, TN)
    for r in range(R):
        s_sc[r] = s_sc[r] * e_last[r] + upd[r * P:(r + 1) * P, :]
    z = z_ref[0]
    yz = jnp.concatenate(y_pairs, axis=-1) * (z * _sigmoid(z))
    y_ref[0] = yz * lax.rsqrt(jnp.mean(yz * yz, axis=-1, keepdims=True) + NORM_EPS) * nw_ref[...]

    @pl.when(c == pl.num_programs(2) - 1)
    def _():
        so_ref[0] = s_sc[...]


def ssd_chunk_scan(xbc, zx, dt_raw, dt_bias, a_log, d_skip, norm_w, s0, Q):
    B, L, _ = xbc.shape
    G, R, P, N = SSD_GROUPS, SSD_HPG, SSD_HEADDIM, SSD_STATE
    gw = R * P
    assert L % Q == 0 and N == LANE
    pad_heads = lambda t: jnp.pad(t.reshape(G, R), ((0, 0), (0, LANE - R))).reshape(1, G * LANE)
    xs = pl.BlockSpec((1, Q, gw), lambda b, g, c: (b, c, g))
    bs = pl.BlockSpec((1, Q, N), lambda b, g, c: (b, c, SSD_D_INNER // N + g))
    cs = pl.BlockSpec((1, Q, N), lambda b, g, c: (b, c, SSD_D_INNER // N + G + g))
    ds = pl.BlockSpec((1, Q, LANE), lambda b, g, c: (b, c, g))
    hs = pl.BlockSpec((1, LANE), lambda b, g, c: (0, g))
    ws = pl.BlockSpec((1, gw), lambda b, g, c: (0, g))
    ss = pl.BlockSpec((1, R, P, N), lambda b, g, c: (b, g, 0, 0))
    return pl.pallas_call(
        _ssd_scan_kernel, name="ssd_scan",
        grid=(B, G, L // Q),
        in_specs=[xs, bs, cs, xs, ds, hs, hs, ws, ws, ss],
        out_specs=[xs, ss],
        out_shape=[jax.ShapeDtypeStruct((B, L, SSD_D_INNER), F32), jax.ShapeDtypeStruct(s0.shape, F32)],
        scratch_shapes=[pltpu.VMEM((R, P, N), F32)],
        compiler_params=_cparams(("parallel", "parallel", "arbitrary")),
    )(xbc, xbc, xbc, zx, dt_raw, pad_heads(dt_bias), pad_heads(a_log),
      jnp.repeat(d_skip, P).reshape(1, SSD_D_INNER), norm_w.reshape(1, SSD_D_INNER), s0)


def ssd_layer(x, shift, scale, gate, nw, conv_buf, ssm0, w_in, conv_w, conv_b, dt_bias, a_log, d_skip, norm_w,
              w_out, gb, rb, Q):
    G, R, D = x.shape
    nzx = SSD_D_INNER + SSD_CONV_DIM
    h = modnorm(x, shift, scale, nw, gb, rb, BF16)
    zx = mm(h, w_in[:, :nzx])
    w_dt = jnp.pad(w_in[:, nzx:].reshape(D, SSD_GROUPS, SSD_HPG), ((0, 0), (0, 0), (0, LANE - SSD_HPG)))
    dt_raw = mm(h, w_dt.reshape(D, SSD_GROUPS * LANE))
    xbc = ssd_conv(zx, conv_buf, conv_w, conv_b, gb, rb)
    y, s_fin = ssd_chunk_scan(xbc, zx, dt_raw, dt_bias, a_log, d_skip, norm_w, ssm0, Q)
    new_conv = zx[:, R - (SSD_CONV - 1):, SSD_D_INNER:]
    return oproj(y, None, w_out.astype(BF16), x, gate, gb, rb), new_conv, s_fin


def _mlstm_kernel(q_ref, k_ref, v_ref, o_ref, g_ref, bif_ref, nw_ref, c0_ref, n0_ref, m0_ref,
                  y_ref, co_ref, no_ref, mo_ref, c_sc, n_sc, m_sc):
    c = pl.program_id(2)
    Q = q_ref.shape[1]
    DK, DV = ML_QK_DIM, ML_V_DIM

    @pl.when(c == 0)
    def _():
        c_sc[...] = c0_ref[0]
        n_sc[...] = n0_ref[0, 0]
        m_sc[...] = m0_ref[0, 0]

    ti = lax.broadcasted_iota(jnp.int32, (Q, Q), 0)
    si = lax.broadcasted_iota(jnp.int32, (Q, Q), 1)
    causal, eye = si <= ti, si == ti
    gates = g_ref[0] + bif_ref[...]
    bcum = _dot(causal.astype(F32), -_softplus(-gates), hi=True)
    ys = []
    for hh in range(2):
        i_col, b_col = gates[:, hh:hh + 1], bcum[:, 2 + hh:3 + hh]
        m_prev = m_sc[:, hh:hh + 1]
        q = q_ref[0][:, hh * DK:(hh + 1) * DK] * DK ** -0.5
        k = k_ref[0][:, hh * DK:(hh + 1) * DK]
        v = v_ref[0][:, hh * DV:(hh + 1) * DV]
        cst, nst = c_sc[hh], n_sc[hh:hh + 1, :]
        dmat = jnp.where(causal, b_col - _col2row(b_col, eye) + _col2row(i_col, eye), -jnp.inf)
        m_inter = b_col + m_prev
        m_t = jnp.maximum(m_inter, jnp.max(dmat, axis=-1, keepdims=True))
        w_inter = jnp.exp(m_inter - m_t)
        qk = _dot(q, k, NT) * jnp.exp(dmat - m_t)
        num = _dot(qk, v) + w_inter * _dot(q, cst, NT)
        den = jnp.sum(qk, axis=-1, keepdims=True) + w_inter * jnp.sum(q * nst, axis=-1, keepdims=True)
        hout = num / jnp.maximum(jnp.abs(den), jnp.exp(-m_t))
        m_last, b_last = m_t[Q - 1:Q], b_col[Q - 1:Q]
        w_state = jnp.exp(b_last - b_col + i_col - m_last)
        decay = jnp.exp(b_last + m_prev - m_last)
        c_sc[hh] = decay * cst + _dot(v * w_state, k, TN)
        n_sc[hh:hh + 1, :] = decay * nst + jnp.sum(w_state * k, axis=0, keepdims=True)
        m_sc[:, hh:hh + 1] = m_last
        hn = hout * lax.rsqrt(jnp.mean(hout * hout, axis=-1, keepdims=True) + NORM_EPS)
        ys.append(_sigmoid(o_ref[0][:, hh * DV:(hh + 1) * DV]) * (hn * nw_ref[:, hh * DV:(hh + 1) * DV]))
    y_ref[0] = jnp.concatenate(ys, axis=-1)

    @pl.when(c == pl.num_programs(2) - 1)
    def _():
        co_ref[0] = c_sc[...]
        no_ref[0, 0] = n_sc[...]
        mo_ref[0, 0] = m_sc[...]


def mlstm_scan(proj, gates, b_if, norm_w, c0, n0, m0, Q):
    B, L, _ = proj.shape
    H, DK, DV = ML_HEADS, ML_QK_DIM, ML_V_DIM
    assert L % Q == 0 and 2 * DK == LANE
    npair = H // 2
    qs = pl.BlockSpec((1, Q, LANE), lambda b, j, c: (b, c, j))
    ks = pl.BlockSpec((1, Q, LANE), lambda b, j, c: (b, c, npair + j))
    vs = pl.BlockSpec((1, Q, 2 * DV), lambda b, j, c: (b, c, H * DK // DV + j))
    os_ = pl.BlockSpec((1, Q, 2 * DV), lambda b, j, c: (b, c, H * DK // DV + npair + j))
    ys = pl.BlockSpec((1, Q, 2 * DV), lambda b, j, c: (b, c, j))
    cs = pl.BlockSpec((1, 2, DV, DK), lambda b, j, c: (b, j, 0, 0))
    ns = pl.BlockSpec((1, 1, 2, DK), lambda b, j, c: (b, j, 0, 0))
    ms = pl.BlockSpec((1, 1, 1, 2), lambda b, j, c: (b, j, 0, 0))
    bi, bf = b_if[:H].reshape(npair, 2), b_if[H:].reshape(npair, 2)
    bias = jnp.pad(jnp.concatenate([bi, bf], axis=-1), ((0, 0), (0, LANE - 4))).reshape(1, npair * LANE)
    y, c, n, m = pl.pallas_call(
        _mlstm_kernel, name="mlstm",
        grid=(B, npair, L // Q),
        in_specs=[qs, ks, vs, os_, qs, pl.BlockSpec((1, LANE), lambda b, j, c: (0, j)),
                  pl.BlockSpec((1, 2 * DV), lambda b, j, c: (0, j)), cs, ns, ms],
        out_specs=[ys, cs, ns, ms],
        out_shape=[jax.ShapeDtypeStruct((B, L, H * DV), F32), jax.ShapeDtypeStruct((B, H, DV, DK), F32),
                   jax.ShapeDtypeStruct((B, npair, 2, DK), F32), jax.ShapeDtypeStruct((B, npair, 1, 2), F32)],
        scratch_shapes=[pltpu.VMEM((2, DV, DK), F32), pltpu.VMEM((2, DK), F32), pltpu.VMEM((1, 2), F32)],
        compiler_params=_cparams(("parallel", "parallel", "arbitrary")),
    )(proj, proj, proj, proj, gates, bias, norm_w.reshape(1, H * DV), c0,
      n0.reshape(B, npair, 2, DK), m0.reshape(B, npair, 1, 2))
    return y, c, n.reshape(B, H, DK), m.reshape(B, H)


def mlstm_layer(x, shift, scale, gate, nw, c0, n0, m0, w_in, b_if, norm_w, w_o, gb, rb, Q):
    D = x.shape[-1]
    H = ML_HEADS
    nmain = 2 * H * ML_QK_DIM + 2 * H * ML_V_DIM
    h = modnorm(x, shift, scale, nw, gb, rb, BF16)
    proj = mm(h, w_in[:, :nmain])
    wi, wf = w_in[:, nmain:nmain + H].reshape(D, H // 2, 2), w_in[:, nmain + H:].reshape(D, H // 2, 2)
    wg = jnp.pad(jnp.concatenate([wi, wf], axis=-1), ((0, 0), (0, 0), (0, LANE - 4)))
    gates = mm(h, wg.reshape(D, (H // 2) * LANE))
    y, c, n, m = mlstm_scan(proj, gates, b_if, norm_w, c0, n0, m0, Q)
    return oproj(y, None, w_o.astype(BF16), x, gate, gb, rb), c, n, m


def rmsnorm(x, w):
    xf = x.astype(jnp.float32)
    y = xf * lax.rsqrt(jnp.mean(xf * xf, axis=-1, keepdims=True) + NORM_EPS)
    return (y * w).astype(x.dtype)


def sublayer_in(x, g, shift, scale):
    return rmsnorm(x, g) * (1 + scale) + shift


def causal_dwconv(x, buf, w, b):
    L, K = x.shape[1], w.shape[0]
    xp = jnp.concatenate([buf.astype(x.dtype), x], axis=1)
    y = b + sum(xp[:, k:k + L] * w[k] for k in range(K))
    return y, xp[:, -(K - 1):]


def ssd_scan(x, dt, a, bm, cm, s0):
    B, L = x.shape[0], x.shape[1]
    Q = math.gcd(L, SSD_CHUNK)
    nc = L // Q
    chunks = lambda t: jnp.moveaxis(t.reshape(B, nc, Q, *t.shape[2:]), 1, 0)
    causal = jnp.tril(jnp.ones((Q, Q), bool))[None, :, :, None, None]

    def step(s, inp):
        xc, dtc, bc, cc = inp
        xf, bf, cf = xc.astype(jnp.float32), bc.astype(jnp.float32), cc.astype(jnp.float32)
        cs = jnp.cumsum(dtc * a, axis=1)
        decay = jnp.exp(jnp.where(causal, cs[:, :, None] - cs[:, None], -jnp.inf))
        cb = jnp.einsum('btgn,bsgn->btsg', cf, bf)
        wgt = cb[..., None] * decay * dtc[:, None]
        y = jnp.einsum('btsgr,bsgrp->btgrp', wgt, xf)
        y = y + jnp.einsum('btgn,bgrpn->btgrp', cf, s) * jnp.exp(cs)[..., None]
        w_end = jnp.exp(cs[:, -1:] - cs) * dtc
        s_new = s * jnp.exp(cs[:, -1])[..., None, None] + jnp.einsum('bsgr,bsgn,bsgrp->bgrpn', w_end, bf, xf)
        return s_new, y

    s_fin, y = lax.scan(step, s0, (chunks(x), chunks(dt), chunks(bm), chunks(cm)))
    y = jnp.moveaxis(y, 0, 1).reshape(x.shape)
    return y, s_fin


def ssd_mixer(h, conv_buf, ssm0, w_in, conv_w, conv_b, dt_bias, a_log, d_skip, norm_w, w_out):
    B, L, _ = h.shape
    G, R, P, N = SSD_GROUPS, SSD_HPG, SSD_HEADDIM, SSD_STATE
    zxbcdt = mm(h, w_in)
    z = zxbcdt[..., :SSD_D_INNER]
    xbc = zxbcdt[..., SSD_D_INNER:SSD_D_INNER + SSD_CONV_DIM]
    dt_raw = zxbcdt[..., SSD_D_INNER + SSD_CONV_DIM:]
    xbc, new_conv = causal_dwconv(xbc, conv_buf, conv_w, conv_b)
    xbc = jax.nn.silu(xbc)
    xs = xbc[..., :SSD_D_INNER].reshape(B, L, G, R, P)
    bm = xbc[..., SSD_D_INNER:SSD_D_INNER + G * N].reshape(B, L, G, N)
    cm = xbc[..., SSD_D_INNER + G * N:].reshape(B, L, G, N)
    dt = jax.nn.softplus(dt_raw.astype(jnp.float32) + dt_bias).reshape(B, L, G, R)
    a = -jnp.exp(a_log.astype(jnp.float32)).reshape(G, R)
    y, s_fin = ssd_scan(xs, dt, a, bm, cm, ssm0.astype(jnp.float32).reshape(B, G, R, P, N))
    y = y + xs.astype(jnp.float32) * d_skip.reshape(G, R, 1)
    yz = y.reshape(B, L, SSD_D_INNER) * jax.nn.silu(z.astype(jnp.float32))
    yz = rmsnorm(yz.reshape(B, L, G, -1), norm_w.reshape(G, -1)).reshape(B, L, SSD_D_INNER)
    return mm(yz.astype(h.dtype), w_out), new_conv, s_fin.reshape(B, SSD_HEADS, P, N)


def rwkv7_mixer(h, shift_buf, wkv0, mu, w_r, w_k, w_v, w0, w1, w2, a0, a1, a2, g1, g2,
                k_k, k_a, r_k, lnx_w, lnx_b, w_o):
    B, L, D = h.shape
    H, N = RW_HEADS, RW_HEADDIM
    prev = jnp.concatenate([shift_buf[:, None, :].astype(h.dtype), h[:, :-1]], axis=1)
    xx = prev - h
    xr, xw, xk = h + xx * mu[0], h + xx * mu[1], h + xx * mu[2]
    xv, xa, xg = h + xx * mu[3], h + xx * mu[4], h + xx * mu[5]
    r, k, v = mm(xr, w_r), mm(xk, w_k), mm(xv, w_v)
    w_raw = (w0 + mm(jnp.tanh(mm(xw, w1)), w2)).astype(jnp.float32)
    decay = jnp.exp(-jnp.exp(-jax.nn.softplus(-w_raw) - 0.5))
    a = jax.nn.sigmoid(a0 + mm(mm(xa, a1), a2))
    g = mm(jax.nn.sigmoid(mm(xg, g1)), g2)
    heads = lambda t: t.reshape(B, L, H, N).astype(jnp.float32)
    kk = heads(k * k_k)
    kk = kk / jnp.maximum(jnp.sqrt(jnp.sum(kk * kk, axis=-1, keepdims=True)), 1e-12)
    k = k * (1 + (a - 1) * k_a)
    rh, kh, vh, ah, wh = heads(r), heads(k), heads(v), heads(a), heads(decay)
    tm = lambda t: jnp.moveaxis(t, 1, 0)

    def step(S, inp):
        r_t, w_t, k_t, v_t, kk_t, a_t = inp
        sa = jnp.einsum('bhvk,bhk->bhv', S, -kk_t)
        S = S * w_t[:, :, None, :] + sa[..., None] * (kk_t * a_t)[:, :, None, :] + v_t[..., None] * k_t[:, :, None, :]
        return S, jnp.einsum('bhvk,bhk->bhv', S, r_t)

    S_fin, y = lax.scan(step, wkv0.astype(jnp.float32), (tm(rh), tm(wh), tm(kh), tm(vh), tm(kk), tm(ah)))
    y = jnp.moveaxis(y, 0, 1)
    mean = jnp.mean(y, axis=-1, keepdims=True)
    var = jnp.mean(jnp.square(y - mean), axis=-1, keepdims=True)
    y = ((y - mean) * lax.rsqrt(var + RW_LNX_EPS)).reshape(B, L, D) * lnx_w + lnx_b
    bonus = jnp.sum(rh * kh * r_k, axis=-1, keepdims=True) * vh
    y = (y + bonus.reshape(B, L, D)).astype(h.dtype)
    return mm(y * g, w_o), h[:, -1], S_fin


def rope_partial(x, pos):
    half = ROPE_DIM // 2
    inv = ROPE_THETA ** (-jnp.arange(half, dtype=jnp.float32) / half)
    ang = pos.astype(jnp.float32)[:, None] * inv
    cos, sin = jnp.cos(ang)[:, None, :], jnp.sin(ang)[:, None, :]
    x1 = x[..., :half].astype(jnp.float32)
    x2 = x[..., half:ROPE_DIM].astype(jnp.float32)
    rot = jnp.concatenate([x1 * cos - x2 * sin, x2 * cos + x1 * sin], axis=-1).astype(x.dtype)
    return jnp.concatenate([rot, x[..., ROPE_DIM:]], axis=-1)


def moba_qkv(h, pos, w_qkv, q_norm, k_norm):
    B, L, _ = h.shape
    qkv = mm(h, w_qkv).reshape(B, L, 3, MB_HEADS, MB_HEADDIM)
    q = rope_partial(rmsnorm(qkv[:, :, 0], q_norm), pos)
    k = rope_partial(rmsnorm(qkv[:, :, 1], k_norm), pos)
    return q, k, qkv[:, :, 2]


def moba_attend_one(qpos, q, k_all, v_all):
    Lq, Lk = q.shape[0], k_all.shape[0]
    H, Dh, K = MB_HEADS, MB_HEADDIM, MB_TOPK
    nb = max(-(-Lk // MB_BLOCK), K)
    pad = nb * MB_BLOCK - Lk
    blocks = lambda t: jnp.pad(t, ((0, pad), (0, 0), (0, 0))).reshape(nb, MB_BLOCK, H, Dh).transpose(2, 0, 1, 3)
    kb, vb = blocks(k_all), blocks(v_all)
    kmean = jnp.mean(kb.astype(jnp.float32), axis=2)
    cur = qpos // MB_BLOCK
    gate = jnp.einsum('qhd,hnd->hqn', q.astype(jnp.float32), kmean)
    fully_past = jnp.arange(nb)[None, :] < cur[:, None]
    gate = jnp.where(fully_past[None], gate, -jnp.inf)
    _, top = lax.top_k(gate, K)
    sel_ok = jnp.arange(K)[None, :] < cur[:, None]
    idx = jnp.concatenate([top, jnp.broadcast_to(cur[None, :, None], (H, Lq, 1))], axis=-1)
    qb = math.gcd(Lq, MB_QBLOCK)
    nq = Lq // qb
    hidx = jnp.arange(H)[:, None, None]
    scale = MB_HEADDIM ** -0.5

    def one_block(args):
        qs, ps, ids, ok = args
        kg, vg = kb[hidx, ids], vb[hidx, ids]
        s = jnp.einsum('qhd,hqsbd->hqsb', qs, kg).astype(jnp.float32) * scale
        kpos = ids[..., None] * MB_BLOCK + jnp.arange(MB_BLOCK)
        valid = jnp.concatenate([ok, jnp.ones((qb, 1), bool)], axis=-1)
        mask = valid[None, :, :, None] & (kpos <= ps[None, :, None, None])
        s = jnp.where(mask, s, -jnp.inf)
        p = jax.nn.softmax(s.reshape(H, qb, -1), axis=-1).reshape(s.shape)
        return jnp.einsum('hqsb,hqsbd->qhd', p.astype(vg.dtype), vg)

    out = lax.map(one_block, (q.reshape(nq, qb, H, Dh), qpos.reshape(nq, qb),
                              idx.reshape(H, nq, qb, K + 1).transpose(1, 0, 2, 3), sel_ok.reshape(nq, qb, K)))
    return out.reshape(Lq, H, Dh)


def moba_sample_one(qpos, ck, cv, args):
    q, k_new, v_new, pt = args
    k_all = jnp.concatenate([ck[pt].reshape(-1, MB_HEADS, MB_HEADDIM).astype(k_new.dtype), k_new], axis=0)
    v_all = jnp.concatenate([cv[pt].reshape(-1, MB_HEADS, MB_HEADDIM).astype(v_new.dtype), v_new], axis=0)
    return moba_attend_one(qpos, q, k_all, v_all)


def mlstm_chunked(q, k, v, i_pre, log_f, c0, n0, m0):
    B, L = q.shape[0], q.shape[1]
    Q = math.gcd(L, ML_CHUNK)
    nc = L // Q
    chunks = lambda t: jnp.moveaxis(t.reshape(B, nc, Q, *t.shape[2:]), 1, 0)
    causal = jnp.tril(jnp.ones((Q, Q), bool))[None, :, :, None]

    def step(carry, inp):
        C, n, m = carry
        qc, kc, vc, ic, fc = inp
        b = jnp.cumsum(fc, axis=1)
        dmat = jnp.where(causal, b[:, :, None] - b[:, None] + ic[:, None], -jnp.inf)
        m_inter = b + m[:, None]
        m_t = jnp.maximum(m_inter, jnp.max(dmat, axis=2))
        w_intra = jnp.exp(dmat - m_t[:, :, None])
        w_inter = jnp.exp(m_inter - m_t)
        qk = jnp.einsum('bthd,bshd->btsh', qc, kc) * w_intra
        num = jnp.einsum('btsh,bshv->bthv', qk, vc) + w_inter[..., None] * jnp.einsum('bhvd,bthd->bthv', C, qc)
        den = jnp.sum(qk, axis=2) + w_inter * jnp.einsum('bhd,bthd->bth', n, qc)
        hout = num / jnp.maximum(jnp.abs(den), jnp.exp(-m_t))[..., None]
        m_last = m_t[:, -1]
        w_state = jnp.exp(b[:, -1:] - b + ic - m_last[:, None])
        decay = jnp.exp(b[:, -1] + m - m_last)
        C = decay[..., None, None] * C + jnp.einsum('bsh,bshv,bshd->bhvd', w_state, vc, kc)
        n = decay[..., None] * n + jnp.einsum('bsh,bshd->bhd', w_state, kc)
        return (C, n, m_last), hout

    carry0 = (c0.astype(jnp.float32), n0.astype(jnp.float32), m0.astype(jnp.float32))
    (C, n, m), hout = lax.scan(step, carry0, (chunks(q), chunks(k), chunks(v), chunks(i_pre), chunks(log_f)))
    return jnp.moveaxis(hout, 0, 1).reshape(v.shape), C, n, m


def mlstm_mixer(h, c0, n0, m0, w_in, b_if, norm_w, w_o):
    B, L, _ = h.shape
    H, DK, DV = ML_HEADS, ML_QK_DIM, ML_V_DIM
    proj = mm(h, w_in)
    o1, o2, o3, o4 = H * DK, 2 * H * DK, 2 * H * DK + H * DV, 2 * H * DK + 2 * H * DV
    q = proj[..., :o1].reshape(B, L, H, DK).astype(jnp.float32) * DK ** -0.5
    k = proj[..., o1:o2].reshape(B, L, H, DK).astype(jnp.float32)
    v = proj[..., o2:o3].reshape(B, L, H, DV).astype(jnp.float32)
    o = jax.nn.sigmoid(proj[..., o3:o4])
    gates = proj[..., o4:].astype(jnp.float32) + b_if
    i_pre, log_f = gates[..., :H], jax.nn.log_sigmoid(gates[..., H:])
    hh, C, n, m = mlstm_chunked(q, k, v, i_pre, log_f, c0, n0, m0)
    hh = rmsnorm(hh, norm_w.reshape(H, DV)).reshape(B, L, H * DV)
    return mm(o * hh.astype(h.dtype), w_o), C, n, m


def kernel(x_prompt, x_sample, state_ssd_conv, state_ssd, state_rwkv_shift, state_rwkv, cache_k, cache_v, state_mlstm_c, state_mlstm_n, state_mlstm_m, page_table, c_prompt, c_sample, ada_w, ada_b, norm_w, ffn_w_in, ffn_w_out, ssd_w_in, ssd_conv_w, ssd_conv_b, ssd_dt_bias, ssd_a_log, ssd_d, ssd_norm_w, ssd_w_out, rw_mu, rw_w_r, rw_w_k, rw_w_v, rw_w0, rw_w1, rw_w2, rw_a0, rw_a1, rw_a2, rw_g1, rw_g2, rw_k_k, rw_k_a, rw_r_k, rw_lnx_w, rw_lnx_b, rw_w_o, mb_w_qkv, mb_q_norm, mb_k_norm, mb_w_o, ml_w_in, ml_b_if, ml_norm_w, ml_w_o):
    B, S = x_prompt.shape[0], x_prompt.shape[1]
    Bs, Ls = x_sample.shape[0], x_sample.shape[1]
    D = D_MODEL
    pos_p = jnp.arange(S, dtype=jnp.int32)
    pos_s = PAST_LEN + jnp.arange(Ls, dtype=jnp.int32)
    xp, xs = x_prompt, x_sample
    outs = {}
    c_all = jnp.concatenate([c_prompt, c_sample], axis=0)
    sc_all = jax.nn.silu(c_all)
    for i in range(DEPTH):
        kind, j = i % N_MIXERS, i // N_MIXERS
        mod = (mm(sc_all, ada_w[i]) + ada_b[i]).reshape(B + Bs, N_MOD, 1, D)
        mp, ms = mod[:B], mod[B:]
        wi0, wo0 = ffn_w_in[i, 0].astype(BF16), ffn_w_out[i, 0].astype(BF16)
        wi1, wo1 = ffn_w_in[i, 1].astype(BF16), ffn_w_out[i, 1].astype(BF16)
        xp = ffn(xp, mp[:, 0], mp[:, 1], mp[:, 2], norm_w[i, 0], wi0, wo0, 1, 512)
        xs = ffn(xs, ms[:, 0], ms[:, 1], ms[:, 2], norm_w[i, 0], wi0, wo0, 64, Ls)
        m1p, m1s = (mp[:, 3], mp[:, 4], mp[:, 5], norm_w[i, 1]), (ms[:, 3], ms[:, 4], ms[:, 5], norm_w[i, 1])
        if kind == 0:
            w = (ssd_w_in[j], ssd_conv_w[j], ssd_conv_b[j], ssd_dt_bias[j], ssd_a_log[j], ssd_d[j], ssd_norm_w[j], ssd_w_out[j])
            xp, cp, sp = ssd_layer(xp, *m1p, jnp.zeros((B, SSD_CONV - 1, SSD_CONV_DIM), F32),
                                   jnp.zeros((B, SSD_HEADS, SSD_HEADDIM, SSD_STATE), F32), *w, 1, 512, SSD_CHUNK)
            xs, cs_, ss = ssd_layer(xs, *m1s, state_ssd_conv[j], state_ssd[j], *w, 64, Ls, Ls)
            outs[2], outs[3], outs[4], outs[5] = cp[None], cs_[None], sp[None], ss[None]
        elif kind == 1:
            w = (rw_mu[j], rw_w_r[j], rw_w_k[j], rw_w_v[j], rw_w0[j], rw_w1[j], rw_w2[j], rw_a0[j], rw_a1[j],
                 rw_a2[j], rw_g1[j], rw_g2[j], rw_k_k[j], rw_k_a[j], rw_r_k[j], rw_lnx_w[j], rw_lnx_b[j], rw_w_o[j])
            xp, shp, sp = rwkv_layer(xp, *m1p, jnp.zeros((B, D), F32),
                                     jnp.zeros((B, RW_HEADS, RW_HEADDIM, RW_HEADDIM), F32), *w, 1, 256, 64, 4)
            xs, shs, ss = rwkv_layer(xs, *m1s, state_rwkv_shift[j], state_rwkv[j], *w, 32, Ls, Ls, RW_HEADS)
            outs[6], outs[7], outs[8], outs[9] = shp[None], shs[None], sp[None], ss[None]
        elif kind == 2:
            hp = modnorm(xp, mp[:, 3], mp[:, 4], norm_w[i, 1], 1, 512, BF16)
            hs = modnorm(xs, ms[:, 3], ms[:, 4], norm_w[i, 1], 64, Ls, BF16)
            qkv_p, qkv_s = mm(hp, mb_w_qkv[j]), mm(hs, mb_w_qkv[j])
            qp, kp = qkpost(qkv_p, pos_p, mb_q_norm[j], mb_k_norm[j], 1, 512)
            qs, ks_ = qkpost(qkv_s, pos_s, mb_q_norm[j], mb_k_norm[j], 16, Ls)
            vp, vs = qkv_p[..., 2 * D:], qkv_s[..., 2 * D:]
            op = moba_prompt_attend_t(qp, kp, vp)
            nl, npool, page = cache_k.shape[0], cache_k.shape[1], cache_k.shape[2]
            os_ = moba_sample_attend(qs, ks_, vs, cache_k.reshape(nl * npool, page, MB_HEADS, MB_HEADDIM),
                                     cache_v.reshape(nl * npool, page, MB_HEADS, MB_HEADDIM),
                                     page_table + j * npool)
            w_o = mb_w_o[j].astype(BF16)
            xp = oproj(op, None, w_o, xp, mp[:, 5], 1, 512)
            xs = oproj(os_, None, w_o, xs, ms[:, 5], 64, Ls)
            hd = (MB_HEADS, MB_HEADDIM)
            outs[10], outs[11] = kp.reshape(1, B, S, *hd), ks_.reshape(1, Bs, Ls, *hd)
            outs[12], outs[13] = vp.reshape(1, B, S, *hd), vs.reshape(1, Bs, Ls, *hd)
        else:
            w = (ml_w_in[j], ml_b_if[j], ml_norm_w[j], ml_w_o[j])
            xp, cp, np_, mp_ = mlstm_layer(xp, *m1p, jnp.zeros((B, ML_HEADS, ML_V_DIM, ML_QK_DIM), F32),
                                           jnp.zeros((B, ML_HEADS, ML_QK_DIM), F32), jnp.zeros((B, ML_HEADS), F32),
                                           *w, 1, 512, 128)
            xs, cs_, ns_, ms_ = mlstm_layer(xs, *m1s, state_mlstm_c[j], state_mlstm_n[j], state_mlstm_m[j],
                                            *w, 64, Ls, Ls)
            outs[14], outs[15], outs[16], outs[17], outs[18], outs[19] = cp[None], cs_[None], np_[None], ns_[None], mp_[None], ms_[None]
        xp = ffn(xp, mp[:, 6], mp[:, 7], mp[:, 8], norm_w[i, 2], wi1, wo1, 1, 512)
        xs = ffn(xs, ms[:, 6], ms[:, 7], ms[:, 8], norm_w[i, 2], wi1, wo1, 64, Ls)
    outs[0], outs[1] = xp, xs
    return tuple(outs[i] for i in range(20))
```

```python
import functools
import math

import jax
import jax.numpy as jnp
from jax import lax
from jax.experimental import pallas as pl
from jax.experimental.pallas import tpu as pltpu

D_MODEL = 1024
DEPTH = 4
PAST_LEN = 2048
N_MIXERS = 4
NORM_EPS = 1e-6
N_MOD = 9
D_FF = 2816

SSD_D_INNER = 2 * D_MODEL
SSD_HEADDIM = 64
SSD_HEADS = SSD_D_INNER // SSD_HEADDIM
SSD_GROUPS = 4
SSD_HPG = SSD_HEADS // SSD_GROUPS
SSD_STATE = 128
SSD_CONV = 4
SSD_CHUNK = 128
SSD_CONV_DIM = SSD_D_INNER + 2 * SSD_GROUPS * SSD_STATE

RW_HEADDIM = 64
RW_HEADS = D_MODEL // RW_HEADDIM
RW_LNX_EPS = 64e-5

MB_HEADS = 16
MB_HEADDIM = D_MODEL // MB_HEADS
MB_BLOCK = 256
MB_TOPK = 3
MB_QBLOCK = 16
ROPE_DIM = MB_HEADDIM // 4
ROPE_THETA = 500000.0

ML_HEADS = 8
ML_QK_DIM = D_MODEL // 2 // ML_HEADS
ML_V_DIM = D_MODEL // ML_HEADS
ML_CHUNK = 64

LANE = 128
VMEM_LIMIT = 48 * 1024 * 1024

BF16 = jnp.bfloat16
F32 = jnp.float32


def _cparams(sem):
    return pltpu.CompilerParams(dimension_semantics=sem, vmem_limit_bytes=VMEM_LIMIT)


def _mm_kernel(a_ref, w_ref, o_ref):
    o_ref[...] = jnp.dot(a_ref[...].astype(BF16), w_ref[...], preferred_element_type=F32)


def _pick(n, cands):
    for c in cands:
        if n % c == 0:
            return c
    raise ValueError(n)


def mm(a, w):
    lead = a.shape[:-1]
    K = a.shape[-1]
    N = w.shape[-1]
    a2 = a.reshape(-1, K)
    M = a2.shape[0]
    Mp = -(-M // 8) * 8
    if Mp != M:
        a2 = jnp.pad(a2, ((0, Mp - M), (0, 0)))
    Np = -(-N // LANE) * LANE
    wb = w.astype(BF16)
    if Np != N:
        wb = jnp.pad(wb, ((0, 0), (0, Np - N)))
    tm = _pick(Mp, (512, 256, 128, 64, 32, 16, 8))
    tn = _pick(Np, (1024, 512, 256, 128))
    out = pl.pallas_call(
        _mm_kernel, name="mm",
        grid=(Mp // tm, Np // tn),
        in_specs=[pl.BlockSpec((tm, K), lambda i, j: (i, 0)),
                  pl.BlockSpec((K, tn), lambda i, j: (0, j))],
        out_specs=pl.BlockSpec((tm, tn), lambda i, j: (i, j)),
        out_shape=jax.ShapeDtypeStruct((Mp, Np), F32),
        compiler_params=_cparams(("parallel", "arbitrary")),
    )(a2, wb)
    return out[:M, :N].reshape(*lead, N)


def _ffn_kernel(x_ref, sh_ref, sc_ref, gt_ref, nw_ref, wg_ref, wu_ref, wo_ref, o_ref, h_sc, acc_sc):
    k = pl.program_id(2)
    gb, rb, d = x_ref.shape

    @pl.when(k == 0)
    def _():
        x = x_ref[...]
        y = x * lax.rsqrt(jnp.mean(x * x, axis=-1, keepdims=True) + NORM_EPS) * nw_ref[...]
        h = y * (1.0 + sc_ref[...]) + sh_ref[...]
        h_sc[...] = h.reshape(gb * rb, d).astype(BF16)

    h = h_sc[...]
    g = jnp.dot(h, wg_ref[...], preferred_element_type=F32)
    u = jnp.dot(h, wu_ref[...], preferred_element_type=F32)
    a = (g * jax.nn.sigmoid(g) * u).astype(BF16)
    part = jnp.dot(a, wo_ref[...], preferred_element_type=F32)

    @pl.when(k == 0)
    def _():
        acc_sc[...] = part

    @pl.when(k > 0)
    def _():
        acc_sc[...] += part

    @pl.when(k == pl.num_programs(2) - 1)
    def _():
        o_ref[...] = x_ref[...] + 0.5 * gt_ref[...] * acc_sc[...].reshape(gb, rb, d)


def ffn(x, shift, scale, gate, nw, w_in_b, w_out_b, gb, rb):
    G, R, D = x.shape
    tf = D_FF // 2
    nk = D_FF // tf
    xspec = pl.BlockSpec((gb, rb, D), lambda g, r, k: (g, r, 0))
    mspec = pl.BlockSpec((gb, 1, D), lambda g, r, k: (g, 0, 0))
    return pl.pallas_call(
        _ffn_kernel, name="ffn",
        grid=(G // gb, R // rb, nk),
        in_specs=[xspec, mspec, mspec, mspec,
                  pl.BlockSpec((1, 1, D), lambda g, r, k: (0, 0, 0)),
                  pl.BlockSpec((D, tf), lambda g, r, k: (0, k)),
                  pl.BlockSpec((D, tf), lambda g, r, k: (0, k + nk)),
                  pl.BlockSpec((tf, D), lambda g, r, k: (k, 0))],
        out_specs=xspec,
        out_shape=jax.ShapeDtypeStruct(x.shape, F32),
        scratch_shapes=[pltpu.VMEM((gb * rb, D), BF16), pltpu.VMEM((gb * rb, D), F32)],
        compiler_params=_cparams(("parallel", "parallel", "arbitrary")),
    )(x, shift, scale, gate, nw.reshape(1, 1, D), w_in_b, w_in_b, w_out_b)


HI = lax.Precision.HIGHEST
NN = ((1,), (0,))
NT = ((1,), (1,))
TN = ((0,), (0,))
NEG = -1e30


def _dot(a, b, dims=NN, hi=False):
    if hi:
        return lax.dot_general(a, b, (dims, ((), ())), precision=HI, preferred_element_type=F32)
    return lax.dot_general(a.astype(BF16), b.astype(BF16), (dims, ((), ())), preferred_element_type=F32)


def _split(a):
    hi = a.astype(BF16)
    return hi, (a - hi.astype(F32)).astype(BF16)


def _dot3s(sa, sb, dims=NN):
    d = lambda x, y: lax.dot_general(x, y, (dims, ((), ())), preferred_element_type=F32)
    return d(sa[0], sb[0]) + d(sa[0], sb[1]) + d(sa[1], sb[0])


def _dot3(a, b, dims=NN):
    return _dot3s(_split(a), _split(b), dims)


def _modnorm_kernel(x_ref, sh_ref, sc_ref, nw_ref, o_ref):
    x = x_ref[...]
    y = x * lax.rsqrt(jnp.mean(x * x, axis=-1, keepdims=True) + NORM_EPS) * nw_ref[...]
    o_ref[...] = (y * (1.0 + sc_ref[...]) + sh_ref[...]).astype(o_ref.dtype)


def modnorm(x, shift, scale, nw, gb, rb, dtype):
    G, R, D = x.shape
    xspec = pl.BlockSpec((gb, rb, D), lambda g, r: (g, r, 0))
    mspec = pl.BlockSpec((gb, 1, D), lambda g, r: (g, 0, 0))
    return pl.pallas_call(
        _modnorm_kernel, name="modnorm",
        grid=(G // gb, R // rb),
        in_specs=[xspec, mspec, mspec, pl.BlockSpec((1, 1, D), lambda g, r: (0, 0, 0))],
        out_specs=xspec,
        out_shape=jax.ShapeDtypeStruct(x.shape, dtype),
        compiler_params=_cparams(("parallel", "parallel")),
    )(x, shift, scale, nw.reshape(1, 1, D))


def _oproj_kernel(*refs, has_mul):
    if has_mul:
        y_ref, mul_ref, w_ref, x_ref, gt_ref, o_ref = refs
    else:
        y_ref, w_ref, x_ref, gt_ref, o_ref = refs
    gb, rb, kd = y_ref.shape
    y = y_ref[...]
    if has_mul:
        y = y * mul_ref[...]
    acc = jnp.dot(y.reshape(gb * rb, kd).astype(BF16), w_ref[...], preferred_element_type=F32)
    o_ref[...] = x_ref[...] + gt_ref[...] * acc.reshape(gb, rb, acc.shape[-1])


def oproj(y, mul, w_b, x, gate, gb, rb):
    G, R, Kd = y.shape
    D = x.shape[-1]
    yspec = pl.BlockSpec((gb, rb, Kd), lambda g, r: (g, r, 0))
    xspec = pl.BlockSpec((gb, rb, D), lambda g, r: (g, r, 0))
    ins = [y] + ([mul] if mul is not None else []) + [w_b, x, gate]
    specs = [yspec] + ([yspec] if mul is not None else []) + [
        pl.BlockSpec((Kd, D), lambda g, r: (0, 0)), xspec,
        pl.BlockSpec((gb, 1, D), lambda g, r: (g, 0, 0))]
    return pl.pallas_call(
        functools.partial(_oproj_kernel, has_mul=mul is not None), name="oproj",
        grid=(G // gb, R // rb),
        in_specs=specs,
        out_specs=xspec,
        out_shape=jax.ShapeDtypeStruct(x.shape, F32),
        compiler_params=_cparams(("parallel", "parallel")),
    )(*ins)


def _qkpost_kernel(q_ref, k_ref, cos_ref, sin_ref, qw_ref, kw_ref, qo_ref, ko_ref):
    gb, rb, w = q_ref.shape
    n = gb * rb
    lane = lax.broadcasted_iota(jnp.int32, (n, w), 1)
    lo = lane < MB_HEADDIM
    first = (lane % MB_HEADDIM) < (ROPE_DIM // 2)
    cos = jnp.broadcast_to(cos_ref[...][None], (gb, rb, w)).reshape(n, w)
    sin = jnp.broadcast_to(sin_ref[...][None], (gb, rb, w)).reshape(n, w)

    def norm_rope(x, wt):
        x2 = x * x
        s_lo = jnp.sum(jnp.where(lo, x2, 0.0), axis=-1, keepdims=True)
        s_hi = jnp.sum(jnp.where(lo, 0.0, x2), axis=-1, keepdims=True)
        ms = jnp.where(lo, s_lo, s_hi) * (1.0 / MB_HEADDIM)
        y = x * lax.rsqrt(ms + NORM_EPS) * wt
        up = pltpu.roll(y, w - ROPE_DIM // 2, 1)
        dn = pltpu.roll(y, ROPE_DIM // 2, 1)
        return y * cos + jnp.where(first, up, dn) * sin

    qo_ref[...] = norm_rope(q_ref[...].reshape(n, w), qw_ref[...]).reshape(gb, rb, w)
    ko_ref[...] = norm_rope(k_ref[...].reshape(n, w), kw_ref[...]).reshape(gb, rb, w)


def _rope_tables(pos):
    half = ROPE_DIM // 2
    inv = ROPE_THETA ** (-jnp.arange(half, dtype=F32) / half)
    ang = pos.astype(F32)[:, None] * inv
    c, s = jnp.cos(ang), jnp.sin(ang)
    ones = jnp.ones((pos.shape[0], MB_HEADDIM - ROPE_DIM), F32)
    ch = jnp.concatenate([c, c, ones], axis=-1)
    sh = jnp.concatenate([-s, s, 0.0 * ones], axis=-1)
    return jnp.concatenate([ch, ch], axis=-1), jnp.concatenate([sh, sh], axis=-1)


def qkpost(qkv, pos, q_norm, k_norm, gb, rb):
    G, R, _ = qkv.shape
    D = D_MODEL
    npair = D // LANE
    cos, sin = _rope_tables(pos)
    tspec = pl.BlockSpec((rb, LANE), lambda g, r, p: (r, 0))
    wspec = pl.BlockSpec((1, LANE), lambda g, r, p: (0, 0))
    ospec = pl.BlockSpec((gb, rb, LANE), lambda g, r, p: (g, r, p))
    return pl.pallas_call(
        _qkpost_kernel, name="qkpost",
        grid=(G // gb, R // rb, npair),
        in_specs=[pl.BlockSpec((gb, rb, LANE), lambda g, r, p: (g, r, p)),
                  pl.BlockSpec((gb, rb, LANE), lambda g, r, p: (g, r, p + npair)),
                  tspec, tspec, wspec, wspec],
        out_specs=[ospec, ospec],
        out_shape=[jax.ShapeDtypeStruct((G, R, D), F32)] * 2,
        compiler_params=_cparams(("parallel", "parallel", "parallel")),
    )(qkv, qkv, cos, sin, jnp.tile(q_norm, 2).reshape(1, LANE), jnp.tile(k_norm, 2).reshape(1, LANE))


def _top3_mask(g, idx, nb):
    sel = jnp.zeros(g.shape, jnp.bool_)
    for _ in range(MB_TOPK):
        mx = jnp.max(g, axis=-1, keepdims=True)
        first = jnp.min(jnp.where(g == mx, idx, nb), axis=-1, keepdims=True)
        pick = idx == first
        sel = jnp.logical_or(sel, pick)
        g = jnp.where(pick, -jnp.inf, g)
    return sel


def _moba_prompt_kernel(q_ref, k_ref, v_ref, o_ref, kmean_sc, selx_sc):
    c = pl.program_id(2)
    T = MB_BLOCK
    nb = k_ref.shape[1] // T
    scale = MB_HEADDIM ** -0.5

    @pl.when(c == 0)
    def _():
        for n in range(nb):
            kmean_sc[n:n + 1, :] = jnp.mean(k_ref[0, n * T:(n + 1) * T, :], axis=0, keepdims=True)

    q = q_ref[0]
    lo = lax.broadcasted_iota(jnp.int32, (T, LANE), 1) < MB_HEADDIM
    qh = (jnp.where(lo, q, 0.0), jnp.where(lo, 0.0, q))
    blk = lax.broadcasted_iota(jnp.int32, (T, nb), 1)
    for hh in range(2):
        g = _dot(qh[hh], kmean_sc[...], NT, hi=True)
        g = jnp.where(blk < c, g, -jnp.inf)
        sel = jnp.logical_and(_top3_mask(g, blk, nb), blk < c).astype(F32)
        for n in range(nb):
            selx_sc[hh, n] = jnp.broadcast_to(sel[:, n:n + 1], (T, LANE))
    qb = tuple((x * scale).astype(BF16) for x in qh)

    def blocks(n):
        st = pl.multiple_of(n * T, T)
        return k_ref[0, pl.ds(st, T), :].astype(BF16), v_ref[0, pl.ds(st, T), :].astype(BF16)

    kc, vc = blocks(c)
    causal = lax.broadcasted_iota(jnp.int32, (T, T), 0) >= lax.broadcasted_iota(jnp.int32, (T, T), 1)
    init = []
    for hh in range(2):
        s = jnp.where(causal, _dot(qb[hh], kc, NT), NEG)
        m = jnp.max(s, axis=-1, keepdims=True)
        p = jnp.exp(s - m)
        init += [m, jnp.sum(p, axis=-1, keepdims=True), _dot(p, vc)]

    def body(n, carry):
        kn, vn = blocks(n)
        out = []
        for hh in range(2):
            m, l, acc = carry[3 * hh:3 * hh + 3]
            keep = selx_sc[hh, n] > 0.5
            s = _dot(qb[hh], kn, NT)
            s0 = jnp.where(keep, s[:, :LANE], NEG)
            s1 = jnp.where(keep, s[:, LANE:], NEG)
            m_new = jnp.maximum(m, jnp.max(jnp.maximum(s0, s1), axis=-1, keepdims=True))
            alpha = jnp.exp(m - m_new)
            p0 = jnp.exp(s0 - m_new)
            p1 = jnp.exp(s1 - m_new)
            l = alpha * l + jnp.sum(p0 + p1, axis=-1, keepdims=True)
            acc = alpha * acc + _dot(p0, vn[:LANE]) + _dot(p1, vn[LANE:])
            out += [m_new, l, acc]
        return tuple(out)

    res = lax.fori_loop(0, c, body, tuple(init))
    o_ref[0] = jnp.where(lo, res[2] / res[1], res[5] / res[4])


def moba_prompt_attend(q, k, v):
    B, L, D = q.shape
    nb = L // MB_BLOCK
    kvspec = pl.BlockSpec((1, L, LANE), lambda b, p, c: (b, 0, p))
    qspec = pl.BlockSpec((1, MB_BLOCK, LANE), lambda b, p, c: (b, c, p))
    return pl.pallas_call(
        _moba_prompt_kernel, name="moba_prompt",
        grid=(B, D // LANE, nb),
        in_specs=[qspec, kvspec, kvspec],
        out_specs=qspec,
        out_shape=jax.ShapeDtypeStruct((B, L, D), F32),
        scratch_shapes=[pltpu.VMEM((nb, LANE), F32), pltpu.VMEM((2, nb, MB_BLOCK, LANE), F32)],
        compiler_params=_cparams(("parallel", "parallel", "arbitrary")),
    )(q, k, v)


def _top3_mask0(g, idx, nb):
    sel = jnp.zeros(g.shape, jnp.bool_)
    for _ in range(MB_TOPK):
        mx = jnp.max(g, axis=0, keepdims=True)
        first = jnp.min(jnp.where(g == mx, idx, nb), axis=0, keepdims=True)
        pick = idx == first
        sel = jnp.logical_or(sel, pick)
        g = jnp.where(pick, -jnp.inf, g)
    return sel


def _moba_prompt_t_kernel(q_ref, k_ref, v_ref, o_ref, kmean_sc, kb_sc, vt_sc, selt_sc):
    c = pl.program_id(2)
    T = MB_BLOCK
    P = MB_HEADDIM
    nb = k_ref.shape[1] // T
    scale = MB_HEADDIM ** -0.5

    @pl.when(c == 0)
    def _():
        for n in range(nb):
            kn = k_ref[0, n * T:(n + 1) * T, :]
            kmean_sc[n:n + 1, :] = jnp.mean(kn, axis=0, keepdims=True)
            kb_sc[n] = kn.astype(BF16)
            vt_sc[n] = v_ref[0, n * T:(n + 1) * T, :].T.astype(BF16)

    q = q_ref[0]
    lo = lax.broadcasted_iota(jnp.int32, (T, LANE), 1) < P
    qh = (jnp.where(lo, q, 0.0), jnp.where(lo, 0.0, q))
    blk = lax.broadcasted_iota(jnp.int32, (nb, T), 0)
    for hh in range(2):
        g = _dot(kmean_sc[...], qh[hh], NT, hi=True)
        g = jnp.where(blk < c, g, -jnp.inf)
        selt_sc[hh] = jnp.logical_and(_top3_mask0(g, blk, nb), blk < c).astype(F32)
    qb = tuple((x * scale).astype(BF16) for x in qh)
    causal = lax.broadcasted_iota(jnp.int32, (T, T), 0) <= lax.broadcasted_iota(jnp.int32, (T, T), 1)
    kc, vc = kb_sc[c], vt_sc[c]
    init = []
    for hh in range(2):
        s = jnp.where(causal, _dot(kc, qb[hh], NT), NEG)
        m = jnp.max(s, axis=0, keepdims=True)
        p = jnp.exp(s - m)
        init += [m, jnp.sum(p, axis=0, keepdims=True), _dot(vc[hh * P:(hh + 1) * P, :], p)]

    def scores(i):
        return tuple(_dot(kb_sc[jnp.minimum(2 * i + d, nb - 1)], qb[hh], NT) for hh in range(2) for d in range(2))

    def body(i, carry):
        state, s_cur = carry
        s_next = scores(i + 1)
        ns = (2 * i, 2 * i + 1)
        out = []
        for hh in range(2):
            m, l, acc = state[3 * hh:3 * hh + 3]
            ss = [jnp.where(selt_sc[hh, pl.ds(n, 1), :] > 0.5, s_cur[2 * hh + d], NEG) for d, n in enumerate(ns)]
            m_new = jnp.maximum(m, jnp.max(jnp.maximum(ss[0], ss[1]), axis=0, keepdims=True))
            alpha = jnp.exp(m - m_new)
            ps = [jnp.exp(s - m_new) for s in ss]
            l = alpha * l + jnp.sum(ps[0] + ps[1], axis=0, keepdims=True)
            acc = alpha * acc + sum(_dot(vt_sc[n, hh * P:(hh + 1) * P, :], p) for n, p in zip(ns, ps))
            out += [m_new, l, acc]
        return tuple(out), s_next

    res, _ = lax.fori_loop(0, (c + 1) // 2, body, (tuple(init), scores(0)))
    o_ref[0] = jnp.concatenate([res[2] / res[1], res[5] / res[4]], axis=0).T


def moba_prompt_attend_t(q, k, v):
    B, L, D = q.shape
    nb = L // MB_BLOCK
    kvspec = pl.BlockSpec((1, L, LANE), lambda b, p, c: (b, 0, p))
    qspec = pl.BlockSpec((1, MB_BLOCK, LANE), lambda b, p, c: (b, c, p))
    return pl.pallas_call(
        _moba_prompt_t_kernel, name="moba_prompt",
        grid=(B, D // LANE, nb),
        in_specs=[qspec, kvspec, kvspec],
        out_specs=qspec,
        out_shape=jax.ShapeDtypeStruct((B, L, D), F32),
        scratch_shapes=[pltpu.VMEM((nb, LANE), F32), pltpu.VMEM((nb, MB_BLOCK, LANE), BF16),
                        pltpu.VMEM((nb, LANE, MB_BLOCK), BF16), pltpu.VMEM((2, nb, MB_BLOCK), F32)],
        compiler_params=_cparams(("parallel", "parallel", "arbitrary")),
    )(q, k, v)


def _moba_sample_kernel(pt_ref, q_ref, kn_ref, vn_ref, k0_ref, k1_ref, v0_ref, v1_ref, o_ref,
                        qbd_sc, qbdb_sc, kmean_sc, o_sc, m_sc, l_sc):
    del pt_ref
    n = pl.program_id(1)
    nblk = pl.num_programs(1)
    lq, d = q_ref.shape[1], q_ref.shape[2]
    rows = MB_HEADS * lq
    scale = MB_HEADDIM ** -0.5
    own = (lax.broadcasted_iota(jnp.int32, (MB_HEADS, lq, d), 0)
           == lax.broadcasted_iota(jnp.int32, (MB_HEADS, lq, d), 2) // MB_HEADDIM)

    @pl.when(n == 0)
    def _():
        qbd = jnp.where(own, q_ref[0][None], 0.0).reshape(rows, d)
        qbd_sc[...] = qbd
        qbdb_sc[...] = (qbd * scale).astype(BF16)

    qb = qbdb_sc[...]
    k0, k1 = _split(k0_ref[0]), _split(k1_ref[0])
    s0 = _dot(qb, k0[0])
    s1 = _dot(qb, k1[0])
    m = jnp.maximum(jnp.max(s0, axis=-1, keepdims=True), jnp.max(s1, axis=-1, keepdims=True))
    p0 = jnp.exp(s0 - m)
    p1 = jnp.exp(s1 - m)
    o_sc[n] = _dot(p0, v0_ref[0], NT) + _dot(p1, v1_ref[0], NT)
    m_sc[n] = m
    l_sc[n] = jnp.sum(p0, axis=-1, keepdims=True) + jnp.sum(p1, axis=-1, keepdims=True)
    ones = jnp.ones((8, k0_ref.shape[2]), BF16)
    d1 = lambda x: lax.dot_general(ones, x, (NT, ((), ())), preferred_element_type=F32)
    ksum = d1(k0[0]) + d1(k0[1]) + d1(k1[0]) + d1(k1[1])
    kmean_sc[n] = ksum[0:1] * (1.0 / MB_BLOCK)

    @pl.when(n == nblk - 1)
    def _():
        qbd = qbd_sc[...]
        nblocks = o_sc.shape[0]
        s = lax.dot_general(qbd * scale, kn_ref[0], (NT, ((), ())), preferred_element_type=F32)
        qi = lax.broadcasted_iota(jnp.int32, (rows, lq), 0) % lq
        s = jnp.where(lax.broadcasted_iota(jnp.int32, (rows, lq), 1) <= qi, s, NEG)
        m_own = jnp.max(s, axis=-1, keepdims=True)
        p = jnp.exp(s - m_own)
        l_own = jnp.sum(p, axis=-1, keepdims=True)
        o_own = jnp.dot(p, vn_ref[0], preferred_element_type=F32)
        g = [jnp.sum(qbd * kmean_sc[j], axis=-1, keepdims=True) for j in range(nblocks)]
        sel = []
        for j in range(nblocks):
            rank = jnp.zeros((rows, 1), jnp.int32)
            for i in range(nblocks):
                if i != j:
                    ahead = (g[i] >= g[j]) if i < j else (g[i] > g[j])
                    rank = rank + ahead.astype(jnp.int32)
            sel.append(rank < MB_TOPK)
        mx = m_own
        for j in range(nblocks):
            mx = jnp.where(sel[j], jnp.maximum(mx, m_sc[j]), mx)
        w_own = jnp.exp(m_own - mx)
        l = l_own * w_own
        o = o_own * w_own
        for j in range(nblocks):
            wj = jnp.where(sel[j], jnp.exp(m_sc[j] - mx), 0.0)
            l = l + wj * l_sc[j]
            o = o + wj * o_sc[j]
        res = (o / l).reshape(MB_HEADS, lq, d)
        o_ref[0] = jnp.sum(jnp.where(own, res, 0.0), axis=0)


def moba_sample_attend(q, k_new, v_new, cache_k, cache_v, page_table):
    Bs, Ls, D = q.shape
    page = cache_k.shape[2]
    assert MB_BLOCK == 2 * page and PAST_LEN % MB_BLOCK == 0 and Ls <= MB_BLOCK
    nblk = PAST_LEN // MB_BLOCK
    assert nblk >= MB_TOPK
    rows = MB_HEADS * Ls
    newspec = pl.BlockSpec((1, Ls, D), lambda b, n, pt: (b, 0, 0))
    pg0 = pl.BlockSpec((1, D, page), lambda b, n, pt: (pt[b, 2 * n], 0, 0))
    pg1 = pl.BlockSpec((1, D, page), lambda b, n, pt: (pt[b, 2 * n + 1], 0, 0))
    return pl.pallas_call(
        _moba_sample_kernel, name="moba_sample",
        grid_spec=pltpu.PrefetchScalarGridSpec(
            num_scalar_prefetch=1,
            grid=(Bs, nblk),
            in_specs=[newspec, newspec, newspec, pg0, pg1, pg0, pg1],
            out_specs=newspec,
            scratch_shapes=[pltpu.VMEM((rows, D), F32), pltpu.VMEM((rows, D), BF16),
                            pltpu.VMEM((nblk, 1, D), F32), pltpu.VMEM((nblk, rows, D), F32),
                            pltpu.VMEM((nblk, rows, 1), F32), pltpu.VMEM((nblk, rows, 1), F32)]),
        out_shape=jax.ShapeDtypeStruct((Bs, Ls, D), F32),
        compiler_params=_cparams(("parallel", "arbitrary")),
    )(page_table, q, k_new, v_new, cache_k, cache_k, cache_v, cache_v)


def _softplus(z):
    return jnp.maximum(z, 0.0) + jnp.log1p(jnp.exp(-jnp.abs(z)))


def _sigmoid(z):
    return 1.0 / (1.0 + jnp.exp(-z))


def _rwkv_pre_kernel(x_ref, halo_ref, sb_ref, sh_ref, sc_ref, nw_ref, mu_ref, w0_ref, a0_ref,
                     wr_ref, wk_ref, wv_ref, w1_ref, w2_ref, a1_ref, a2_ref, g1_ref, g2_ref,
                     r_o, lw_o, k_o, v_o, a_o, g_o, hl_o, hbuf):
    rr = pl.program_id(1)
    gb, rb, d = x_ref.shape
    n = gb * rb

    def hn(x):
        y = x * lax.rsqrt(jnp.mean(x * x, axis=-1, keepdims=True) + NORM_EPS) * nw_ref[...]
        return y * (1.0 + sc_ref[...]) + sh_ref[...]

    h = hn(x_ref[...])
    prev_row = jnp.where(rr == 0, sb_ref[...], hn(halo_ref[...])[:, 7:8, :])
    hbuf[:, 8:, :] = h
    hbuf[:, 7:8, :] = prev_row
    h2 = h.reshape(n, d)
    xx = hbuf[:, 7:7 + rb, :].reshape(n, d) - h2
    mix = lambda i: h2 + xx * mu_ref[i:i + 1, :]
    dot = lambda a, w_ref: jnp.dot(a.astype(BF16), w_ref[...], preferred_element_type=F32)
    out = lambda o_ref, val: o_ref.__setitem__(Ellipsis, val.reshape(gb, rb, d))
    out(r_o, dot(mix(0), wr_ref))
    out(k_o, dot(mix(2), wk_ref))
    out(v_o, dot(mix(3), wv_ref))
    w_raw = w0_ref[...] + dot(jnp.tanh(dot(mix(1), w1_ref)), w2_ref)
    out(lw_o, -jnp.exp(-_softplus(-w_raw) - 0.5))
    out(a_o, _sigmoid(a0_ref[...] + dot(dot(mix(4), a1_ref), a2_ref)))
    out(g_o, dot(_sigmoid(dot(mix(5), g1_ref)), g2_ref))
    hl_o[...] = h[:, rb - 1:rb, :]


def rwkv_pre(x, shift_buf, shift, scale, nw, mu, w0, a0, ws, gb, rb):
    G, R, D = x.shape
    assert gb == 1 or rb == R
    xspec = pl.BlockSpec((gb, rb, D), lambda g, r: (g, r, 0))
    halo = pl.BlockSpec((gb, 8, D), lambda g, r: (g, jnp.maximum(r * (rb // 8) - 1, 0), 0))
    mspec = pl.BlockSpec((gb, 1, D), lambda g, r: (g, 0, 0))
    vec = pl.BlockSpec((1, D), lambda g, r: (0, 0))
    full = lambda w: pl.BlockSpec(w.shape, lambda g, r: (0, 0))
    mu8 = jnp.pad(mu, ((0, 8 - mu.shape[0]), (0, 0)))
    big = jax.ShapeDtypeStruct((G, R, D), F32)
    return pl.pallas_call(
        _rwkv_pre_kernel, name="rwkv_pre",
        grid=(G // gb, R // rb),
        in_specs=[xspec, halo, mspec, mspec, mspec, vec, full(mu8), vec, vec] + [full(w) for w in ws],
        out_specs=[xspec] * 6 + [mspec],
        out_shape=[big] * 6 + [jax.ShapeDtypeStruct((G, 1, D), F32)],
        scratch_shapes=[pltpu.VMEM((gb, rb + 8, D), F32)],
        compiler_params=_cparams(("parallel", "arbitrary")),
    )(x, x, shift_buf.reshape(G, 1, D), shift, scale, nw.reshape(1, D), mu8, w0.reshape(1, D),
      a0.reshape(1, D), *ws)


def _rwkv_scan_kernel(r_ref, lw_ref, k_ref, v_ref, a_ref, kk_ref, ka_ref, rk_ref, lnw_ref, lnb_ref, s0_ref,
                      y_ref, so_ref, s_sc):
    c = pl.program_id(2)
    C = r_ref.shape[1]
    N = RW_HEADDIM

    @pl.when(c == 0)
    def _():
        s_sc[...] = s0_ref[0]

    hg = s_sc.shape[0]
    rows = hg * C
    stack = lambda ref: jnp.concatenate([ref[0][:, h * N:(h + 1) * N] for h in range(hg)], axis=0)
    pstack = lambda ref: jnp.concatenate(
        [jnp.broadcast_to(ref[:, h * N:(h + 1) * N], (C, N)) for h in range(hg)], axis=0)
    r, lw, kraw, v, a = (stack(ref) for ref in (r_ref, lw_ref, k_ref, v_ref, a_ref))
    kkr = kraw * pstack(kk_ref)
    kk = kkr / jnp.maximum(jnp.sqrt(jnp.sum(kkr * kkr, axis=-1, keepdims=True)), 1e-12)
    k = kraw * (1.0 + (a - 1.0) * pstack(ka_ref))
    tpos = lax.broadcasted_iota(jnp.int32, (rows, N), 0) % C
    cum, sh = lw, 1
    while sh < C:
        cum = cum + jnp.where(tpos >= sh, pltpu.roll(cum, sh, 0), 0.0)
        sh *= 2
    e_in, e_ex, e_neg = jnp.exp(cum), jnp.exp(cum - lw), jnp.exp(-cum)
    ab, bb, kb, rb_ = -kk * e_ex, kk * a * e_neg, k * e_neg, r * e_in
    ri = lax.broadcasted_iota(jnp.int32, (rows, rows), 0)
    ci = lax.broadcasted_iota(jnp.int32, (rows, rows), 1)
    same = (ri // C) == (ci // C)
    strict, incl = jnp.logical_and(same, ci < ri), jnp.logical_and(same, ci <= ri)
    sab, sbb, skb, srb = _split(ab), _split(bb), _split(kb), _split(rb_)
    d1 = lambda x, y, dims=NN: lax.dot_general(x, y, (dims, ((), ())), preferred_element_type=F32)
    lab = jnp.where(strict, d1(sab[0], sbb[0], NT), 0.0)
    lak = jnp.where(strict, d1(sab[0], skb[0], NT), 0.0)
    mrb = jnp.where(incl, d1(srb[0], sbb[0], NT), 0.0)
    mrk = jnp.where(incl, d1(srb[0], skb[0], NT), 0.0)
    tinv = (ri == ci).astype(F32) + lab
    spw = _split(lab)
    for _ in range(int(math.log2(C)) - 1):
        spw = _split(_dot3s(spw, spw))
        tinv = tinv + _dot3s(_split(tinv), spw)
    hsl = lambda t, h: t[h * C:(h + 1) * C]
    xs = [_dot3(jnp.concatenate([hsl(ab, h), hsl(rb_, h)], axis=0), s_sc[h], NT) for h in range(hg)]
    ab_s = jnp.concatenate([x[:C] for x in xs], axis=0)
    rb_s = jnp.concatenate([x[C:] for x in xs], axis=0)
    u = _dot3(tinv, ab_s + _dot3(lak, v))
    y = rb_s + _dot3(jnp.concatenate([mrb, mrk], axis=1), jnp.concatenate([u, v], axis=0))
    for h in range(hg):
        upd = _dot3(jnp.concatenate([hsl(u, h), hsl(v, h)], axis=0),
                    jnp.concatenate([hsl(bb, h), hsl(kb, h)], axis=0), TN)
        s_sc[h] = (s_sc[h] + upd) * e_in[(h + 1) * C - 1:(h + 1) * C, :]
    mean = jnp.mean(y, axis=-1, keepdims=True)
    var = jnp.mean(jnp.square(y - mean), axis=-1, keepdims=True)
    yn = (y - mean) * lax.rsqrt(var + RW_LNX_EPS) * pstack(lnw_ref) + pstack(lnb_ref)
    out = yn + jnp.sum(r * k * pstack(rk_ref), axis=-1, keepdims=True) * v
    y_ref[0] = jnp.concatenate([hsl(out, h) for h in range(hg)], axis=-1)

    @pl.when(c == pl.num_programs(2) - 1)
    def _():
        so_ref[0] = s_sc[...]


def rwkv_scan(r, lw, k, v, a, k_k, k_a, r_k, lnx_w, lnx_b, s0, C, hg):
    B, L, D = r.shape
    N = RW_HEADDIM
    assert L % C == 0 and C & (C - 1) == 0 and RW_HEADS % hg == 0 and (hg * N) % LANE == 0
    tspec = pl.BlockSpec((1, C, hg * N), lambda b, p, c: (b, c, p))
    pspec = pl.BlockSpec((1, hg * N), lambda b, p, c: (0, p))
    sspec = pl.BlockSpec((1, hg, N, N), lambda b, p, c: (b, p, 0, 0))
    vec = lambda t: t.reshape(1, D)
    return pl.pallas_call(
        _rwkv_scan_kernel, name="rwkv_scan",
        grid=(B, RW_HEADS // hg, L // C),
        in_specs=[tspec] * 5 + [pspec] * 5 + [sspec],
        out_specs=[tspec, sspec],
        out_shape=[jax.ShapeDtypeStruct((B, L, D), F32), jax.ShapeDtypeStruct(s0.shape, F32)],
        scratch_shapes=[pltpu.VMEM((hg, N, N), F32)],
        compiler_params=_cparams(("parallel", "parallel", "arbitrary")),
    )(r, lw, k, v, a, vec(k_k), vec(k_a), vec(r_k), vec(lnx_w), vec(lnx_b), s0)


def rwkv_layer(x, shift, scale, gate, nw, shift_buf, wkv0, mu, w_r, w_k, w_v, w0, w1, w2, a0, a1, a2, g1, g2,
               k_k, k_a, r_k, lnx_w, lnx_b, w_o, gb, rb, C, hg):
    ws = [w.astype(BF16) for w in (w_r, w_k, w_v, w1, w2, a1, a2, g1, g2)]
    r, lw, k, v, a, g, hl = rwkv_pre(x, shift_buf, shift, scale, nw, mu, w0, a0, ws, gb, rb)
    y, s_fin = rwkv_scan(r, lw, k, v, a, k_k, k_a, r_k, lnx_w, lnx_b, wkv0, C, hg)
    return oproj(y, g, w_o.astype(BF16), x, gate, gb, rb), hl[:, 0], s_fin


def _conv_kernel(x_ref, halo_ref, buf_ref, w_ref, b_ref, o_ref, sc):
    rr = pl.program_id(1)
    gb, rb, cw = x_ref.shape
    sc[:, :8, :] = jnp.where(rr == 0, buf_ref[...], halo_ref[...])
    sc[:, 8:, :] = x_ref[...]
    y = b_ref[...]
    for t in range(SSD_CONV):
        y = y + sc[:, 8 - (SSD_CONV - 1) + t:8 - (SSD_CONV - 1) + t + rb, :] * w_ref[t:t + 1, :]
    o_ref[...] = y * _sigmoid(y)


def ssd_conv(zx, buf, w, b, gb, rb, cw=512):
    G, R, _ = zx.shape
    assert gb == 1 or rb == R
    off = SSD_D_INNER // cw
    buf8 = jnp.pad(buf, ((0, 0), (8 - (SSD_CONV - 1), 0), (0, 0)))
    w8 = jnp.pad(w, ((0, 8 - SSD_CONV), (0, 0)))
    return pl.pallas_call(
        _conv_kernel, name="ssd_conv",
        grid=(G // gb, R // rb, SSD_CONV_DIM // cw),
        in_specs=[pl.BlockSpec((gb, rb, cw), lambda g, r, j: (g, r, off + j)),
                  pl.BlockSpec((gb, 8, cw), lambda g, r, j: (g, jnp.maximum(r * (rb // 8) - 1, 0), off + j)),
                  pl.BlockSpec((gb, 8, cw), lambda g, r, j: (g, 0, j)),
                  pl.BlockSpec((8, cw), lambda g, r, j: (0, j)),
                  pl.BlockSpec((1, cw), lambda g, r, j: (0, j))],
        out_specs=pl.BlockSpec((gb, rb, cw), lambda g, r, j: (g, r, j)),
        out_shape=jax.ShapeDtypeStruct((G, R, SSD_CONV_DIM), F32),
        scratch_shapes=[pltpu.VMEM((gb, rb + 8, cw), F32)],
        compiler_params=_cparams(("parallel", "arbitrary", "arbitrary")),
    )(zx, zx, buf8, w8, b.reshape(1, SSD_CONV_DIM))


def _col2row(col, eye):
    return jnp.sum(jnp.where(eye, col, 0.0), axis=0, keepdims=True)


def _ssd_scan_kernel(x_ref, bm_ref, cm_ref, z_ref, dt_ref, dtb_ref, alog_ref, dsk_ref, nw_ref, s0_ref,
                     y_ref, so_ref, s_sc):
    c = pl.program_id(2)
    Q = x_ref.shape[1]
    P, N, R = SSD_HEADDIM, SSD_STATE, SSD_HPG

    @pl.when(c == 0)
    def _():
        s_sc[...] = s0_ref[0]

    ti = lax.broadcasted_iota(jnp.int32, (Q, Q), 0)
    si = lax.broadcasted_iota(jnp.int32, (Q, Q), 1)
    causal, eye = si <= ti, si == ti
    lo = lax.broadcasted_iota(jnp.int32, (Q, LANE), 1) < P
    dt = _softplus(dt_ref[0] + dtb_ref[...])
    cs = _dot(causal.astype(F32), dt * (-jnp.exp(alog_ref[...])), hi=True)
    x, bm, cm = x_ref[0], bm_ref[0], cm_ref[0]
    cb = _dot(cm, bm, NT)
    s_all = s_sc[...].reshape(R * P, N)
    ys = _dot(cm, s_all, NT)
    y_pairs, xw_pairs, e_last = [], [], []
    for j in range(R // 2):
        xp = x[:, j * LANE:(j + 1) * LANE]
        yh, grow, wend = [], [], []
        for r in (2 * j, 2 * j + 1):
            cs_col, dt_col = cs[:, r:r + 1], dt[:, r:r + 1]
            cs_last = cs[Q - 1:Q, r:r + 1]
            dec = jnp.exp(jnp.where(causal, cs_col - _col2row(cs_col, eye), -jnp.inf))
            yh.append(_dot(cb * dec * _col2row(dt_col, eye), xp))
            grow.append(jnp.exp(cs_col))
            wend.append(jnp.exp(cs_last - cs_col) * dt_col)
            e_last.append(jnp.exp(cs_last))
        yp = jnp.where(lo, yh[0], yh[1]) + ys[:, j * LANE:(j + 1) * LANE] * jnp.where(lo, grow[0], grow[1])
        y_pairs.append(yp + xp * dsk_ref[:, j * LANE:(j + 1) * LANE])
        xw_pairs.append(xp * jnp.where(lo, wend[0], wend[1]))
    upd = _dot(jnp.concatenate(xw_pairs, axis=-1), bm, TN)
    for r in range(R):
        s_sc[r] = s_sc[r] * e_last[r] + upd[r * P:(r + 1) * P, :]
    z = z_ref[0]
    yz = jnp.concatenate(y_pairs, axis=-1) * (z * _sigmoid(z))
    y_ref[0] = yz * lax.rsqrt(jnp.mean(yz * yz, axis=-1, keepdims=True) + NORM_EPS) * nw_ref[...]

    @pl.when(c == pl.num_programs(2) - 1)
    def _():
        so_ref[0] = s_sc[...]


def ssd_chunk_scan(xbc, zx, dt_raw, dt_bias, a_log, d_skip, norm_w, s0, Q):
    B, L, _ = xbc.shape
    G, R, P, N = SSD_GROUPS, SSD_HPG, SSD_HEADDIM, SSD_STATE
    gw = R * P
    assert L % Q == 0 and N == LANE
    pad_heads = lambda t: jnp.pad(t.reshape(G, R), ((0, 0), (0, LANE - R))).reshape(1, G * LANE)
    xs = pl.BlockSpec((1, Q, gw), lambda b, g, c: (b, c, g))
    bs = pl.BlockSpec((1, Q, N), lambda b, g, c: (b, c, SSD_D_INNER // N + g))
    cs = pl.BlockSpec((1, Q, N), lambda b, g, c: (b, c, SSD_D_INNER // N + G + g))
    ds = pl.BlockSpec((1, Q, LANE), lambda b, g, c: (b, c, g))
    hs = pl.BlockSpec((1, LANE), lambda b, g, c: (0, g))
    ws = pl.BlockSpec((1, gw), lambda b, g, c: (0, g))
    ss = pl.BlockSpec((1, R, P, N), lambda b, g, c: (b, g, 0, 0))
    return pl.pallas_call(
        _ssd_scan_kernel, name="ssd_scan",
        grid=(B, G, L // Q),
        in_specs=[xs, bs, cs, xs, ds, hs, hs, ws, ws, ss],
        out_specs=[xs, ss],
        out_shape=[jax.ShapeDtypeStruct((B, L, SSD_D_INNER), F32), jax.ShapeDtypeStruct(s0.shape, F32)],
        scratch_shapes=[pltpu.VMEM((R, P, N), F32)],
        compiler_params=_cparams(("parallel", "parallel", "arbitrary")),
    )(xbc, xbc, xbc, zx, dt_raw, pad_heads(dt_bias), pad_heads(a_log),
      jnp.repeat(d_skip, P).reshape(1, SSD_D_INNER), norm_w.reshape(1, SSD_D_INNER), s0)


def ssd_layer(x, shift, scale, gate, nw, conv_buf, ssm0, w_in, conv_w, conv_b, dt_bias, a_log, d_skip, norm_w,
              w_out, gb, rb, Q):
    G, R, D = x.shape
    nzx = SSD_D_INNER + SSD_CONV_DIM
    h = modnorm(x, shift, scale, nw, gb, rb, BF16)
    zx = mm(h, w_in[:, :nzx])
    w_dt = jnp.pad(w_in[:, nzx:].reshape(D, SSD_GROUPS, SSD_HPG), ((0, 0), (0, 0), (0, LANE - SSD_HPG)))
    dt_raw = mm(h, w_dt.reshape(D, SSD_GROUPS * LANE))
    xbc = ssd_conv(zx, conv_buf, conv_w, conv_b, gb, rb)
    y, s_fin = ssd_chunk_scan(xbc, zx, dt_raw, dt_bias, a_log, d_skip, norm_w, ssm0, Q)
    new_conv = zx[:, R - (SSD_CONV - 1):, SSD_D_INNER:]
    return oproj(y, None, w_out.astype(BF16), x, gate, gb, rb), new_conv, s_fin


def _mlstm_kernel(q_ref, k_ref, v_ref, o_ref, g_ref, bif_ref, nw_ref, c0_ref, n0_ref, m0_ref,
                  y_ref, co_ref, no_ref, mo_ref, c_sc, n_sc, m_sc):
    c = pl.program_id(2)
    Q = q_ref.shape[1]
    DK, DV = ML_QK_DIM, ML_V_DIM

    @pl.when(c == 0)
    def _():
        c_sc[...] = c0_ref[0]
        n_sc[...] = n0_ref[0, 0]
        m_sc[...] = m0_ref[0, 0]

    ti = lax.broadcasted_iota(jnp.int32, (Q, Q), 0)
    si = lax.broadcasted_iota(jnp.int32, (Q, Q), 1)
    causal, eye = si <= ti, si == ti
    gates = g_ref[0] + bif_ref[...]
    bcum = _dot(causal.astype(F32), -_softplus(-gates), hi=True)
    ys = []
    for hh in range(2):
        i_col, b_col = gates[:, hh:hh + 1], bcum[:, 2 + hh:3 + hh]
        m_prev = m_sc[:, hh:hh + 1]
        q = q_ref[0][:, hh * DK:(hh + 1) * DK] * DK ** -0.5
        k = k_ref[0][:, hh * DK:(hh + 1) * DK]
        v = v_ref[0][:, hh * DV:(hh + 1) * DV]
        cst, nst = c_sc[hh], n_sc[hh:hh + 1, :]
        dmat = jnp.where(causal, b_col - _col2row(b_col, eye) + _col2row(i_col, eye), -jnp.inf)
        m_inter = b_col + m_prev
        m_t = jnp.maximum(m_inter, jnp.max(dmat, axis=-1, keepdims=True))
        w_inter = jnp.exp(m_inter - m_t)
        qk = _dot(q, k, NT) * jnp.exp(dmat - m_t)
        num = _dot(qk, v) + w_inter * _dot(q, cst, NT)
        den = jnp.sum(qk, axis=-1, keepdims=True) + w_inter * jnp.sum(q * nst, axis=-1, keepdims=True)
        hout = num / jnp.maximum(jnp.abs(den), jnp.exp(-m_t))
        m_last, b_last = m_t[Q - 1:Q], b_col[Q - 1:Q]
        w_state = jnp.exp(b_last - b_col + i_col - m_last)
        decay = jnp.exp(b_last + m_prev - m_last)
        c_sc[hh] = decay * cst + _dot(v * w_state, k, TN)
        n_sc[hh:hh + 1, :] = decay * nst + jnp.sum(w_state * k, axis=0, keepdims=True)
        m_sc[:, hh:hh + 1] = m_last
        hn = hout * lax.rsqrt(jnp.mean(hout * hout, axis=-1, keepdims=True) + NORM_EPS)
        ys.append(_sigmoid(o_ref[0][:, hh * DV:(hh + 1) * DV]) * (hn * nw_ref[:, hh * DV:(hh + 1) * DV]))
    y_ref[0] = jnp.concatenate(ys, axis=-1)

    @pl.when(c == pl.num_programs(2) - 1)
    def _():
        co_ref[0] = c_sc[...]
        no_ref[0, 0] = n_sc[...]
        mo_ref[0, 0] = m_sc[...]


def mlstm_scan(proj, gates, b_if, norm_w, c0, n0, m0, Q):
    B, L, _ = proj.shape
    H, DK, DV = ML_HEADS, ML_QK_DIM, ML_V_DIM
    assert L % Q == 0 and 2 * DK == LANE
    npair = H // 2
    qs = pl.BlockSpec((1, Q, LANE), lambda b, j, c: (b, c, j))
    ks = pl.BlockSpec((1, Q, LANE), lambda b, j, c: (b, c, npair + j))
    vs = pl.BlockSpec((1, Q, 2 * DV), lambda b, j, c: (b, c, H * DK // DV + j))
    os_ = pl.BlockSpec((1, Q, 2 * DV), lambda b, j, c: (b, c, H * DK // DV + npair + j))
    ys = pl.BlockSpec((1, Q, 2 * DV), lambda b, j, c: (b, c, j))
    cs = pl.BlockSpec((1, 2, DV, DK), lambda b, j, c: (b, j, 0, 0))
    ns = pl.BlockSpec((1, 1, 2, DK), lambda b, j, c: (b, j, 0, 0))
    ms = pl.BlockSpec((1, 1, 1, 2), lambda b, j, c: (b, j, 0, 0))
    bi, bf = b_if[:H].reshape(npair, 2), b_if[H:].reshape(npair, 2)
    bias = jnp.pad(jnp.concatenate([bi, bf], axis=-1), ((0, 0), (0, LANE - 4))).reshape(1, npair * LANE)
    y, c, n, m = pl.pallas_call(
        _mlstm_kernel, name="mlstm",
        grid=(B, npair, L // Q),
        in_specs=[qs, ks, vs, os_, qs, pl.BlockSpec((1, LANE), lambda b, j, c: (0, j)),
                  pl.BlockSpec((1, 2 * DV), lambda b, j, c: (0, j)), cs, ns, ms],
        out_specs=[ys, cs, ns, ms],
        out_shape=[jax.ShapeDtypeStruct((B, L, H * DV), F32), jax.ShapeDtypeStruct((B, H, DV, DK), F32),
                   jax.ShapeDtypeStruct((B, npair, 2, DK), F32), jax.ShapeDtypeStruct((B, npair, 1, 2), F32)],
        scratch_shapes=[pltpu.VMEM((2, DV, DK), F32), pltpu.VMEM((2, DK), F32), pltpu.VMEM((1, 2), F32)],
        compiler_params=_cparams(("parallel", "parallel", "arbitrary")),
    )(proj, proj, proj, proj, gates, bias, norm_w.reshape(1, H * DV), c0,
      n0.reshape(B, npair, 2, DK), m0.reshape(B, npair, 1, 2))
    return y, c, n.reshape(B, H, DK), m.reshape(B, H)


def mlstm_layer(x, shift, scale, gate, nw, c0, n0, m0, w_in, b_if, norm_w, w_o, gb, rb, Q):
    D = x.shape[-1]
    H = ML_HEADS
    nmain = 2 * H * ML_QK_DIM + 2 * H * ML_V_DIM
    h = modnorm(x, shift, scale, nw, gb, rb, BF16)
    proj = mm(h, w_in[:, :nmain])
    wi, wf = w_in[:, nmain:nmain + H].reshape(D, H // 2, 2), w_in[:, nmain + H:].reshape(D, H // 2, 2)
    wg = jnp.pad(jnp.concatenate([wi, wf], axis=-1), ((0, 0), (0, 0), (0, LANE - 4)))
    gates = mm(h, wg.reshape(D, (H // 2) * LANE))
    y, c, n, m = mlstm_scan(proj, gates, b_if, norm_w, c0, n0, m0, Q)
    return oproj(y, None, w_o.astype(BF16), x, gate, gb, rb), c, n, m


def rmsnorm(x, w):
    xf = x.astype(jnp.float32)
    y = xf * lax.rsqrt(jnp.mean(xf * xf, axis=-1, keepdims=True) + NORM_EPS)
    return (y * w).astype(x.dtype)


def sublayer_in(x, g, shift, scale):
    return rmsnorm(x, g) * (1 + scale) + shift


def causal_dwconv(x, buf, w, b):
    L, K = x.shape[1], w.shape[0]
    xp = jnp.concatenate([buf.astype(x.dtype), x], axis=1)
    y = b + sum(xp[:, k:k + L] * w[k] for k in range(K))
    return y, xp[:, -(K - 1):]


def ssd_scan(x, dt, a, bm, cm, s0):
    B, L = x.shape[0], x.shape[1]
    Q = math.gcd(L, SSD_CHUNK)
    nc = L // Q
    chunks = lambda t: jnp.moveaxis(t.reshape(B, nc, Q, *t.shape[2:]), 1, 0)
    causal = jnp.tril(jnp.ones((Q, Q), bool))[None, :, :, None, None]

    def step(s, inp):
        xc, dtc, bc, cc = inp
        xf, bf, cf = xc.astype(jnp.float32), bc.astype(jnp.float32), cc.astype(jnp.float32)
        cs = jnp.cumsum(dtc * a, axis=1)
        decay = jnp.exp(jnp.where(causal, cs[:, :, None] - cs[:, None], -jnp.inf))
        cb = jnp.einsum('btgn,bsgn->btsg', cf, bf)
        wgt = cb[..., None] * decay * dtc[:, None]
        y = jnp.einsum('btsgr,bsgrp->btgrp', wgt, xf)
        y = y + jnp.einsum('btgn,bgrpn->btgrp', cf, s) * jnp.exp(cs)[..., None]
        w_end = jnp.exp(cs[:, -1:] - cs) * dtc
        s_new = s * jnp.exp(cs[:, -1])[..., None, None] + jnp.einsum('bsgr,bsgn,bsgrp->bgrpn', w_end, bf, xf)
        return s_new, y

    s_fin, y = lax.scan(step, s0, (chunks(x), chunks(dt), chunks(bm), chunks(cm)))
    y = jnp.moveaxis(y, 0, 1).reshape(x.shape)
    return y, s_fin


def ssd_mixer(h, conv_buf, ssm0, w_in, conv_w, conv_b, dt_bias, a_log, d_skip, norm_w, w_out):
    B, L, _ = h.shape
    G, R, P, N = SSD_GROUPS, SSD_HPG, SSD_HEADDIM, SSD_STATE
    zxbcdt = mm(h, w_in)
    z = zxbcdt[..., :SSD_D_INNER]
    xbc = zxbcdt[..., SSD_D_INNER:SSD_D_INNER + SSD_CONV_DIM]
    dt_raw = zxbcdt[..., SSD_D_INNER + SSD_CONV_DIM:]
    xbc, new_conv = causal_dwconv(xbc, conv_buf, conv_w, conv_b)
    xbc = jax.nn.silu(xbc)
    xs = xbc[..., :SSD_D_INNER].reshape(B, L, G, R, P)
    bm = xbc[..., SSD_D_INNER:SSD_D_INNER + G * N].reshape(B, L, G, N)
    cm = xbc[..., SSD_D_INNER + G * N:].reshape(B, L, G, N)
    dt = jax.nn.softplus(dt_raw.astype(jnp.float32) + dt_bias).reshape(B, L, G, R)
    a = -jnp.exp(a_log.astype(jnp.float32)).reshape(G, R)
    y, s_fin = ssd_scan(xs, dt, a, bm, cm, ssm0.astype(jnp.float32).reshape(B, G, R, P, N))
    y = y + xs.astype(jnp.float32) * d_skip.reshape(G, R, 1)
    yz = y.reshape(B, L, SSD_D_INNER) * jax.nn.silu(z.astype(jnp.float32))
    yz = rmsnorm(yz.reshape(B, L, G, -1), norm_w.reshape(G, -1)).reshape(B, L, SSD_D_INNER)
    return mm(yz.astype(h.dtype), w_out), new_conv, s_fin.reshape(B, SSD_HEADS, P, N)


def rwkv7_mixer(h, shift_buf, wkv0, mu, w_r, w_k, w_v, w0, w1, w2, a0, a1, a2, g1, g2,
                k_k, k_a, r_k, lnx_w, lnx_b, w_o):
    B, L, D = h.shape
    H, N = RW_HEADS, RW_HEADDIM
    prev = jnp.concatenate([shift_buf[:, None, :].astype(h.dtype), h[:, :-1]], axis=1)
    xx = prev - h
    xr, xw, xk = h + xx * mu[0], h + xx * mu[1], h + xx * mu[2]
    xv, xa, xg = h + xx * mu[3], h + xx * mu[4], h + xx * mu[5]
    r, k, v = mm(xr, w_r), mm(xk, w_k), mm(xv, w_v)
    w_raw = (w0 + mm(jnp.tanh(mm(xw, w1)), w2)).astype(jnp.float32)
    decay = jnp.exp(-jnp.exp(-jax.nn.softplus(-w_raw) - 0.5))
    a = jax.nn.sigmoid(a0 + mm(mm(xa, a1), a2))
    g = mm(jax.nn.sigmoid(mm(xg, g1)), g2)
    heads = lambda t: t.reshape(B, L, H, N).astype(jnp.float32)
    kk = heads(k * k_k)
    kk = kk / jnp.maximum(jnp.sqrt(jnp.sum(kk * kk, axis=-1, keepdims=True)), 1e-12)
    k = k * (1 + (a - 1) * k_a)
    rh, kh, vh, ah, wh = heads(r), heads(k), heads(v), heads(a), heads(decay)
    tm = lambda t: jnp.moveaxis(t, 1, 0)

    def step(S, inp):
        r_t, w_t, k_t, v_t, kk_t, a_t = inp
        sa = jnp.einsum('bhvk,bhk->bhv', S, -kk_t)
        S = S * w_t[:, :, None, :] + sa[..., None] * (kk_t * a_t)[:, :, None, :] + v_t[..., None] * k_t[:, :, None, :]
        return S, jnp.einsum('bhvk,bhk->bhv', S, r_t)

    S_fin, y = lax.scan(step, wkv0.astype(jnp.float32), (tm(rh), tm(wh), tm(kh), tm(vh), tm(kk), tm(ah)))
    y = jnp.moveaxis(y, 0, 1)
    mean = jnp.mean(y, axis=-1, keepdims=True)
    var = jnp.mean(jnp.square(y - mean), axis=-1, keepdims=True)
    y = ((y - mean) * lax.rsqrt(var + RW_LNX_EPS)).reshape(B, L, D) * lnx_w + lnx_b
    bonus = jnp.sum(rh * kh * r_k, axis=-1, keepdims=True) * vh
    y = (y + bonus.reshape(B, L, D)).astype(h.dtype)
    return mm(y * g, w_o), h[:, -1], S_fin


def rope_partial(x, pos):
    half = ROPE_DIM // 2
    inv = ROPE_THETA ** (-jnp.arange(half, dtype=jnp.float32) / half)
    ang = pos.astype(jnp.float32)[:, None] * inv
    cos, sin = jnp.cos(ang)[:, None, :], jnp.sin(ang)[:, None, :]
    x1 = x[..., :half].astype(jnp.float32)
    x2 = x[..., half:ROPE_DIM].astype(jnp.float32)
    rot = jnp.concatenate([x1 * cos - x2 * sin, x2 * cos + x1 * sin], axis=-1).astype(x.dtype)
    return jnp.concatenate([rot, x[..., ROPE_DIM:]], axis=-1)


def moba_qkv(h, pos, w_qkv, q_norm, k_norm):
    B, L, _ = h.shape
    qkv = mm(h, w_qkv).reshape(B, L, 3, MB_HEADS, MB_HEADDIM)
    q = rope_partial(rmsnorm(qkv[:, :, 0], q_norm), pos)
    k = rope_partial(rmsnorm(qkv[:, :, 1], k_norm), pos)
    return q, k, qkv[:, :, 2]


def moba_attend_one(qpos, q, k_all, v_all):
    Lq, Lk = q.shape[0], k_all.shape[0]
    H, Dh, K = MB_HEADS, MB_HEADDIM, MB_TOPK
    nb = max(-(-Lk // MB_BLOCK), K)
    pad = nb * MB_BLOCK - Lk
    blocks = lambda t: jnp.pad(t, ((0, pad), (0, 0), (0, 0))).reshape(nb, MB_BLOCK, H, Dh).transpose(2, 0, 1, 3)
    kb, vb = blocks(k_all), blocks(v_all)
    kmean = jnp.mean(kb.astype(jnp.float32), axis=2)
    cur = qpos // MB_BLOCK
    gate = jnp.einsum('qhd,hnd->hqn', q.astype(jnp.float32), kmean)
    fully_past = jnp.arange(nb)[None, :] < cur[:, None]
    gate = jnp.where(fully_past[None], gate, -jnp.inf)
    _, top = lax.top_k(gate, K)
    sel_ok = jnp.arange(K)[None, :] < cur[:, None]
    idx = jnp.concatenate([top, jnp.broadcast_to(cur[None, :, None], (H, Lq, 1))], axis=-1)
    qb = math.gcd(Lq, MB_QBLOCK)
    nq = Lq // qb
    hidx = jnp.arange(H)[:, None, None]
    scale = MB_HEADDIM ** -0.5

    def one_block(args):
        qs, ps, ids, ok = args
        kg, vg = kb[hidx, ids], vb[hidx, ids]
        s = jnp.einsum('qhd,hqsbd->hqsb', qs, kg).astype(jnp.float32) * scale
        kpos = ids[..., None] * MB_BLOCK + jnp.arange(MB_BLOCK)
        valid = jnp.concatenate([ok, jnp.ones((qb, 1), bool)], axis=-1)
        mask = valid[None, :, :, None] & (kpos <= ps[None, :, None, None])
        s = jnp.where(mask, s, -jnp.inf)
        p = jax.nn.softmax(s.reshape(H, qb, -1), axis=-1).reshape(s.shape)
        return jnp.einsum('hqsb,hqsbd->qhd', p.astype(vg.dtype), vg)

    out = lax.map(one_block, (q.reshape(nq, qb, H, Dh), qpos.reshape(nq, qb),
                              idx.reshape(H, nq, qb, K + 1).transpose(1, 0, 2, 3), sel_ok.reshape(nq, qb, K)))
    return out.reshape(Lq, H, Dh)


def moba_sample_one(qpos, ck, cv, args):
    q, k_new, v_new, pt = args
    k_all = jnp.concatenate([ck[pt].reshape(-1, MB_HEADS, MB_HEADDIM).astype(k_new.dtype), k_new], axis=0)
    v_all = jnp.concatenate([cv[pt].reshape(-1, MB_HEADS, MB_HEADDIM).astype(v_new.dtype), v_new], axis=0)
    return moba_attend_one(qpos, q, k_all, v_all)


def mlstm_chunked(q, k, v, i_pre, log_f, c0, n0, m0):
    B, L = q.shape[0], q.shape[1]
    Q = math.gcd(L, ML_CHUNK)
    nc = L // Q
    chunks = lambda t: jnp.moveaxis(t.reshape(B, nc, Q, *t.shape[2:]), 1, 0)
    causal = jnp.tril(jnp.ones((Q, Q), bool))[None, :, :, None]

    def step(carry, inp):
        C, n, m = carry
        qc, kc, vc, ic, fc = inp
        b = jnp.cumsum(fc, axis=1)
        dmat = jnp.where(causal, b[:, :, None] - b[:, None] + ic[:, None], -jnp.inf)
        m_inter = b + m[:, None]
        m_t = jnp.maximum(m_inter, jnp.max(dmat, axis=2))
        w_intra = jnp.exp(dmat - m_t[:, :, None])
        w_inter = jnp.exp(m_inter - m_t)
        qk = jnp.einsum('bthd,bshd->btsh', qc, kc) * w_intra
        num = jnp.einsum('btsh,bshv->bthv', qk, vc) + w_inter[..., None] * jnp.einsum('bhvd,bthd->bthv', C, qc)
        den = jnp.sum(qk, axis=2) + w_inter * jnp.einsum('bhd,bthd->bth', n, qc)
        hout = num / jnp.maximum(jnp.abs(den), jnp.exp(-m_t))[..., None]
        m_last = m_t[:, -1]
        w_state = jnp.exp(b[:, -1:] - b + ic - m_last[:, None])
        decay = jnp.exp(b[:, -1] + m - m_last)
        C = decay[..., None, None] * C + jnp.einsum('bsh,bshv,bshd->bhvd', w_state, vc, kc)
        n = decay[..., None] * n + jnp.einsum('bsh,bshd->bhd', w_state, kc)
        return (C, n, m_last), hout

    carry0 = (c0.astype(jnp.float32), n0.astype(jnp.float32), m0.astype(jnp.float32))
    (C, n, m), hout = lax.scan(step, carry0, (chunks(q), chunks(k), chunks(v), chunks(i_pre), chunks(log_f)))
    return jnp.moveaxis(hout, 0, 1).reshape(v.shape), C, n, m


def mlstm_mixer(h, c0, n0, m0, w_in, b_if, norm_w, w_o):
    B, L, _ = h.shape
    H, DK, DV = ML_HEADS, ML_QK_DIM, ML_V_DIM
    proj = mm(h, w_in)
    o1, o2, o3, o4 = H * DK, 2 * H * DK, 2 * H * DK + H * DV, 2 * H * DK + 2 * H * DV
    q = proj[..., :o1].reshape(B, L, H, DK).astype(jnp.float32) * DK ** -0.5
    k = proj[..., o1:o2].reshape(B, L, H, DK).astype(jnp.float32)
    v = proj[..., o2:o3].reshape(B, L, H, DV).astype(jnp.float32)
    o = jax.nn.sigmoid(proj[..., o3:o4])
    gates = proj[..., o4:].astype(jnp.float32) + b_if
    i_pre, log_f = gates[..., :H], jax.nn.log_sigmoid(gates[..., H:])
    hh, C, n, m = mlstm_chunked(q, k, v, i_pre, log_f, c0, n0, m0)
    hh = rmsnorm(hh, norm_w.reshape(H, DV)).reshape(B, L, H * DV)
    return mm(o * hh.astype(h.dtype), w_o), C, n, m


def kernel(x_prompt, x_sample, state_ssd_conv, state_ssd, state_rwkv_shift, state_rwkv, cache_k, cache_v, state_mlstm_c, state_mlstm_n, state_mlstm_m, page_table, c_prompt, c_sample, ada_w, ada_b, norm_w, ffn_w_in, ffn_w_out, ssd_w_in, ssd_conv_w, ssd_conv_b, ssd_dt_bias, ssd_a_log, ssd_d, ssd_norm_w, ssd_w_out, rw_mu, rw_w_r, rw_w_k, rw_w_v, rw_w0, rw_w1, rw_w2, rw_a0, rw_a1, rw_a2, rw_g1, rw_g2, rw_k_k, rw_k_a, rw_r_k, rw_lnx_w, rw_lnx_b, rw_w_o, mb_w_qkv, mb_q_norm, mb_k_norm, mb_w_o, ml_w_in, ml_b_if, ml_norm_w, ml_w_o):
    B, S = x_prompt.shape[0], x_prompt.shape[1]
    Bs, Ls = x_sample.shape[0], x_sample.shape[1]
    D = D_MODEL
    pos_p = jnp.arange(S, dtype=jnp.int32)
    pos_s = PAST_LEN + jnp.arange(Ls, dtype=jnp.int32)
    xp, xs = x_prompt, x_sample
    outs = {}
    c_all = jnp.concatenate([c_prompt, c_sample], axis=0)
    sc_all = jax.nn.silu(c_all)
    for i in range(DEPTH):
        kind, j = i % N_MIXERS, i // N_MIXERS
        mod = (mm(sc_all, ada_w[i]) + ada_b[i]).reshape(B + Bs, N_MOD, 1, D)
        mp, ms = mod[:B], mod[B:]
        wi0, wo0 = ffn_w_in[i, 0].astype(BF16), ffn_w_out[i, 0].astype(BF16)
        wi1, wo1 = ffn_w_in[i, 1].astype(BF16), ffn_w_out[i, 1].astype(BF16)
        xp = ffn(xp, mp[:, 0], mp[:, 1], mp[:, 2], norm_w[i, 0], wi0, wo0, 1, 512)
        xs = ffn(xs, ms[:, 0], ms[:, 1], ms[:, 2], norm_w[i, 0], wi0, wo0, 64, Ls)
        m1p, m1s = (mp[:, 3], mp[:, 4], mp[:, 5], norm_w[i, 1]), (ms[:, 3], ms[:, 4], ms[:, 5], norm_w[i, 1])
        if kind == 0:
            w = (ssd_w_in[j], ssd_conv_w[j], ssd_conv_b[j], ssd_dt_bias[j], ssd_a_log[j], ssd_d[j], ssd_norm_w[j], ssd_w_out[j])
            xp, cp, sp = ssd_layer(xp, *m1p, jnp.zeros((B, SSD_CONV - 1, SSD_CONV_DIM), F32),
                                   jnp.zeros((B, SSD_HEADS, SSD_HEADDIM, SSD_STATE), F32), *w, 1, 512, SSD_CHUNK)
            xs, cs_, ss = ssd_layer(xs, *m1s, state_ssd_conv[j], state_ssd[j], *w, 64, Ls, Ls)
            outs[2], outs[3], outs[4], outs[5] = cp[None], cs_[None], sp[None], ss[None]
        elif kind == 1:
            w = (rw_mu[j], rw_w_r[j], rw_w_k[j], rw_w_v[j], rw_w0[j], rw_w1[j], rw_w2[j], rw_a0[j], rw_a1[j],
                 rw_a2[j], rw_g1[j], rw_g2[j], rw_k_k[j], rw_k_a[j], rw_r_k[j], rw_lnx_w[j], rw_lnx_b[j], rw_w_o[j])
            xp, shp, sp = rwkv_layer(xp, *m1p, jnp.zeros((B, D), F32),
                                     jnp.zeros((B, RW_HEADS, RW_HEADDIM, RW_HEADDIM), F32), *w, 1, 256, 64, 4)
            xs, shs, ss = rwkv_layer(xs, *m1s, state_rwkv_shift[j], state_rwkv[j], *w, 32, Ls, Ls, RW_HEADS)
            outs[6], outs[7], outs[8], outs[9] = shp[None], shs[None], sp[None], ss[None]
        elif kind == 2:
            hp = modnorm(xp, mp[:, 3], mp[:, 4], norm_w[i, 1], 1, 512, BF16)
            hs = modnorm(xs, ms[:, 3], ms[:, 4], norm_w[i, 1], 64, Ls, BF16)
            qkv_p, qkv_s = mm(hp, mb_w_qkv[j]), mm(hs, mb_w_qkv[j])
            qp, kp = qkpost(qkv_p, pos_p, mb_q_norm[j], mb_k_norm[j], 1, 512)
            qs, ks_ = qkpost(qkv_s, pos_s, mb_q_norm[j], mb_k_norm[j], 16, Ls)
            vp, vs = qkv_p[..., 2 * D:], qkv_s[..., 2 * D:]
            op = moba_prompt_attend_t(qp, kp, vp)
            nl, npool, page = cache_k.shape[0], cache_k.shape[1], cache_k.shape[2]
            tview = lambda t: jnp.transpose(t, (0, 1, 3, 4, 2)).reshape(nl * npool, D, page)
            os_ = moba_sample_attend(qs, ks_, vs, tview(cache_k), tview(cache_v), page_table + j * npool)
            w_o = mb_w_o[j].astype(BF16)
            xp = oproj(op, None, w_o, xp, mp[:, 5], 1, 512)
            xs = oproj(os_, None, w_o, xs, ms[:, 5], 64, Ls)
            hd = (MB_HEADS, MB_HEADDIM)
            outs[10], outs[11] = kp.reshape(1, B, S, *hd), ks_.reshape(1, Bs, Ls, *hd)
            outs[12], outs[13] = vp.reshape(1, B, S, *hd), vs.reshape(1, Bs, Ls, *hd)
        else:
            w = (ml_w_in[j], ml_b_if[j], ml_norm_w[j], ml_w_o[j])
            xp, cp, np_, mp_ = mlstm_layer(xp, *m1p, jnp.zeros((B, ML_HEADS, ML_V_DIM, ML_QK_DIM), F32),
                                           jnp.zeros((B, ML_HEADS, ML_QK_DIM), F32), jnp.zeros((B, ML_HEADS), F32),
                                           *w, 1, 512, 128)
            xs, cs_, ns_, ms_ = mlstm_layer(xs, *m1s, state_mlstm_c[j], state_mlstm_n[j], state_mlstm_m[j],
                                            *w, 64, Ls, Ls)
            outs[14], outs[15], outs[16], outs[17], outs[18], outs[19] = cp[None], cs_[None], np_[None], ns_[None], mp_[None], ms_[None]
        xp = ffn(xp, mp[:, 6], mp[:, 7], mp[:, 8], norm_w[i, 2], wi1, wo1, 1, 512)
        xs = ffn(xs, ms[:, 6], ms[:, 7], ms[:, 8], norm_w[i, 2], wi1, wo1, 64, Ls)
    outs[0], outs[1] = xp, xs
    return tuple(outs[i] for i in range(20))
```

```python
import functools
import math

import jax
import jax.numpy as jnp
from jax import lax
from jax.experimental import pallas as pl
from jax.experimental.pallas import tpu as pltpu

D_MODEL = 1024
DEPTH = 4
PAST_LEN = 2048
N_MIXERS = 4
NORM_EPS = 1e-6
N_MOD = 9
D_FF = 2816

SSD_D_INNER = 2 * D_MODEL
SSD_HEADDIM = 64
SSD_HEADS = SSD_D_INNER // SSD_HEADDIM
SSD_GROUPS = 4
SSD_HPG = SSD_HEADS // SSD_GROUPS
SSD_STATE = 128
SSD_CONV = 4
SSD_CHUNK = 128
SSD_CONV_DIM = SSD_D_INNER + 2 * SSD_GROUPS * SSD_STATE

RW_HEADDIM = 64
RW_HEADS = D_MODEL // RW_HEADDIM
RW_LNX_EPS = 64e-5

MB_HEADS = 16
MB_HEADDIM = D_MODEL // MB_HEADS
MB_BLOCK = 256
MB_TOPK = 3
MB_QBLOCK = 16
ROPE_DIM = MB_HEADDIM // 4
ROPE_THETA = 500000.0

ML_HEADS = 8
ML_QK_DIM = D_MODEL // 2 // ML_HEADS
ML_V_DIM = D_MODEL // ML_HEADS
ML_CHUNK = 64

LANE = 128
VMEM_LIMIT = 48 * 1024 * 1024

BF16 = jnp.bfloat16
F32 = jnp.float32


def _cparams(sem):
    return pltpu.CompilerParams(dimension_semantics=sem, vmem_limit_bytes=VMEM_LIMIT)


def _mm_kernel(a_ref, w_ref, o_ref):
    o_ref[...] = jnp.dot(a_ref[...].astype(BF16), w_ref[...], preferred_element_type=F32)


def _pick(n, cands):
    for c in cands:
        if n % c == 0:
            return c
    raise ValueError(n)


def mm(a, w):
    lead = a.shape[:-1]
    K = a.shape[-1]
    N = w.shape[-1]
    a2 = a.reshape(-1, K)
    M = a2.shape[0]
    Mp = -(-M // 8) * 8
    if Mp != M:
        a2 = jnp.pad(a2, ((0, Mp - M), (0, 0)))
    Np = -(-N // LANE) * LANE
    wb = w.astype(BF16)
    if Np != N:
        wb = jnp.pad(wb, ((0, 0), (0, Np - N)))
    tm = _pick(Mp, (512, 256, 128, 64, 32, 16, 8))
    tn = _pick(Np, (1024, 512, 256, 128))
    out = pl.pallas_call(
        _mm_kernel, name="mm",
        grid=(Mp // tm, Np // tn),
        in_specs=[pl.BlockSpec((tm, K), lambda i, j: (i, 0)),
                  pl.BlockSpec((K, tn), lambda i, j: (0, j))],
        out_specs=pl.BlockSpec((tm, tn), lambda i, j: (i, j)),
        out_shape=jax.ShapeDtypeStruct((Mp, Np), F32),
        compiler_params=_cparams(("parallel", "arbitrary")),
    )(a2, wb)
    return out[:M, :N].reshape(*lead, N)


def _ffn_kernel(x_ref, sh_ref, sc_ref, gt_ref, nw_ref, wg_ref, wu_ref, wo_ref, o_ref, h_sc, acc_sc):
    k = pl.program_id(2)
    gb, rb, d = x_ref.shape

    @pl.when(k == 0)
    def _():
        x = x_ref[...]
        y = x * lax.rsqrt(jnp.mean(x * x, axis=-1, keepdims=True) + NORM_EPS) * nw_ref[...]
        h = y * (1.0 + sc_ref[...]) + sh_ref[...]
        h_sc[...] = h.reshape(gb * rb, d).astype(BF16)

    h = h_sc[...]
    g = jnp.dot(h, wg_ref[...], preferred_element_type=F32)
    u = jnp.dot(h, wu_ref[...], preferred_element_type=F32)
    a = (g * jax.nn.sigmoid(g) * u).astype(BF16)
    part = jnp.dot(a, wo_ref[...], preferred_element_type=F32)

    @pl.when(k == 0)
    def _():
        acc_sc[...] = part

    @pl.when(k > 0)
    def _():
        acc_sc[...] += part

    @pl.when(k == pl.num_programs(2) - 1)
    def _():
        o_ref[...] = x_ref[...] + 0.5 * gt_ref[...] * acc_sc[...].reshape(gb, rb, d)


def ffn(x, shift, scale, gate, nw, w_in_b, w_out_b, gb, rb):
    G, R, D = x.shape
    tf = D_FF // 2
    nk = D_FF // tf
    xspec = pl.BlockSpec((gb, rb, D), lambda g, r, k: (g, r, 0))
    mspec = pl.BlockSpec((gb, 1, D), lambda g, r, k: (g, 0, 0))
    return pl.pallas_call(
        _ffn_kernel, name="ffn",
        grid=(G // gb, R // rb, nk),
        in_specs=[xspec, mspec, mspec, mspec,
                  pl.BlockSpec((1, 1, D), lambda g, r, k: (0, 0, 0)),
                  pl.BlockSpec((D, tf), lambda g, r, k: (0, k)),
                  pl.BlockSpec((D, tf), lambda g, r, k: (0, k + nk)),
                  pl.BlockSpec((tf, D), lambda g, r, k: (k, 0))],
        out_specs=xspec,
        out_shape=jax.ShapeDtypeStruct(x.shape, F32),
        scratch_shapes=[pltpu.VMEM((gb * rb, D), BF16), pltpu.VMEM((gb * rb, D), F32)],
        compiler_params=_cparams(("parallel", "parallel", "arbitrary")),
    )(x, shift, scale, gate, nw.reshape(1, 1, D), w_in_b, w_in_b, w_out_b)


HI = lax.Precision.HIGHEST
NN = ((1,), (0,))
NT = ((1,), (1,))
TN = ((0,), (0,))
NEG = -1e30


def _dot(a, b, dims=NN, hi=False):
    if hi:
        return lax.dot_general(a, b, (dims, ((), ())), precision=HI, preferred_element_type=F32)
    return lax.dot_general(a.astype(BF16), b.astype(BF16), (dims, ((), ())), preferred_element_type=F32)


def _split(a):
    hi = a.astype(BF16)
    return hi, (a - hi.astype(F32)).astype(BF16)


def _dot3s(sa, sb, dims=NN):
    d = lambda x, y: lax.dot_general(x, y, (dims, ((), ())), preferred_element_type=F32)
    return d(sa[0], sb[0]) + d(sa[0], sb[1]) + d(sa[1], sb[0])


def _dot3(a, b, dims=NN):
    return _dot3s(_split(a), _split(b), dims)


def _modnorm_kernel(x_ref, sh_ref, sc_ref, nw_ref, o_ref):
    x = x_ref[...]
    y = x * lax.rsqrt(jnp.mean(x * x, axis=-1, keepdims=True) + NORM_EPS) * nw_ref[...]
    o_ref[...] = (y * (1.0 + sc_ref[...]) + sh_ref[...]).astype(o_ref.dtype)


def modnorm(x, shift, scale, nw, gb, rb, dtype):
    G, R, D = x.shape
    xspec = pl.BlockSpec((gb, rb, D), lambda g, r: (g, r, 0))
    mspec = pl.BlockSpec((gb, 1, D), lambda g, r: (g, 0, 0))
    return pl.pallas_call(
        _modnorm_kernel, name="modnorm",
        grid=(G // gb, R // rb),
        in_specs=[xspec, mspec, mspec, pl.BlockSpec((1, 1, D), lambda g, r: (0, 0, 0))],
        out_specs=xspec,
        out_shape=jax.ShapeDtypeStruct(x.shape, dtype),
        compiler_params=_cparams(("parallel", "parallel")),
    )(x, shift, scale, nw.reshape(1, 1, D))


def _oproj_kernel(*refs, has_mul):
    if has_mul:
        y_ref, mul_ref, w_ref, x_ref, gt_ref, o_ref = refs
    else:
        y_ref, w_ref, x_ref, gt_ref, o_ref = refs
    gb, rb, kd = y_ref.shape
    y = y_ref[...]
    if has_mul:
        y = y * mul_ref[...]
    acc = jnp.dot(y.reshape(gb * rb, kd).astype(BF16), w_ref[...], preferred_element_type=F32)
    o_ref[...] = x_ref[...] + gt_ref[...] * acc.reshape(gb, rb, acc.shape[-1])


def oproj(y, mul, w_b, x, gate, gb, rb):
    G, R, Kd = y.shape
    D = x.shape[-1]
    yspec = pl.BlockSpec((gb, rb, Kd), lambda g, r: (g, r, 0))
    xspec = pl.BlockSpec((gb, rb, D), lambda g, r: (g, r, 0))
    ins = [y] + ([mul] if mul is not None else []) + [w_b, x, gate]
    specs = [yspec] + ([yspec] if mul is not None else []) + [
        pl.BlockSpec((Kd, D), lambda g, r: (0, 0)), xspec,
        pl.BlockSpec((gb, 1, D), lambda g, r: (g, 0, 0))]
    return pl.pallas_call(
        functools.partial(_oproj_kernel, has_mul=mul is not None), name="oproj",
        grid=(G // gb, R // rb),
        in_specs=specs,
        out_specs=xspec,
        out_shape=jax.ShapeDtypeStruct(x.shape, F32),
        compiler_params=_cparams(("parallel", "parallel")),
    )(*ins)


def _qkpost_kernel(q_ref, k_ref, cos_ref, sin_ref, qw_ref, kw_ref, qo_ref, ko_ref):
    gb, rb, w = q_ref.shape
    n = gb * rb
    lane = lax.broadcasted_iota(jnp.int32, (n, w), 1)
    lo = lane < MB_HEADDIM
    first = (lane % MB_HEADDIM) < (ROPE_DIM // 2)
    cos = jnp.broadcast_to(cos_ref[...][None], (gb, rb, w)).reshape(n, w)
    sin = jnp.broadcast_to(sin_ref[...][None], (gb, rb, w)).reshape(n, w)

    def norm_rope(x, wt):
        x2 = x * x
        s_lo = jnp.sum(jnp.where(lo, x2, 0.0), axis=-1, keepdims=True)
        s_hi = jnp.sum(jnp.where(lo, 0.0, x2), axis=-1, keepdims=True)
        ms = jnp.where(lo, s_lo, s_hi) * (1.0 / MB_HEADDIM)
        y = x * lax.rsqrt(ms + NORM_EPS) * wt
        up = pltpu.roll(y, w - ROPE_DIM // 2, 1)
        dn = pltpu.roll(y, ROPE_DIM // 2, 1)
        return y * cos + jnp.where(first, up, dn) * sin

    qo_ref[...] = norm_rope(q_ref[...].reshape(n, w), qw_ref[...]).reshape(gb, rb, w)
    ko_ref[...] = norm_rope(k_ref[...].reshape(n, w), kw_ref[...]).reshape(gb, rb, w)


def _rope_tables(pos):
    half = ROPE_DIM // 2
    inv = ROPE_THETA ** (-jnp.arange(half, dtype=F32) / half)
    ang = pos.astype(F32)[:, None] * inv
    c, s = jnp.cos(ang), jnp.sin(ang)
    ones = jnp.ones((pos.shape[0], MB_HEADDIM - ROPE_DIM), F32)
    ch = jnp.concatenate([c, c, ones], axis=-1)
    sh = jnp.concatenate([-s, s, 0.0 * ones], axis=-1)
    return jnp.concatenate([ch, ch], axis=-1), jnp.concatenate([sh, sh], axis=-1)


def qkpost(qkv, pos, q_norm, k_norm, gb, rb):
    G, R, _ = qkv.shape
    D = D_MODEL
    npair = D // LANE
    cos, sin = _rope_tables(pos)
    tspec = pl.BlockSpec((rb, LANE), lambda g, r, p: (r, 0))
    wspec = pl.BlockSpec((1, LANE), lambda g, r, p: (0, 0))
    ospec = pl.BlockSpec((gb, rb, LANE), lambda g, r, p: (g, r, p))
    return pl.pallas_call(
        _qkpost_kernel, name="qkpost",
        grid=(G // gb, R // rb, npair),
        in_specs=[pl.BlockSpec((gb, rb, LANE), lambda g, r, p: (g, r, p)),
                  pl.BlockSpec((gb, rb, LANE), lambda g, r, p: (g, r, p + npair)),
                  tspec, tspec, wspec, wspec],
        out_specs=[ospec, ospec],
        out_shape=[jax.ShapeDtypeStruct((G, R, D), F32)] * 2,
        compiler_params=_cparams(("parallel", "parallel", "parallel")),
    )(qkv, qkv, cos, sin, jnp.tile(q_norm, 2).reshape(1, LANE), jnp.tile(k_norm, 2).reshape(1, LANE))


def _top3_mask0(g, idx, nb):
    sel = jnp.zeros(g.shape, jnp.bool_)
    for _ in range(MB_TOPK):
        mx = jnp.max(g, axis=0, keepdims=True)
        first = jnp.min(jnp.where(g == mx, idx, nb), axis=0, keepdims=True)
        pick = idx == first
        sel = jnp.logical_or(sel, pick)
        g = jnp.where(pick, -jnp.inf, g)
    return sel


def _moba_prompt_t_kernel(q_ref, k_ref, v_ref, o_ref, kmean_sc, kb_sc, vt_sc, selt_sc):
    c = pl.program_id(2)
    T = MB_BLOCK
    P = MB_HEADDIM
    nb = k_ref.shape[1] // T
    scale = MB_HEADDIM ** -0.5

    @pl.when(c == 0)
    def _():
        for n in range(nb):
            kn = k_ref[0, n * T:(n + 1) * T, :]
            kmean_sc[n:n + 1, :] = jnp.mean(kn, axis=0, keepdims=True)
            kb_sc[n] = kn.astype(BF16)
            vt_sc[n] = v_ref[0, n * T:(n + 1) * T, :].T.astype(BF16)

    q = q_ref[0]
    lo = lax.broadcasted_iota(jnp.int32, (T, LANE), 1) < P
    qh = (jnp.where(lo, q, 0.0), jnp.where(lo, 0.0, q))
    blk = lax.broadcasted_iota(jnp.int32, (nb, T), 0)
    for hh in range(2):
        g = _dot(kmean_sc[...], qh[hh], NT, hi=True)
        g = jnp.where(blk < c, g, -jnp.inf)
        selt_sc[hh] = jnp.logical_and(_top3_mask0(g, blk, nb), blk < c).astype(F32)
    qb = tuple((x * scale).astype(BF16) for x in qh)
    causal = lax.broadcasted_iota(jnp.int32, (T, T), 0) <= lax.broadcasted_iota(jnp.int32, (T, T), 1)
    kc, vc = kb_sc[c], vt_sc[c]
    init = []
    for hh in range(2):
        s = jnp.where(causal, _dot(kc, qb[hh], NT), NEG)
        m = jnp.max(s, axis=0, keepdims=True)
        p = jnp.exp(s - m)
        init += [m, jnp.sum(p, axis=0, keepdims=True), _dot(vc[hh * P:(hh + 1) * P, :], p)]

    def scores(i):
        return tuple(_dot(kb_sc[jnp.minimum(2 * i + d, nb - 1)], qb[hh], NT) for hh in range(2) for d in range(2))

    def body(i, carry):
        state, s_cur = carry
        s_next = scores(i + 1)
        ns = (2 * i, 2 * i + 1)
        out, stage = [], []
        for hh in range(2):
            m, l, acc = state[3 * hh:3 * hh + 3]
            ss = [jnp.where(selt_sc[hh, pl.ds(n, 1), :] > 0.5, s_cur[2 * hh + d], NEG) for d, n in enumerate(ns)]
            m_new = jnp.maximum(m, jnp.max(jnp.maximum(ss[0], ss[1]), axis=0, keepdims=True))
            alpha = jnp.exp(m - m_new)
            ps = [jnp.exp(s - m_new) for s in ss]
            stage.append((m_new, alpha * l + jnp.sum(ps[0] + ps[1], axis=0, keepdims=True), alpha * acc, ps))
        for hh, (m_new, l, acc, ps) in enumerate(stage):
            acc = acc + sum(_dot(vt_sc[n, hh * P:(hh + 1) * P, :], p) for n, p in zip(ns, ps))
            out += [m_new, l, acc]
        return tuple(out), s_next

    res, _ = lax.fori_loop(0, (c + 1) // 2, body, (tuple(init), scores(0)))
    o_ref[0] = jnp.concatenate([res[2] / res[1], res[5] / res[4]], axis=0).T


def moba_prompt_attend_t(q, k, v):
    B, L, D = q.shape
    nb = L // MB_BLOCK
    kvspec = pl.BlockSpec((1, L, LANE), lambda b, p, c: (b, 0, p))
    qspec = pl.BlockSpec((1, MB_BLOCK, LANE), lambda b, p, c: (b, c, p))
    return pl.pallas_call(
        _moba_prompt_t_kernel, name="moba_prompt",
        grid=(B, D // LANE, nb),
        in_specs=[qspec, kvspec, kvspec],
        out_specs=qspec,
        out_shape=jax.ShapeDtypeStruct((B, L, D), F32),
        scratch_shapes=[pltpu.VMEM((nb, LANE), F32), pltpu.VMEM((nb, MB_BLOCK, LANE), BF16),
                        pltpu.VMEM((nb, LANE, MB_BLOCK), BF16), pltpu.VMEM((2, nb, MB_BLOCK), F32)],
        compiler_params=_cparams(("parallel", "parallel", "arbitrary")),
    )(q, k, v)


def _moba_sample_kernel(pt_ref, q_ref, kn_ref, vn_ref, k0_ref, k1_ref, v0_ref, v1_ref, o_ref,
                        qbd_sc, qbdb_sc, kmean_sc, o_sc, m_sc, l_sc):
    del pt_ref
    n = pl.program_id(1)
    nblk = pl.num_programs(1)
    lq, d = q_ref.shape[1], q_ref.shape[2]
    rows = MB_HEADS * lq
    scale = MB_HEADDIM ** -0.5
    own = (lax.broadcasted_iota(jnp.int32, (MB_HEADS, lq, d), 0)
           == lax.broadcasted_iota(jnp.int32, (MB_HEADS, lq, d), 2) // MB_HEADDIM)

    @pl.when(n == 0)
    def _():
        qbd = jnp.where(own, q_ref[0][None], 0.0).reshape(rows, d)
        qbd_sc[...] = qbd
        qbdb_sc[...] = (qbd * scale).astype(BF16)

    qb = qbdb_sc[...]
    k0, k1 = _split(k0_ref[0]), _split(k1_ref[0])
    s0 = _dot(qb, k0[0])
    s1 = _dot(qb, k1[0])
    m = jnp.maximum(jnp.max(s0, axis=-1, keepdims=True), jnp.max(s1, axis=-1, keepdims=True))
    p0 = jnp.exp(s0 - m)
    p1 = jnp.exp(s1 - m)
    o_sc[n] = _dot(p0, v0_ref[0], NT) + _dot(p1, v1_ref[0], NT)
    m_sc[n] = m
    l_sc[n] = jnp.sum(p0, axis=-1, keepdims=True) + jnp.sum(p1, axis=-1, keepdims=True)
    ones = jnp.ones((8, k0_ref.shape[2]), BF16)
    d1 = lambda x: lax.dot_general(ones, x, (NT, ((), ())), preferred_element_type=F32)
    ksum = d1(k0[0]) + d1(k0[1]) + d1(k1[0]) + d1(k1[1])
    kmean_sc[n] = ksum[0:1] * (1.0 / MB_BLOCK)

    @pl.when(n == nblk - 1)
    def _():
        qbd = qbd_sc[...]
        nblocks = o_sc.shape[0]
        s = lax.dot_general(qbd * scale, kn_ref[0], (NT, ((), ())), preferred_element_type=F32)
        qi = lax.broadcasted_iota(jnp.int32, (rows, lq), 0) % lq
        s = jnp.where(lax.broadcasted_iota(jnp.int32, (rows, lq), 1) <= qi, s, NEG)
        m_own = jnp.max(s, axis=-1, keepdims=True)
        p = jnp.exp(s - m_own)
        l_own = jnp.sum(p, axis=-1, keepdims=True)
        o_own = jnp.dot(p, vn_ref[0], preferred_element_type=F32)
        g = [jnp.sum(qbd * kmean_sc[j], axis=-1, keepdims=True) for j in range(nblocks)]
        sel = []
        for j in range(nblocks):
            rank = jnp.zeros((rows, 1), jnp.int32)
            for i in range(nblocks):
                if i != j:
                    ahead = (g[i] >= g[j]) if i < j else (g[i] > g[j])
                    rank = rank + ahead.astype(jnp.int32)
            sel.append(rank < MB_TOPK)
        mx = m_own
        for j in range(nblocks):
            mx = jnp.where(sel[j], jnp.maximum(mx, m_sc[j]), mx)
        w_own = jnp.exp(m_own - mx)
        l = l_own * w_own
        o = o_own * w_own
        for j in range(nblocks):
            wj = jnp.where(sel[j], jnp.exp(m_sc[j] - mx), 0.0)
            l = l + wj * l_sc[j]
            o = o + wj * o_sc[j]
        res = (o / l).reshape(MB_HEADS, lq, d)
        o_ref[0] = jnp.sum(jnp.where(own, res, 0.0), axis=0)


def moba_sample_attend(q, k_new, v_new, cache_k, cache_v, page_table):
    Bs, Ls, D = q.shape
    page = cache_k.shape[2]
    assert MB_BLOCK == 2 * page and PAST_LEN % MB_BLOCK == 0 and Ls <= MB_BLOCK
    nblk = PAST_LEN // MB_BLOCK
    assert nblk >= MB_TOPK
    rows = MB_HEADS * Ls
    newspec = pl.BlockSpec((1, Ls, D), lambda b, n, pt: (b, 0, 0))
    pg0 = pl.BlockSpec((1, D, page), lambda b, n, pt: (pt[b, 2 * n], 0, 0))
    pg1 = pl.BlockSpec((1, D, page), lambda b, n, pt: (pt[b, 2 * n + 1], 0, 0))
    return pl.pallas_call(
        _moba_sample_kernel, name="moba_sample",
        grid_spec=pltpu.PrefetchScalarGridSpec(
            num_scalar_prefetch=1,
            grid=(Bs, nblk),
            in_specs=[newspec, newspec, newspec, pg0, pg1, pg0, pg1],
            out_specs=newspec,
            scratch_shapes=[pltpu.VMEM((rows, D), F32), pltpu.VMEM((rows, D), BF16),
                            pltpu.VMEM((nblk, 1, D), F32), pltpu.VMEM((nblk, rows, D), F32),
                            pltpu.VMEM((nblk, rows, 1), F32), pltpu.VMEM((nblk, rows, 1), F32)]),
        out_shape=jax.ShapeDtypeStruct((Bs, Ls, D), F32),
        compiler_params=_cparams(("parallel", "arbitrary")),
    )(page_table, q, k_new, v_new, cache_k, cache_k, cache_v, cache_v)


def _softplus(z):
    return jnp.maximum(z, 0.0) + jnp.log1p(jnp.exp(-jnp.abs(z)))


def _sigmoid(z):
    return 1.0 / (1.0 + jnp.exp(-z))


def _rwkv_pre_kernel(x_ref, halo_ref, sb_ref, sh_ref, sc_ref, nw_ref, mu_ref, w0_ref, a0_ref,
                     wr_ref, wk_ref, wv_ref, w1_ref, w2_ref, a1_ref, a2_ref, g1_ref, g2_ref,
                     r_o, lw_o, k_o, v_o, a_o, g_o, hl_o, hbuf):
    rr = pl.program_id(1)
    gb, rb, d = x_ref.shape
    n = gb * rb

    def hn(x):
        y = x * lax.rsqrt(jnp.mean(x * x, axis=-1, keepdims=True) + NORM_EPS) * nw_ref[...]
        return y * (1.0 + sc_ref[...]) + sh_ref[...]

    h = hn(x_ref[...])
    prev_row = jnp.where(rr == 0, sb_ref[...], hn(halo_ref[...])[:, 7:8, :])
    hbuf[:, 8:, :] = h
    hbuf[:, 7:8, :] = prev_row
    h2 = h.reshape(n, d)
    xx = hbuf[:, 7:7 + rb, :].reshape(n, d) - h2
    mix = lambda i: h2 + xx * mu_ref[i:i + 1, :]
    dot = lambda a, w_ref: jnp.dot(a.astype(BF16), w_ref[...], preferred_element_type=F32)
    out = lambda o_ref, val: o_ref.__setitem__(Ellipsis, val.reshape(gb, rb, d))
    out(r_o, dot(mix(0), wr_ref))
    out(k_o, dot(mix(2), wk_ref))
    out(v_o, dot(mix(3), wv_ref))
    w_raw = w0_ref[...] + dot(jnp.tanh(dot(mix(1), w1_ref)), w2_ref)
    out(lw_o, -jnp.exp(-_softplus(-w_raw) - 0.5))
    out(a_o, _sigmoid(a0_ref[...] + dot(dot(mix(4), a1_ref), a2_ref)))
    out(g_o, dot(_sigmoid(dot(mix(5), g1_ref)), g2_ref))
    hl_o[...] = h[:, rb - 1:rb, :]


def rwkv_pre(x, shift_buf, shift, scale, nw, mu, w0, a0, ws, gb, rb):
    G, R, D = x.shape
    assert gb == 1 or rb == R
    xspec = pl.BlockSpec((gb, rb, D), lambda g, r: (g, r, 0))
    halo = pl.BlockSpec((gb, 8, D), lambda g, r: (g, jnp.maximum(r * (rb // 8) - 1, 0), 0))
    mspec = pl.BlockSpec((gb, 1, D), lambda g, r: (g, 0, 0))
    vec = pl.BlockSpec((1, D), lambda g, r: (0, 0))
    full = lambda w: pl.BlockSpec(w.shape, lambda g, r: (0, 0))
    mu8 = jnp.pad(mu, ((0, 8 - mu.shape[0]), (0, 0)))
    big = jax.ShapeDtypeStruct((G, R, D), F32)
    return pl.pallas_call(
        _rwkv_pre_kernel, name="rwkv_pre",
        grid=(G // gb, R // rb),
        in_specs=[xspec, halo, mspec, mspec, mspec, vec, full(mu8), vec, vec] + [full(w) for w in ws],
        out_specs=[xspec] * 6 + [mspec],
        out_shape=[big] * 6 + [jax.ShapeDtypeStruct((G, 1, D), F32)],
        scratch_shapes=[pltpu.VMEM((gb, rb + 8, D), F32)],
        compiler_params=_cparams(("parallel", "arbitrary")),
    )(x, x, shift_buf.reshape(G, 1, D), shift, scale, nw.reshape(1, D), mu8, w0.reshape(1, D),
      a0.reshape(1, D), *ws)


def _rwkv_scan_kernel(r_ref, lw_ref, k_ref, v_ref, a_ref, kk_ref, ka_ref, rk_ref, lnw_ref, lnb_ref, s0_ref,
                      y_ref, so_ref, s_sc):
    c = pl.program_id(2)
    C = r_ref.shape[1]
    N = RW_HEADDIM

    @pl.when(c == 0)
    def _():
        s_sc[...] = s0_ref[...]

    hg = s_sc.shape[1]
    rows = hg * C
    s_in = [[s_sc[bi, h] for h in range(hg)] for bi in range(s_sc.shape[0])]
    s_out = []
    def chain(bi):
        stack = lambda ref: jnp.concatenate([ref[bi][:, h * N:(h + 1) * N] for h in range(hg)], axis=0)
        pstack = lambda ref: jnp.concatenate(
            [jnp.broadcast_to(ref[:, h * N:(h + 1) * N], (C, N)) for h in range(hg)], axis=0)
        r, lw, kraw, v, a = (stack(ref) for ref in (r_ref, lw_ref, k_ref, v_ref, a_ref))
        kkr = kraw * pstack(kk_ref)
        kk = kkr / jnp.maximum(jnp.sqrt(jnp.sum(kkr * kkr, axis=-1, keepdims=True)), 1e-12)
        k = kraw * (1.0 + (a - 1.0) * pstack(ka_ref))
        tpos = lax.broadcasted_iota(jnp.int32, (rows, N), 0) % C
        cum, sh = lw, 1
        while sh < C:
            cum = cum + jnp.where(tpos >= sh, pltpu.roll(cum, sh, 0), 0.0)
            sh *= 2
        e_in, e_ex, e_neg = jnp.exp(cum), jnp.exp(cum - lw), jnp.exp(-cum)
        ab, bb, kb, rb_ = -kk * e_ex, kk * a * e_neg, k * e_neg, r * e_in
        ri = lax.broadcasted_iota(jnp.int32, (rows, rows), 0)
        ci = lax.broadcasted_iota(jnp.int32, (rows, rows), 1)
        same = (ri // C) == (ci // C)
        strict, incl = jnp.logical_and(same, ci < ri), jnp.logical_and(same, ci <= ri)
        sab, sbb, skb, srb = _split(ab), _split(bb), _split(kb), _split(rb_)
        d1 = lambda x, y, dims=NN: lax.dot_general(x, y, (dims, ((), ())), preferred_element_type=F32)
        lab = jnp.where(strict, d1(sab[0], sbb[0], NT), 0.0)
        lak = jnp.where(strict, d1(sab[0], skb[0], NT), 0.0)
        mrb = jnp.where(incl, d1(srb[0], sbb[0], NT), 0.0)
        mrk = jnp.where(incl, d1(srb[0], skb[0], NT), 0.0)
        tinv = (ri == ci).astype(F32) + lab
        spw = _split(lab)
        yield
        for _ in range(int(math.log2(C)) - 1):
            spw = _split(_dot3s(spw, spw))
            yield
            tinv = tinv + _dot3s(_split(tinv), spw)
            yield
        hsl = lambda t, h: t[h * C:(h + 1) * C]
        xs = [_dot3(jnp.concatenate([hsl(ab, h), hsl(rb_, h)], axis=0), s_in[bi][h], NT) for h in range(hg)]
        ab_s = jnp.concatenate([x[:C] for x in xs], axis=0)
        rb_s = jnp.concatenate([x[C:] for x in xs], axis=0)
        yield
        u = _dot3(tinv, ab_s + _dot3(lak, v))
        yield
        y = rb_s + _dot3(jnp.concatenate([mrb, mrk], axis=1), jnp.concatenate([u, v], axis=0))
        yield
        for h in range(hg):
            upd = _dot3(jnp.concatenate([hsl(u, h), hsl(v, h)], axis=0),
                        jnp.concatenate([hsl(bb, h), hsl(kb, h)], axis=0), TN)
            s_out.append((bi, h, (s_in[bi][h] + upd) * e_in[(h + 1) * C - 1:(h + 1) * C, :]))
        mean = jnp.mean(y, axis=-1, keepdims=True)
        var = jnp.mean(jnp.square(y - mean), axis=-1, keepdims=True)
        yn = (y - mean) * lax.rsqrt(var + RW_LNX_EPS) * pstack(lnw_ref) + pstack(lnb_ref)
        out = yn + jnp.sum(r * k * pstack(rk_ref), axis=-1, keepdims=True) * v
        y_ref[bi] = jnp.concatenate([hsl(out, h) for h in range(hg)], axis=-1)
    chains = [chain(bi) for bi in range(s_sc.shape[0])]
    while chains:
        chains = [g for g in chains if next(g, True) is None]
    for bi, h, val in s_out:
        s_sc[bi, h] = val

    @pl.when(c == pl.num_programs(2) - 1)
    def _():
        so_ref[...] = s_sc[...]


def rwkv_scan(r, lw, k, v, a, k_k, k_a, r_k, lnx_w, lnx_b, s0, C, hg, nbk):
    B, L, D = r.shape
    N = RW_HEADDIM
    assert L % C == 0 and C & (C - 1) == 0 and RW_HEADS % hg == 0 and (hg * N) % LANE == 0 and B % nbk == 0
    tspec = pl.BlockSpec((nbk, C, hg * N), lambda b, p, c: (b, c, p))
    pspec = pl.BlockSpec((1, hg * N), lambda b, p, c: (0, p))
    sspec = pl.BlockSpec((nbk, hg, N, N), lambda b, p, c: (b, p, 0, 0))
    vec = lambda t: t.reshape(1, D)
    return pl.pallas_call(
        _rwkv_scan_kernel, name="rwkv_scan",
        grid=(B // nbk, RW_HEADS // hg, L // C),
        in_specs=[tspec] * 5 + [pspec] * 5 + [sspec],
        out_specs=[tspec, sspec],
        out_shape=[jax.ShapeDtypeStruct((B, L, D), F32), jax.ShapeDtypeStruct(s0.shape, F32)],
        scratch_shapes=[pltpu.VMEM((nbk, hg, N, N), F32)],
        compiler_params=_cparams(("parallel", "parallel", "arbitrary")),
    )(r, lw, k, v, a, vec(k_k), vec(k_a), vec(r_k), vec(lnx_w), vec(lnx_b), s0)


def rwkv_layer(x, shift, scale, gate, nw, shift_buf, wkv0, mu, w_r, w_k, w_v, w0, w1, w2, a0, a1, a2, g1, g2,
               k_k, k_a, r_k, lnx_w, lnx_b, w_o, gb, rb, C, hg, nbk):
    ws = [w.astype(BF16) for w in (w_r, w_k, w_v, w1, w2, a1, a2, g1, g2)]
    r, lw, k, v, a, g, hl = rwkv_pre(x, shift_buf, shift, scale, nw, mu, w0, a0, ws, gb, rb)
    y, s_fin = rwkv_scan(r, lw, k, v, a, k_k, k_a, r_k, lnx_w, lnx_b, wkv0, C, hg, nbk)
    return oproj(y, g, w_o.astype(BF16), x, gate, gb, rb), hl[:, 0], s_fin


def _conv_kernel(x_ref, halo_ref, buf_ref, w_ref, b_ref, o_ref, sc):
    rr = pl.program_id(1)
    gb, rb, cw = x_ref.shape
    sc[:, :8, :] = jnp.where(rr == 0, buf_ref[...], halo_ref[...])
    sc[:, 8:, :] = x_ref[...]
    y = b_ref[...]
    for t in range(SSD_CONV):
        y = y + sc[:, 8 - (SSD_CONV - 1) + t:8 - (SSD_CONV - 1) + t + rb, :] * w_ref[t:t + 1, :]
    o_ref[...] = y * _sigmoid(y)


def ssd_conv(zx, buf, w, b, gb, rb, cw=512):
    G, R, _ = zx.shape
    assert gb == 1 or rb == R
    off = SSD_D_INNER // cw
    buf8 = jnp.pad(buf, ((0, 0), (8 - (SSD_CONV - 1), 0), (0, 0)))
    w8 = jnp.pad(w, ((0, 8 - SSD_CONV), (0, 0)))
    return pl.pallas_call(
        _conv_kernel, name="ssd_conv",
        grid=(G // gb, R // rb, SSD_CONV_DIM // cw),
        in_specs=[pl.BlockSpec((gb, rb, cw), lambda g, r, j: (g, r, off + j)),
                  pl.BlockSpec((gb, 8, cw), lambda g, r, j: (g, jnp.maximum(r * (rb // 8) - 1, 0), off + j)),
                  pl.BlockSpec((gb, 8, cw), lambda g, r, j: (g, 0, j)),
                  pl.BlockSpec((8, cw), lambda g, r, j: (0, j)),
                  pl.BlockSpec((1, cw), lambda g, r, j: (0, j))],
        out_specs=pl.BlockSpec((gb, rb, cw), lambda g, r, j: (g, r, j)),
        out_shape=jax.ShapeDtypeStruct((G, R, SSD_CONV_DIM), F32),
        scratch_shapes=[pltpu.VMEM((gb, rb + 8, cw), F32)],
        compiler_params=_cparams(("parallel", "arbitrary", "arbitrary")),
    )(zx, zx, buf8, w8, b.reshape(1, SSD_CONV_DIM))


def _col2row(col, eye):
    return jnp.sum(jnp.where(eye, col, 0.0), axis=0, keepdims=True)


def _ssd_scan_kernel(x_ref, bm_ref, cm_ref, z_ref, dt_ref, dtb_ref, alog_ref, dsk_ref, nw_ref, s0_ref,
                     y_ref, so_ref, s_sc):
    c = pl.program_id(2)
    Q = x_ref.shape[1]
    P, N, R = SSD_HEADDIM, SSD_STATE, SSD_HPG

    @pl.when(c == 0)
    def _():
        s_sc[...] = s0_ref[0]

    ti = lax.broadcasted_iota(jnp.int32, (Q, Q), 0)
    si = lax.broadcasted_iota(jnp.int32, (Q, Q), 1)
    causal, eye = si <= ti, si == ti
    lo = lax.broadcasted_iota(jnp.int32, (Q, LANE), 1) < P
    dt = _softplus(dt_ref[0] + dtb_ref[...])
    cs = _dot(causal.astype(F32), dt * (-jnp.exp(alog_ref[...])), hi=True)
    x, bm, cm = x_ref[0], bm_ref[0], cm_ref[0]
    cb = _dot(cm, bm, NT)
    s_all = s_sc[...].reshape(R * P, N)
    ys = _dot(cm, s_all, NT)
    y_pairs, xw_pairs, e_last = [], [], []
    for j in range(R // 2):
        xp = x[:, j * LANE:(j + 1) * LANE]
        yh, grow, wend = [], [], []
        for r in (2 * j, 2 * j + 1):
            cs_col, dt_col = cs[:, r:r + 1], dt[:, r:r + 1]
            cs_last = cs[Q - 1:Q, r:r + 1]
            dec = jnp.exp(jnp.where(causal, cs_col - _col2row(cs_col, eye), -jnp.inf))
            yh.append(_dot(cb * dec * _col2row(dt_col, eye), xp))
            grow.append(jnp.exp(cs_col))
            wend.append(jnp.exp(cs_last - cs_col) * dt_col)
            e_last.append(jnp.exp(cs_last))
        yp = jnp.where(lo, yh[0], yh[1]) + ys[:, j * LANE:(j + 1) * LANE] * jnp.where(lo, grow[0], grow[1])
        y_pairs.append(yp + xp * dsk_ref[:, j * LANE:(j + 1) * LANE])
        xw_pairs.append(xp * jnp.where(lo, wend[0], wend[1]))
    upd = _dot(jnp.concatenate(xw_pairs, axis=-1), bm, TN)
    for r in range(R):
        s_sc[r] = s_sc[r] * e_last[r] + upd[r * P:(r + 1) * P, :]
    z = z_ref[0]
    yz = jnp.concatenate(y_pairs, axis=-1) * (z * _sigmoid(z))
    y_ref[0] = yz * lax.rsqrt(jnp.mean(yz * yz, axis=-1, keepdims=True) + NORM_EPS) * nw_ref[...]

    @pl.when(c == pl.num_programs(2) - 1)
    def _():
        so_ref[0] = s_sc[...]


def ssd_chunk_scan(xbc, zx, dt_raw, dt_bias, a_log, d_skip, norm_w, s0, Q):
    B, L, _ = xbc.shape
    G, R, P, N = SSD_GROUPS, SSD_HPG, SSD_HEADDIM, SSD_STATE
    gw = R * P
    assert L % Q == 0 and N == LANE
    pad_heads = lambda t: jnp.pad(t.reshape(G, R), ((0, 0), (0, LANE - R))).reshape(1, G * LANE)
    xs = pl.BlockSpec((1, Q, gw), lambda b, g, c: (b, c, g))
    bs = pl.BlockSpec((1, Q, N), lambda b, g, c: (b, c, SSD_D_INNER // N + g))
    cs = pl.BlockSpec((1, Q, N), lambda b, g, c: (b, c, SSD_D_INNER // N + G + g))
    ds = pl.BlockSpec((1, Q, LANE), lambda b, g, c: (b, c, g))
    hs = pl.BlockSpec((1, LANE), lambda b, g, c: (0, g))
    ws = pl.BlockSpec((1, gw), lambda b, g, c: (0, g))
    ss = pl.BlockSpec((1, R, P, N), lambda b, g, c: (b, g, 0, 0))
    return pl.pallas_call(
        _ssd_scan_kernel, name="ssd_scan",
        grid=(B, G, L // Q),
        in_specs=[xs, bs, cs, xs, ds, hs, hs, ws, ws, ss],
        out_specs=[xs, ss],
        out_shape=[jax.ShapeDtypeStruct((B, L, SSD_D_INNER), F32), jax.ShapeDtypeStruct(s0.shape, F32)],
        scratch_shapes=[pltpu.VMEM((R, P, N), F32)],
        compiler_params=_cparams(("parallel", "parallel", "arbitrary")),
    )(xbc, xbc, xbc, zx, dt_raw, pad_heads(dt_bias), pad_heads(a_log),
      jnp.repeat(d_skip, P).reshape(1, SSD_D_INNER), norm_w.reshape(1, SSD_D_INNER), s0)


def ssd_layer(x, shift, scale, gate, nw, conv_buf, ssm0, w_in, conv_w, conv_b, dt_bias, a_log, d_skip, norm_w,
              w_out, gb, rb, Q):
    G, R, D = x.shape
    nzx = SSD_D_INNER + SSD_CONV_DIM
    h = modnorm(x, shift, scale, nw, gb, rb, BF16)
    zx = mm(h, w_in[:, :nzx])
    w_dt = jnp.pad(w_in[:, nzx:].reshape(D, SSD_GROUPS, SSD_HPG), ((0, 0), (0, 0), (0, LANE - SSD_HPG)))
    dt_raw = mm(h, w_dt.reshape(D, SSD_GROUPS * LANE))
    xbc = ssd_conv(zx, conv_buf, conv_w, conv_b, gb, rb)
    y, s_fin = ssd_chunk_scan(xbc, zx, dt_raw, dt_bias, a_log, d_skip, norm_w, ssm0, Q)
    new_conv = zx[:, R - (SSD_CONV - 1):, SSD_D_INNER:]
    return oproj(y, None, w_out.astype(BF16), x, gate, gb, rb), new_conv, s_fin


def _mlstm_kernel(q_ref, k_ref, v_ref, o_ref, g_ref, bif_ref, nw_ref, c0_ref, n0_ref, m0_ref,
                  y_ref, co_ref, no_ref, mo_ref, c_sc, n_sc, m_sc):
    c = pl.program_id(2)
    Q = q_ref.shape[1]
    DK, DV = ML_QK_DIM, ML_V_DIM

    @pl.when(c == 0)
    def _():
        c_sc[...] = c0_ref[0]
        n_sc[...] = n0_ref[0, 0]
        m_sc[...] = m0_ref[0, 0]

    ti = lax.broadcasted_iota(jnp.int32, (Q, Q), 0)
    si = lax.broadcasted_iota(jnp.int32, (Q, Q), 1)
    causal, eye = si <= ti, si == ti
    gates = g_ref[0] + bif_ref[...]
    bcum = _dot(causal.astype(F32), -_softplus(-gates), hi=True)
    ys = []
    for hh in range(2):
        i_col, b_col = gates[:, hh:hh + 1], bcum[:, 2 + hh:3 + hh]
        m_prev = m_sc[:, hh:hh + 1]
        q = q_ref[0][:, hh * DK:(hh + 1) * DK] * DK ** -0.5
        k = k_ref[0][:, hh * DK:(hh + 1) * DK]
        v = v_ref[0][:, hh * DV:(hh + 1) * DV]
        cst, nst = c_sc[hh], n_sc[hh:hh + 1, :]
        dmat = jnp.where(causal, b_col - _col2row(b_col, eye) + _col2row(i_col, eye), -jnp.inf)
        m_inter = b_col + m_prev
        m_t = jnp.maximum(m_inter, jnp.max(dmat, axis=-1, keepdims=True))
        w_inter = jnp.exp(m_inter - m_t)
        qk = _dot(q, k, NT) * jnp.exp(dmat - m_t)
        num = _dot(qk, v) + w_inter * _dot(q, cst, NT)
        den = jnp.sum(qk, axis=-1, keepdims=True) + w_inter * jnp.sum(q * nst, axis=-1, keepdims=True)
        hout = num / jnp.maximum(jnp.abs(den), jnp.exp(-m_t))
        m_last, b_last = m_t[Q - 1:Q], b_col[Q - 1:Q]
        w_state = jnp.exp(b_last - b_col + i_col - m_last)
        decay = jnp.exp(b_last + m_prev - m_last)
        c_sc[hh] = decay * cst + _dot(v * w_state, k, TN)
        n_sc[hh:hh + 1, :] = decay * nst + jnp.sum(w_state * k, axis=0, keepdims=True)
        m_sc[:, hh:hh + 1] = m_last
        hn = hout * lax.rsqrt(jnp.mean(hout * hout, axis=-1, keepdims=True) + NORM_EPS)
        ys.append(_sigmoid(o_ref[0][:, hh * DV:(hh + 1) * DV]) * (hn * nw_ref[:, hh * DV:(hh + 1) * DV]))
    y_ref[0] = jnp.concatenate(ys, axis=-1)

    @pl.when(c == pl.num_programs(2) - 1)
    def _():
        co_ref[0] = c_sc[...]
        no_ref[0, 0] = n_sc[...]
        mo_ref[0, 0] = m_sc[...]


def mlstm_scan(proj, gates, b_if, norm_w, c0, n0, m0, Q):
    B, L, _ = proj.shape
    H, DK, DV = ML_HEADS, ML_QK_DIM, ML_V_DIM
    assert L % Q == 0 and 2 * DK == LANE
    npair = H // 2
    qs = pl.BlockSpec((1, Q, LANE), lambda b, j, c: (b, c, j))
    ks = pl.BlockSpec((1, Q, LANE), lambda b, j, c: (b, c, npair + j))
    vs = pl.BlockSpec((1, Q, 2 * DV), lambda b, j, c: (b, c, H * DK // DV + j))
    os_ = pl.BlockSpec((1, Q, 2 * DV), lambda b, j, c: (b, c, H * DK // DV + npair + j))
    ys = pl.BlockSpec((1, Q, 2 * DV), lambda b, j, c: (b, c, j))
    cs = pl.BlockSpec((1, 2, DV, DK), lambda b, j, c: (b, j, 0, 0))
    ns = pl.BlockSpec((1, 1, 2, DK), lambda b, j, c: (b, j, 0, 0))
    ms = pl.BlockSpec((1, 1, 1, 2), lambda b, j, c: (b, j, 0, 0))
    bi, bf = b_if[:H].reshape(npair, 2), b_if[H:].reshape(npair, 2)
    bias = jnp.pad(jnp.concatenate([bi, bf], axis=-1), ((0, 0), (0, LANE - 4))).reshape(1, npair * LANE)
    y, c, n, m = pl.pallas_call(
        _mlstm_kernel, name="mlstm",
        grid=(B, npair, L // Q),
        in_specs=[qs, ks, vs, os_, qs, pl.BlockSpec((1, LANE), lambda b, j, c: (0, j)),
                  pl.BlockSpec((1, 2 * DV), lambda b, j, c: (0, j)), cs, ns, ms],
        out_specs=[ys, cs, ns, ms],
        out_shape=[jax.ShapeDtypeStruct((B, L, H * DV), F32), jax.ShapeDtypeStruct((B, H, DV, DK), F32),
                   jax.ShapeDtypeStruct((B, npair, 2, DK), F32), jax.ShapeDtypeStruct((B, npair, 1, 2), F32)],
        scratch_shapes=[pltpu.VMEM((2, DV, DK), F32), pltpu.VMEM((2, DK), F32), pltpu.VMEM((1, 2), F32)],
        compiler_params=_cparams(("parallel", "parallel", "arbitrary")),
    )(proj, proj, proj, proj, gates, bias, norm_w.reshape(1, H * DV), c0,
      n0.reshape(B, npair, 2, DK), m0.reshape(B, npair, 1, 2))
    return y, c, n.reshape(B, H, DK), m.reshape(B, H)


def mlstm_layer(x, shift, scale, gate, nw, c0, n0, m0, w_in, b_if, norm_w, w_o, gb, rb, Q):
    D = x.shape[-1]
    H = ML_HEADS
    nmain = 2 * H * ML_QK_DIM + 2 * H * ML_V_DIM
    h = modnorm(x, shift, scale, nw, gb, rb, BF16)
    proj = mm(h, w_in[:, :nmain])
    wi, wf = w_in[:, nmain:nmain + H].reshape(D, H // 2, 2), w_in[:, nmain + H:].reshape(D, H // 2, 2)
    wg = jnp.pad(jnp.concatenate([wi, wf], axis=-1), ((0, 0), (0, 0), (0, LANE - 4)))
    gates = mm(h, wg.reshape(D, (H // 2) * LANE))
    y, c, n, m = mlstm_scan(proj, gates, b_if, norm_w, c0, n0, m0, Q)
    return oproj(y, None, w_o.astype(BF16), x, gate, gb, rb), c, n, m


def kernel(x_prompt, x_sample, state_ssd_conv, state_ssd, state_rwkv_shift, state_rwkv, cache_k, cache_v, state_mlstm_c, state_mlstm_n, state_mlstm_m, page_table, c_prompt, c_sample, ada_w, ada_b, norm_w, ffn_w_in, ffn_w_out, ssd_w_in, ssd_conv_w, ssd_conv_b, ssd_dt_bias, ssd_a_log, ssd_d, ssd_norm_w, ssd_w_out, rw_mu, rw_w_r, rw_w_k, rw_w_v, rw_w0, rw_w1, rw_w2, rw_a0, rw_a1, rw_a2, rw_g1, rw_g2, rw_k_k, rw_k_a, rw_r_k, rw_lnx_w, rw_lnx_b, rw_w_o, mb_w_qkv, mb_q_norm, mb_k_norm, mb_w_o, ml_w_in, ml_b_if, ml_norm_w, ml_w_o):
    B, S = x_prompt.shape[0], x_prompt.shape[1]
    Bs, Ls = x_sample.shape[0], x_sample.shape[1]
    D = D_MODEL
    pos_p = jnp.arange(S, dtype=jnp.int32)
    pos_s = PAST_LEN + jnp.arange(Ls, dtype=jnp.int32)
    xp, xs = x_prompt, x_sample
    outs = {}
    c_all = jnp.concatenate([c_prompt, c_sample], axis=0)
    sc_all = jax.nn.silu(c_all)
    for i in range(DEPTH):
        kind, j = i % N_MIXERS, i // N_MIXERS
        mod = (mm(sc_all, ada_w[i]) + ada_b[i]).reshape(B + Bs, N_MOD, 1, D)
        mp, ms = mod[:B], mod[B:]
        wi0, wo0 = ffn_w_in[i, 0].astype(BF16), ffn_w_out[i, 0].astype(BF16)
        wi1, wo1 = ffn_w_in[i, 1].astype(BF16), ffn_w_out[i, 1].astype(BF16)
        xp = ffn(xp, mp[:, 0], mp[:, 1], mp[:, 2], norm_w[i, 0], wi0, wo0, 1, 512)
        xs = ffn(xs, ms[:, 0], ms[:, 1], ms[:, 2], norm_w[i, 0], wi0, wo0, 64, Ls)
        m1p, m1s = (mp[:, 3], mp[:, 4], mp[:, 5], norm_w[i, 1]), (ms[:, 3], ms[:, 4], ms[:, 5], norm_w[i, 1])
        if kind == 0:
            w = (ssd_w_in[j], ssd_conv_w[j], ssd_conv_b[j], ssd_dt_bias[j], ssd_a_log[j], ssd_d[j], ssd_norm_w[j], ssd_w_out[j])
            xp, cp, sp = ssd_layer(xp, *m1p, jnp.zeros((B, SSD_CONV - 1, SSD_CONV_DIM), F32),
                                   jnp.zeros((B, SSD_HEADS, SSD_HEADDIM, SSD_STATE), F32), *w, 1, 512, SSD_CHUNK)
            xs, cs_, ss = ssd_layer(xs, *m1s, state_ssd_conv[j], state_ssd[j], *w, 64, Ls, Ls)
            outs[2], outs[3], outs[4], outs[5] = cp[None], cs_[None], sp[None], ss[None]
        elif kind == 1:
            w = (rw_mu[j], rw_w_r[j], rw_w_k[j], rw_w_v[j], rw_w0[j], rw_w1[j], rw_w2[j], rw_a0[j], rw_a1[j],
                 rw_a2[j], rw_g1[j], rw_g2[j], rw_k_k[j], rw_k_a[j], rw_r_k[j], rw_lnx_w[j], rw_lnx_b[j], rw_w_o[j])
            xp, shp, sp = rwkv_layer(xp, *m1p, jnp.zeros((B, D), F32),
                                     jnp.zeros((B, RW_HEADS, RW_HEADDIM, RW_HEADDIM), F32), *w, 1, 256, 64, 2, B)
            xs, shs, ss = rwkv_layer(xs, *m1s, state_rwkv_shift[j], state_rwkv[j], *w, 32, Ls, Ls, RW_HEADS, 2)
            outs[6], outs[7], outs[8], outs[9] = shp[None], shs[None], sp[None], ss[None]
        elif kind == 2:
            hp = modnorm(xp, mp[:, 3], mp[:, 4], norm_w[i, 1], 1, 512, BF16)
            hs = modnorm(xs, ms[:, 3], ms[:, 4], norm_w[i, 1], 64, Ls, BF16)
            qkv_p, qkv_s = mm(hp, mb_w_qkv[j]), mm(hs, mb_w_qkv[j])
            qp, kp = qkpost(qkv_p, pos_p, mb_q_norm[j], mb_k_norm[j], 1, 512)
            qs, ks_ = qkpost(qkv_s, pos_s, mb_q_norm[j], mb_k_norm[j], 16, Ls)
            vp, vs = qkv_p[..., 2 * D:], qkv_s[..., 2 * D:]
            op = moba_prompt_attend_t(qp, kp, vp)
            nl, npool, page = cache_k.shape[0], cache_k.shape[1], cache_k.shape[2]
            tview = lambda t: jnp.transpose(t, (0, 1, 3, 4, 2)).reshape(nl * npool, D, page)
            os_ = moba_sample_attend(qs, ks_, vs, tview(cache_k), tview(cache_v), page_table + j * npool)
            w_o = mb_w_o[j].astype(BF16)
            xp = oproj(op, None, w_o, xp, mp[:, 5], 1, 512)
            xs = oproj(os_, None, w_o, xs, ms[:, 5], 64, Ls)
            hd = (MB_HEADS, MB_HEADDIM)
            outs[10], outs[11] = kp.reshape(1, B, S, *hd), ks_.reshape(1, Bs, Ls, *hd)
            outs[12], outs[13] = vp.reshape(1, B, S, *hd), vs.reshape(1, Bs, Ls, *hd)
        else:
            w = (ml_w_in[j], ml_b_if[j], ml_norm_w[j], ml_w_o[j])
            xp, cp, np_, mp_ = mlstm_layer(xp, *m1p, jnp.zeros((B, ML_HEADS, ML_V_DIM, ML_QK_DIM), F32),
                                           jnp.zeros((B, ML_HEADS, ML_QK_DIM), F32), jnp.zeros((B, ML_HEADS), F32),
                                           *w, 1, 512, 128)
            xs, cs_, ns_, ms_ = mlstm_layer(xs, *m1s, state_mlstm_c[j], state_mlstm_n[j], state_mlstm_m[j],
                                            *w, 64, Ls, Ls)
            outs[14], outs[15], outs[16], outs[17], outs[18], outs[19] = cp[None], cs_[None], np_[None], ns_[None], mp_[None], ms_[None]
        xp = ffn(xp, mp[:, 6], mp[:, 7], mp[:, 8], norm_w[i, 2], wi1, wo1, 1, 512)
        xs = ffn(xs, ms[:, 6], ms[:, 7], ms[:, 8], norm_w[i, 2], wi1, wo1, 64, Ls)
    outs[0], outs[1] = xp, xs
    return tuple(outs[i] for i in range(20))
```

```python
import functools
import math

import jax
import jax.numpy as jnp
from jax import lax
from jax.experimental import pallas as pl
from jax.experimental.pallas import tpu as pltpu

D_MODEL = 1024
DEPTH = 4
PAST_LEN = 2048
N_MIXERS = 4
NORM_EPS = 1e-6
N_MOD = 9
D_FF = 2816

SSD_D_INNER = 2 * D_MODEL
SSD_HEADDIM = 64
SSD_HEADS = SSD_D_INNER // SSD_HEADDIM
SSD_GROUPS = 4
SSD_HPG = SSD_HEADS // SSD_GROUPS
SSD_STATE = 128
SSD_CONV = 4
SSD_CHUNK = 128
SSD_CONV_DIM = SSD_D_INNER + 2 * SSD_GROUPS * SSD_STATE

RW_HEADDIM = 64
RW_HEADS = D_MODEL // RW_HEADDIM
RW_LNX_EPS = 64e-5

MB_HEADS = 16
MB_HEADDIM = D_MODEL // MB_HEADS
MB_BLOCK = 256
MB_TOPK = 3
MB_QBLOCK = 16
ROPE_DIM = MB_HEADDIM // 4
ROPE_THETA = 500000.0

ML_HEADS = 8
ML_QK_DIM = D_MODEL // 2 // ML_HEADS
ML_V_DIM = D_MODEL // ML_HEADS
ML_CHUNK = 64

LANE = 128
VMEM_LIMIT = 48 * 1024 * 1024

BF16 = jnp.bfloat16
F32 = jnp.float32


def _cparams(sem):
    return pltpu.CompilerParams(dimension_semantics=sem, vmem_limit_bytes=VMEM_LIMIT)


def _mm_kernel(a_ref, w_ref, o_ref):
    o_ref[...] = jnp.dot(a_ref[...].astype(BF16), w_ref[...], preferred_element_type=F32)


def _pick(n, cands):
    for c in cands:
        if n % c == 0:
            return c
    raise ValueError(n)


def mm(a, w):
    lead = a.shape[:-1]
    K = a.shape[-1]
    N = w.shape[-1]
    a2 = a.reshape(-1, K)
    M = a2.shape[0]
    Mp = -(-M // 8) * 8
    if Mp != M:
        a2 = jnp.pad(a2, ((0, Mp - M), (0, 0)))
    Np = -(-N // LANE) * LANE
    wb = w.astype(BF16)
    if Np != N:
        wb = jnp.pad(wb, ((0, 0), (0, Np - N)))
    tm = _pick(Mp, (512, 256, 128, 64, 32, 16, 8))
    tn = _pick(Np, (1024, 512, 256, 128))
    out = pl.pallas_call(
        _mm_kernel, name="mm",
        grid=(Mp // tm, Np // tn),
        in_specs=[pl.BlockSpec((tm, K), lambda i, j: (i, 0)),
                  pl.BlockSpec((K, tn), lambda i, j: (0, j))],
        out_specs=pl.BlockSpec((tm, tn), lambda i, j: (i, j)),
        out_shape=jax.ShapeDtypeStruct((Mp, Np), F32),
        compiler_params=_cparams(("parallel", "arbitrary")),
    )(a2, wb)
    return out[:M, :N].reshape(*lead, N)


def _ffn_kernel(x_ref, sh_ref, sc_ref, gt_ref, nw_ref, wg_ref, wu_ref, wo_ref, o_ref, h_sc, acc_sc):
    k = pl.program_id(2)
    gb, rb, d = x_ref.shape

    @pl.when(k == 0)
    def _():
        x = x_ref[...]
        y = x * lax.rsqrt(jnp.mean(x * x, axis=-1, keepdims=True) + NORM_EPS) * nw_ref[...]
        h = y * (1.0 + sc_ref[...]) + sh_ref[...]
        h_sc[...] = h.reshape(gb * rb, d).astype(BF16)

    h = h_sc[...]
    g = jnp.dot(h, wg_ref[...], preferred_element_type=F32)
    u = jnp.dot(h, wu_ref[...], preferred_element_type=F32)
    a = (g * jax.nn.sigmoid(g) * u).astype(BF16)
    part = jnp.dot(a, wo_ref[...], preferred_element_type=F32)

    @pl.when(k == 0)
    def _():
        acc_sc[...] = part

    @pl.when(k > 0)
    def _():
        acc_sc[...] += part

    @pl.when(k == pl.num_programs(2) - 1)
    def _():
        o_ref[...] = x_ref[...] + 0.5 * gt_ref[...] * acc_sc[...].reshape(gb, rb, d)


def ffn(x, shift, scale, gate, nw, w_in_b, w_out_b, gb, rb):
    G, R, D = x.shape
    tf = D_FF // 2
    nk = D_FF // tf
    xspec = pl.BlockSpec((gb, rb, D), lambda g, r, k: (g, r, 0))
    mspec = pl.BlockSpec((gb, 1, D), lambda g, r, k: (g, 0, 0))
    return pl.pallas_call(
        _ffn_kernel, name="ffn",
        grid=(G // gb, R // rb, nk),
        in_specs=[xspec, mspec, mspec, mspec,
                  pl.BlockSpec((1, 1, D), lambda g, r, k: (0, 0, 0)),
                  pl.BlockSpec((D, tf), lambda g, r, k: (0, k)),
                  pl.BlockSpec((D, tf), lambda g, r, k: (0, k + nk)),
                  pl.BlockSpec((tf, D), lambda g, r, k: (k, 0))],
        out_specs=xspec,
        out_shape=jax.ShapeDtypeStruct(x.shape, F32),
        scratch_shapes=[pltpu.VMEM((gb * rb, D), BF16), pltpu.VMEM((gb * rb, D), F32)],
        compiler_params=_cparams(("parallel", "parallel", "arbitrary")),
    )(x, shift, scale, gate, nw.reshape(1, 1, D), w_in_b, w_in_b, w_out_b)


HI = lax.Precision.HIGHEST
NN = ((1,), (0,))
NT = ((1,), (1,))
TN = ((0,), (0,))
NEG = -1e30


def _dot(a, b, dims=NN, hi=False):
    if hi:
        return lax.dot_general(a, b, (dims, ((), ())), precision=HI, preferred_element_type=F32)
    return lax.dot_general(a.astype(BF16), b.astype(BF16), (dims, ((), ())), preferred_element_type=F32)


def _split(a):
    hi = a.astype(BF16)
    return hi, (a - hi.astype(F32)).astype(BF16)


def _dot3s(sa, sb, dims=NN):
    d = lambda x, y: lax.dot_general(x, y, (dims, ((), ())), preferred_element_type=F32)
    return d(sa[0], sb[0]) + d(sa[0], sb[1]) + d(sa[1], sb[0])


def _dot3(a, b, dims=NN):
    return _dot3s(_split(a), _split(b), dims)


def _modnorm_kernel(x_ref, sh_ref, sc_ref, nw_ref, o_ref):
    x = x_ref[...]
    y = x * lax.rsqrt(jnp.mean(x * x, axis=-1, keepdims=True) + NORM_EPS) * nw_ref[...]
    o_ref[...] = (y * (1.0 + sc_ref[...]) + sh_ref[...]).astype(o_ref.dtype)


def modnorm(x, shift, scale, nw, gb, rb, dtype):
    G, R, D = x.shape
    xspec = pl.BlockSpec((gb, rb, D), lambda g, r: (g, r, 0))
    mspec = pl.BlockSpec((gb, 1, D), lambda g, r: (g, 0, 0))
    return pl.pallas_call(
        _modnorm_kernel, name="modnorm",
        grid=(G // gb, R // rb),
        in_specs=[xspec, mspec, mspec, pl.BlockSpec((1, 1, D), lambda g, r: (0, 0, 0))],
        out_specs=xspec,
        out_shape=jax.ShapeDtypeStruct(x.shape, dtype),
        compiler_params=_cparams(("parallel", "parallel")),
    )(x, shift, scale, nw.reshape(1, 1, D))


def _oproj_kernel(*refs, has_mul):
    if has_mul:
        y_ref, mul_ref, w_ref, x_ref, gt_ref, o_ref = refs
    else:
        y_ref, w_ref, x_ref, gt_ref, o_ref = refs
    gb, rb, kd = y_ref.shape
    y = y_ref[...]
    if has_mul:
        y = y * mul_ref[...]
    acc = jnp.dot(y.reshape(gb * rb, kd).astype(BF16), w_ref[...], preferred_element_type=F32)
    o_ref[...] = x_ref[...] + gt_ref[...] * acc.reshape(gb, rb, acc.shape[-1])


def oproj(y, mul, w_b, x, gate, gb, rb):
    G, R, Kd = y.shape
    D = x.shape[-1]
    yspec = pl.BlockSpec((gb, rb, Kd), lambda g, r: (g, r, 0))
    xspec = pl.BlockSpec((gb, rb, D), lambda g, r: (g, r, 0))
    ins = [y] + ([mul] if mul is not None else []) + [w_b, x, gate]
    specs = [yspec] + ([yspec] if mul is not None else []) + [
        pl.BlockSpec((Kd, D), lambda g, r: (0, 0)), xspec,
        pl.BlockSpec((gb, 1, D), lambda g, r: (g, 0, 0))]
    return pl.pallas_call(
        functools.partial(_oproj_kernel, has_mul=mul is not None), name="oproj",
        grid=(G // gb, R // rb),
        in_specs=specs,
        out_specs=xspec,
        out_shape=jax.ShapeDtypeStruct(x.shape, F32),
        compiler_params=_cparams(("parallel", "parallel")),
    )(*ins)


def _qkpost_kernel(q_ref, k_ref, cos_ref, sin_ref, qw_ref, kw_ref, qo_ref, ko_ref):
    gb, rb, w = q_ref.shape
    n = gb * rb
    lane = lax.broadcasted_iota(jnp.int32, (n, w), 1)
    lo = lane < MB_HEADDIM
    first = (lane % MB_HEADDIM) < (ROPE_DIM // 2)
    cos = jnp.broadcast_to(cos_ref[...][None], (gb, rb, w)).reshape(n, w)
    sin = jnp.broadcast_to(sin_ref[...][None], (gb, rb, w)).reshape(n, w)

    def norm_rope(x, wt):
        x2 = x * x
        s_lo = jnp.sum(jnp.where(lo, x2, 0.0), axis=-1, keepdims=True)
        s_hi = jnp.sum(jnp.where(lo, 0.0, x2), axis=-1, keepdims=True)
        ms = jnp.where(lo, s_lo, s_hi) * (1.0 / MB_HEADDIM)
        y = x * lax.rsqrt(ms + NORM_EPS) * wt
        up = pltpu.roll(y, w - ROPE_DIM // 2, 1)
        dn = pltpu.roll(y, ROPE_DIM // 2, 1)
        return y * cos + jnp.where(first, up, dn) * sin

    qo_ref[...] = norm_rope(q_ref[...].reshape(n, w), qw_ref[...]).reshape(gb, rb, w)
    ko_ref[...] = norm_rope(k_ref[...].reshape(n, w), kw_ref[...]).reshape(gb, rb, w)


def _rope_tables(pos):
    half = ROPE_DIM // 2
    inv = ROPE_THETA ** (-jnp.arange(half, dtype=F32) / half)
    ang = pos.astype(F32)[:, None] * inv
    c, s = jnp.cos(ang), jnp.sin(ang)
    ones = jnp.ones((pos.shape[0], MB_HEADDIM - ROPE_DIM), F32)
    ch = jnp.concatenate([c, c, ones], axis=-1)
    sh = jnp.concatenate([-s, s, 0.0 * ones], axis=-1)
    return jnp.concatenate([ch, ch], axis=-1), jnp.concatenate([sh, sh], axis=-1)


def qkpost(qkv, pos, q_norm, k_norm, gb, rb):
    G, R, _ = qkv.shape
    D = D_MODEL
    npair = D // LANE
    cos, sin = _rope_tables(pos)
    tspec = pl.BlockSpec((rb, LANE), lambda g, r, p: (r, 0))
    wspec = pl.BlockSpec((1, LANE), lambda g, r, p: (0, 0))
    ospec = pl.BlockSpec((gb, rb, LANE), lambda g, r, p: (g, r, p))
    return pl.pallas_call(
        _qkpost_kernel, name="qkpost",
        grid=(G // gb, R // rb, npair),
        in_specs=[pl.BlockSpec((gb, rb, LANE), lambda g, r, p: (g, r, p)),
                  pl.BlockSpec((gb, rb, LANE), lambda g, r, p: (g, r, p + npair)),
                  tspec, tspec, wspec, wspec],
        out_specs=[ospec, ospec],
        out_shape=[jax.ShapeDtypeStruct((G, R, D), F32)] * 2,
        compiler_params=_cparams(("parallel", "parallel", "parallel")),
    )(qkv, qkv, cos, sin, jnp.tile(q_norm, 2).reshape(1, LANE), jnp.tile(k_norm, 2).reshape(1, LANE))


def _top3_mask0(g, idx, nb):
    sel = jnp.zeros(g.shape, jnp.bool_)
    for _ in range(MB_TOPK):
        mx = jnp.max(g, axis=0, keepdims=True)
        first = jnp.min(jnp.where(g == mx, idx, nb), axis=0, keepdims=True)
        pick = idx == first
        sel = jnp.logical_or(sel, pick)
        g = jnp.where(pick, -jnp.inf, g)
    return sel


def _moba_prompt_t_kernel(q_ref, k_ref, v_ref, o_ref, kmean_sc, kb_sc, vt_sc, selt_sc):
    c = pl.program_id(2)
    T = MB_BLOCK
    P = MB_HEADDIM
    nb = k_ref.shape[1] // T
    scale = MB_HEADDIM ** -0.5

    @pl.when(c == 0)
    def _():
        for n in range(nb):
            kn = k_ref[0, n * T:(n + 1) * T, :]
            kmean_sc[n:n + 1, :] = jnp.mean(kn, axis=0, keepdims=True)
            kb_sc[n] = kn.astype(BF16)
            vt_sc[n] = v_ref[0, n * T:(n + 1) * T, :].T.astype(BF16)

    q = q_ref[0]
    lo = lax.broadcasted_iota(jnp.int32, (T, LANE), 1) < P
    qh = (jnp.where(lo, q, 0.0), jnp.where(lo, 0.0, q))
    blk = lax.broadcasted_iota(jnp.int32, (nb, T), 0)
    for hh in range(2):
        g = _dot(kmean_sc[...], qh[hh], NT, hi=True)
        g = jnp.where(blk < c, g, -jnp.inf)
        selt_sc[hh] = jnp.logical_and(_top3_mask0(g, blk, nb), blk < c).astype(F32)
    qb = tuple((x * scale).astype(BF16) for x in qh)
    causal = lax.broadcasted_iota(jnp.int32, (T, T), 0) <= lax.broadcasted_iota(jnp.int32, (T, T), 1)
    kc, vc = kb_sc[c], vt_sc[c]
    init = []
    for hh in range(2):
        s = jnp.where(causal, _dot(kc, qb[hh], NT), NEG)
        m = jnp.max(s, axis=0, keepdims=True)
        p = jnp.exp(s - m)
        init += [m, jnp.sum(p, axis=0, keepdims=True), _dot(vc[hh * P:(hh + 1) * P, :], p)]

    def scores(i):
        return tuple(_dot(kb_sc[jnp.minimum(2 * i + d, nb - 1)], qb[hh], NT) for hh in range(2) for d in range(2))

    def body(i, carry):
        state, s_cur = carry
        s_next = scores(i + 1)
        ns = (2 * i, 2 * i + 1)
        out, stage = [], []
        for hh in range(2):
            m, l, acc = state[3 * hh:3 * hh + 3]
            ss = [jnp.where(selt_sc[hh, pl.ds(n, 1), :] > 0.5, s_cur[2 * hh + d], NEG) for d, n in enumerate(ns)]
            m_new = jnp.maximum(m, jnp.max(jnp.maximum(ss[0], ss[1]), axis=0, keepdims=True))
            alpha = jnp.exp(m - m_new)
            ps = [jnp.exp(s - m_new) for s in ss]
            stage.append((m_new, alpha * l + jnp.sum(ps[0] + ps[1], axis=0, keepdims=True), alpha * acc, ps))
        for hh, (m_new, l, acc, ps) in enumerate(stage):
            acc = acc + sum(_dot(vt_sc[n, hh * P:(hh + 1) * P, :], p) for n, p in zip(ns, ps))
            out += [m_new, l, acc]
        return tuple(out), s_next

    res, _ = lax.fori_loop(0, (c + 1) // 2, body, (tuple(init), scores(0)))
    o_ref[0] = jnp.concatenate([res[2] / res[1], res[5] / res[4]], axis=0).T


def moba_prompt_attend_t(q, k, v):
    B, L, D = q.shape
    nb = L // MB_BLOCK
    kvspec = pl.BlockSpec((1, L, LANE), lambda b, p, c: (b, 0, p))
    qspec = pl.BlockSpec((1, MB_BLOCK, LANE), lambda b, p, c: (b, c, p))
    return pl.pallas_call(
        _moba_prompt_t_kernel, name="moba_prompt",
        grid=(B, D // LANE, nb),
        in_specs=[qspec, kvspec, kvspec],
        out_specs=qspec,
        out_shape=jax.ShapeDtypeStruct((B, L, D), F32),
        scratch_shapes=[pltpu.VMEM((nb, LANE), F32), pltpu.VMEM((nb, MB_BLOCK, LANE), BF16),
                        pltpu.VMEM((nb, LANE, MB_BLOCK), BF16), pltpu.VMEM((2, nb, MB_BLOCK), F32)],
        compiler_params=_cparams(("parallel", "parallel", "arbitrary")),
    )(q, k, v)


def _moba_sample_kernel(pt_ref, q_ref, kn_ref, vn_ref, k0_ref, k1_ref, v0_ref, v1_ref, o_ref,
                        qbd_sc, qbdb_sc, kmean_sc, o_sc, m_sc, l_sc):
    del pt_ref
    n = pl.program_id(1)
    nblk = pl.num_programs(1)
    lq, d = q_ref.shape[1], q_ref.shape[2]
    rows = MB_HEADS * lq
    scale = MB_HEADDIM ** -0.5
    own = (lax.broadcasted_iota(jnp.int32, (MB_HEADS, lq, d), 0)
           == lax.broadcasted_iota(jnp.int32, (MB_HEADS, lq, d), 2) // MB_HEADDIM)

    @pl.when(n == 0)
    def _():
        qbd = jnp.where(own, q_ref[0][None], 0.0).reshape(rows, d)
        qbd_sc[...] = qbd
        qbdb_sc[...] = (qbd * scale).astype(BF16)

    qb = qbdb_sc[...]
    k0, k1 = _split(k0_ref[0]), _split(k1_ref[0])
    s0 = _dot(qb, k0[0])
    s1 = _dot(qb, k1[0])
    m = jnp.maximum(jnp.max(s0, axis=-1, keepdims=True), jnp.max(s1, axis=-1, keepdims=True))
    p0 = jnp.exp(s0 - m)
    p1 = jnp.exp(s1 - m)
    o_sc[n] = _dot(p0, v0_ref[0], NT) + _dot(p1, v1_ref[0], NT)
    m_sc[n] = m
    l_sc[n] = jnp.sum(p0, axis=-1, keepdims=True) + jnp.sum(p1, axis=-1, keepdims=True)
    ones = jnp.ones((8, k0_ref.shape[2]), BF16)
    d1 = lambda x: lax.dot_general(ones, x, (NT, ((), ())), preferred_element_type=F32)
    ksum = d1(k0[0]) + d1(k0[1]) + d1(k1[0]) + d1(k1[1])
    kmean_sc[n] = ksum[0:1] * (1.0 / MB_BLOCK)

    @pl.when(n == nblk - 1)
    def _():
        qbd = qbd_sc[...]
        nblocks = o_sc.shape[0]
        s = lax.dot_general(qbd * scale, kn_ref[0], (NT, ((), ())), preferred_element_type=F32)
        qi = lax.broadcasted_iota(jnp.int32, (rows, lq), 0) % lq
        s = jnp.where(lax.broadcasted_iota(jnp.int32, (rows, lq), 1) <= qi, s, NEG)
        m_own = jnp.max(s, axis=-1, keepdims=True)
        p = jnp.exp(s - m_own)
        l_own = jnp.sum(p, axis=-1, keepdims=True)
        o_own = jnp.dot(p, vn_ref[0], preferred_element_type=F32)
        g = [jnp.sum(qbd * kmean_sc[j], axis=-1, keepdims=True) for j in range(nblocks)]
        sel = []
        for j in range(nblocks):
            rank = jnp.zeros((rows, 1), jnp.int32)
            for i in range(nblocks):
                if i != j:
                    ahead = (g[i] >= g[j]) if i < j else (g[i] > g[j])
                    rank = rank + ahead.astype(jnp.int32)
            sel.append(rank < MB_TOPK)
        mx = m_own
        for j in range(nblocks):
            mx = jnp.where(sel[j], jnp.maximum(mx, m_sc[j]), mx)
        w_own = jnp.exp(m_own - mx)
        l = l_own * w_own
        o = o_own * w_own
        for j in range(nblocks):
            wj = jnp.where(sel[j], jnp.exp(m_sc[j] - mx), 0.0)
            l = l + wj * l_sc[j]
            o = o + wj * o_sc[j]
        res = (o / l).reshape(MB_HEADS, lq, d)
        o_ref[0] = jnp.sum(jnp.where(own, res, 0.0), axis=0)


def moba_sample_attend(q, k_new, v_new, cache_k, cache_v, page_table):
    Bs, Ls, D = q.shape
    page = cache_k.shape[2]
    assert MB_BLOCK == 2 * page and PAST_LEN % MB_BLOCK == 0 and Ls <= MB_BLOCK
    nblk = PAST_LEN // MB_BLOCK
    assert nblk >= MB_TOPK
    rows = MB_HEADS * Ls
    newspec = pl.BlockSpec((1, Ls, D), lambda b, n, pt: (b, 0, 0))
    pg0 = pl.BlockSpec((1, D, page), lambda b, n, pt: (pt[b, 2 * n], 0, 0))
    pg1 = pl.BlockSpec((1, D, page), lambda b, n, pt: (pt[b, 2 * n + 1], 0, 0))
    return pl.pallas_call(
        _moba_sample_kernel, name="moba_sample",
        grid_spec=pltpu.PrefetchScalarGridSpec(
            num_scalar_prefetch=1,
            grid=(Bs, nblk),
            in_specs=[newspec, newspec, newspec, pg0, pg1, pg0, pg1],
            out_specs=newspec,
            scratch_shapes=[pltpu.VMEM((rows, D), F32), pltpu.VMEM((rows, D), BF16),
                            pltpu.VMEM((nblk, 1, D), F32), pltpu.VMEM((nblk, rows, D), F32),
                            pltpu.VMEM((nblk, rows, 1), F32), pltpu.VMEM((nblk, rows, 1), F32)]),
        out_shape=jax.ShapeDtypeStruct((Bs, Ls, D), F32),
        compiler_params=_cparams(("parallel", "arbitrary")),
    )(page_table, q, k_new, v_new, cache_k, cache_k, cache_v, cache_v)


def _softplus(z):
    return jnp.maximum(z, 0.0) + jnp.log1p(jnp.exp(-jnp.abs(z)))


def _sigmoid(z):
    return 1.0 / (1.0 + jnp.exp(-z))


def _rwkv_pre_kernel(x_ref, halo_ref, sb_ref, sh_ref, sc_ref, nw_ref, mu_ref, w0_ref, a0_ref,
                     wr_ref, wk_ref, wv_ref, w1_ref, w2_ref, a1_ref, a2_ref, g1_ref, g2_ref,
                     r_o, lw_o, k_o, v_o, a_o, g_o, hl_o, hbuf):
    rr = pl.program_id(1)
    gb, rb, d = x_ref.shape
    n = gb * rb

    def hn(x):
        y = x * lax.rsqrt(jnp.mean(x * x, axis=-1, keepdims=True) + NORM_EPS) * nw_ref[...]
        return y * (1.0 + sc_ref[...]) + sh_ref[...]

    h = hn(x_ref[...])
    prev_row = jnp.where(rr == 0, sb_ref[...], hn(halo_ref[...])[:, 7:8, :])
    hbuf[:, 8:, :] = h
    hbuf[:, 7:8, :] = prev_row
    h2 = h.reshape(n, d)
    xx = hbuf[:, 7:7 + rb, :].reshape(n, d) - h2
    mix = lambda i: h2 + xx * mu_ref[i:i + 1, :]
    dot = lambda a, w_ref: jnp.dot(a.astype(BF16), w_ref[...], preferred_element_type=F32)
    out = lambda o_ref, val: o_ref.__setitem__(Ellipsis, val.reshape(gb, rb, d))
    out(r_o, dot(mix(0), wr_ref))
    out(k_o, dot(mix(2), wk_ref))
    out(v_o, dot(mix(3), wv_ref))
    w_raw = w0_ref[...] + dot(jnp.tanh(dot(mix(1), w1_ref)), w2_ref)
    out(lw_o, -jnp.exp(-_softplus(-w_raw) - 0.5))
    out(a_o, _sigmoid(a0_ref[...] + dot(dot(mix(4), a1_ref), a2_ref)))
    out(g_o, dot(_sigmoid(dot(mix(5), g1_ref)), g2_ref))
    hl_o[...] = h[:, rb - 1:rb, :]


def rwkv_pre(x, shift_buf, shift, scale, nw, mu, w0, a0, ws, gb, rb):
    G, R, D = x.shape
    assert gb == 1 or rb == R
    xspec = pl.BlockSpec((gb, rb, D), lambda g, r: (g, r, 0))
    halo = pl.BlockSpec((gb, 8, D), lambda g, r: (g, jnp.maximum(r * (rb // 8) - 1, 0), 0))
    mspec = pl.BlockSpec((gb, 1, D), lambda g, r: (g, 0, 0))
    vec = pl.BlockSpec((1, D), lambda g, r: (0, 0))
    full = lambda w: pl.BlockSpec(w.shape, lambda g, r: (0, 0))
    mu8 = jnp.pad(mu, ((0, 8 - mu.shape[0]), (0, 0)))
    big = jax.ShapeDtypeStruct((G, R, D), F32)
    return pl.pallas_call(
        _rwkv_pre_kernel, name="rwkv_pre",
        grid=(G // gb, R // rb),
        in_specs=[xspec, halo, mspec, mspec, mspec, vec, full(mu8), vec, vec] + [full(w) for w in ws],
        out_specs=[xspec] * 6 + [mspec],
        out_shape=[big] * 6 + [jax.ShapeDtypeStruct((G, 1, D), F32)],
        scratch_shapes=[pltpu.VMEM((gb, rb + 8, D), F32)],
        compiler_params=_cparams(("parallel", "arbitrary")),
    )(x, x, shift_buf.reshape(G, 1, D), shift, scale, nw.reshape(1, D), mu8, w0.reshape(1, D),
      a0.reshape(1, D), *ws)


def _rwkv_scan_kernel(r_ref, lw_ref, k_ref, v_ref, a_ref, kk_ref, ka_ref, rk_ref, lnw_ref, lnb_ref, s0_ref,
                      y_ref, so_ref, s_sc, *, hs):
    c = pl.program_id(2)
    C = r_ref.shape[1]
    N = RW_HEADDIM

    @pl.when(c == 0)
    def _():
        s_sc[...] = s0_ref[...]

    hg = s_sc.shape[1]
    rows = hs * C
    s_in = [[s_sc[bi, h] for h in range(hg)] for bi in range(s_sc.shape[0])]
    s_out = []

    def chain(bi, g):
        heads = range(g * hs, (g + 1) * hs)
        stack = lambda ref: jnp.concatenate([ref[bi][:, h * N:(h + 1) * N] for h in heads], axis=0)
        pstack = lambda ref: jnp.concatenate(
            [jnp.broadcast_to(ref[:, h * N:(h + 1) * N], (C, N)) for h in heads], axis=0)
        r, lw, kraw, v, a = (stack(ref) for ref in (r_ref, lw_ref, k_ref, v_ref, a_ref))
        kkr = kraw * pstack(kk_ref)
        kk = kkr / jnp.maximum(jnp.sqrt(jnp.sum(kkr * kkr, axis=-1, keepdims=True)), 1e-12)
        k = kraw * (1.0 + (a - 1.0) * pstack(ka_ref))
        tpos = lax.broadcasted_iota(jnp.int32, (rows, N), 0) % C
        cum, sh = lw, 1
        while sh < C:
            cum = cum + jnp.where(tpos >= sh, pltpu.roll(cum, sh, 0), 0.0)
            sh *= 2
        e_in, e_ex, e_neg = jnp.exp(cum), jnp.exp(cum - lw), jnp.exp(-cum)
        ab, bb, kb, rb_ = -kk * e_ex, kk * a * e_neg, k * e_neg, r * e_in
        ri = lax.broadcasted_iota(jnp.int32, (rows, rows), 0)
        ci = lax.broadcasted_iota(jnp.int32, (rows, rows), 1)
        same = (ri // C) == (ci // C)
        strict, incl = jnp.logical_and(same, ci < ri), jnp.logical_and(same, ci <= ri)
        sab, sbb, skb, srb = _split(ab), _split(bb), _split(kb), _split(rb_)
        d1 = lambda x, y, dims=NN: lax.dot_general(x, y, (dims, ((), ())), preferred_element_type=F32)
        lab = jnp.where(strict, d1(sab[0], sbb[0], NT), 0.0)
        lak = jnp.where(strict, d1(sab[0], skb[0], NT), 0.0)
        mrb = jnp.where(incl, d1(srb[0], sbb[0], NT), 0.0)
        mrk = jnp.where(incl, d1(srb[0], skb[0], NT), 0.0)
        tinv = (ri == ci).astype(F32) + lab
        spw = _split(lab)
        yield
        for _ in range(int(math.log2(C)) - 1):
            spw = _split(_dot3s(spw, spw))
            yield
            tinv = tinv + _dot3s(_split(tinv), spw)
            yield
        hsl = lambda t, h: t[h * C:(h + 1) * C]
        xs = [_dot3(jnp.concatenate([hsl(ab, j), hsl(rb_, j)], axis=0), s_in[bi][h], NT) for j, h in enumerate(heads)]
        ab_s = jnp.concatenate([x[:C] for x in xs], axis=0)
        rb_s = jnp.concatenate([x[C:] for x in xs], axis=0)
        yield
        u = _dot3(tinv, ab_s + _dot3(lak, v))
        yield
        y = rb_s + _dot3(jnp.concatenate([mrb, mrk], axis=1), jnp.concatenate([u, v], axis=0))
        yield
        for j, h in enumerate(heads):
            upd = _dot3(jnp.concatenate([hsl(u, j), hsl(v, j)], axis=0),
                        jnp.concatenate([hsl(bb, j), hsl(kb, j)], axis=0), TN)
            s_out.append((bi, h, (s_in[bi][h] + upd) * e_in[(j + 1) * C - 1:(j + 1) * C, :]))
        mean = jnp.mean(y, axis=-1, keepdims=True)
        var = jnp.mean(jnp.square(y - mean), axis=-1, keepdims=True)
        yn = (y - mean) * lax.rsqrt(var + RW_LNX_EPS) * pstack(lnw_ref) + pstack(lnb_ref)
        out = yn + jnp.sum(r * k * pstack(rk_ref), axis=-1, keepdims=True) * v
        y_ref[bi, :, g * hs * N:(g + 1) * hs * N] = jnp.concatenate([hsl(out, j) for j in range(hs)], axis=-1)

    chains = [chain(bi, g) for bi in range(s_sc.shape[0]) for g in range(hg // hs)]
    while chains:
        chains = [g for g in chains if next(g, True) is None]
    for bi, h, val in s_out:
        s_sc[bi, h] = val

    @pl.when(c == pl.num_programs(2) - 1)
    def _():
        so_ref[...] = s_sc[...]


def rwkv_scan(r, lw, k, v, a, k_k, k_a, r_k, lnx_w, lnx_b, s0, C, hg, nbk, hs):
    B, L, D = r.shape
    N = RW_HEADDIM
    assert L % C == 0 and C & (C - 1) == 0 and RW_HEADS % hg == 0 and B % nbk == 0
    assert hg % hs == 0 and (hs * N) % LANE == 0
    tspec = pl.BlockSpec((nbk, C, hg * N), lambda b, p, c: (b, c, p))
    pspec = pl.BlockSpec((1, hg * N), lambda b, p, c: (0, p))
    sspec = pl.BlockSpec((nbk, hg, N, N), lambda b, p, c: (b, p, 0, 0))
    vec = lambda t: t.reshape(1, D)
    return pl.pallas_call(
        functools.partial(_rwkv_scan_kernel, hs=hs), name="rwkv_scan",
        grid=(B // nbk, RW_HEADS // hg, L // C),
        in_specs=[tspec] * 5 + [pspec] * 5 + [sspec],
        out_specs=[tspec, sspec],
        out_shape=[jax.ShapeDtypeStruct((B, L, D), F32), jax.ShapeDtypeStruct(s0.shape, F32)],
        scratch_shapes=[pltpu.VMEM((nbk, hg, N, N), F32)],
        compiler_params=_cparams(("parallel", "parallel", "arbitrary")),
    )(r, lw, k, v, a, vec(k_k), vec(k_a), vec(r_k), vec(lnx_w), vec(lnx_b), s0)


def rwkv_layer(x, shift, scale, gate, nw, shift_buf, wkv0, mu, w_r, w_k, w_v, w0, w1, w2, a0, a1, a2, g1, g2,
               k_k, k_a, r_k, lnx_w, lnx_b, w_o, gb, rb, C, hg, nbk, hs):
    ws = [w.astype(BF16) for w in (w_r, w_k, w_v, w1, w2, a1, a2, g1, g2)]
    r, lw, k, v, a, g, hl = rwkv_pre(x, shift_buf, shift, scale, nw, mu, w0, a0, ws, gb, rb)
    y, s_fin = rwkv_scan(r, lw, k, v, a, k_k, k_a, r_k, lnx_w, lnx_b, wkv0, C, hg, nbk, hs)
    return oproj(y, g, w_o.astype(BF16), x, gate, gb, rb), hl[:, 0], s_fin


def _conv_kernel(x_ref, halo_ref, buf_ref, w_ref, b_ref, o_ref, sc):
    rr = pl.program_id(1)
    gb, rb, cw = x_ref.shape
    sc[:, :8, :] = jnp.where(rr == 0, buf_ref[...], halo_ref[...])
    sc[:, 8:, :] = x_ref[...]
    y = b_ref[...]
    for t in range(SSD_CONV):
        y = y + sc[:, 8 - (SSD_CONV - 1) + t:8 - (SSD_CONV - 1) + t + rb, :] * w_ref[t:t + 1, :]
    o_ref[...] = y * _sigmoid(y)


def ssd_conv(zx, buf, w, b, gb, rb, cw=512):
    G, R, _ = zx.shape
    assert gb == 1 or rb == R
    off = SSD_D_INNER // cw
    buf8 = jnp.pad(buf, ((0, 0), (8 - (SSD_CONV - 1), 0), (0, 0)))
    w8 = jnp.pad(w, ((0, 8 - SSD_CONV), (0, 0)))
    return pl.pallas_call(
        _conv_kernel, name="ssd_conv",
        grid=(G // gb, R // rb, SSD_CONV_DIM // cw),
        in_specs=[pl.BlockSpec((gb, rb, cw), lambda g, r, j: (g, r, off + j)),
                  pl.BlockSpec((gb, 8, cw), lambda g, r, j: (g, jnp.maximum(r * (rb // 8) - 1, 0), off + j)),
                  pl.BlockSpec((gb, 8, cw), lambda g, r, j: (g, 0, j)),
                  pl.BlockSpec((8, cw), lambda g, r, j: (0, j)),
                  pl.BlockSpec((1, cw), lambda g, r, j: (0, j))],
        out_specs=pl.BlockSpec((gb, rb, cw), lambda g, r, j: (g, r, j)),
        out_shape=jax.ShapeDtypeStruct((G, R, SSD_CONV_DIM), F32),
        scratch_shapes=[pltpu.VMEM((gb, rb + 8, cw), F32)],
        compiler_params=_cparams(("parallel", "arbitrary", "arbitrary")),
    )(zx, zx, buf8, w8, b.reshape(1, SSD_CONV_DIM))


def _col2row(col, eye):
    return jnp.sum(jnp.where(eye, col, 0.0), axis=0, keepdims=True)


def _ssd_scan_kernel(x_ref, bm_ref, cm_ref, z_ref, dt_ref, dtb_ref, alog_ref, dsk_ref, nw_ref, s0_ref,
                     y_ref, so_ref, s_sc):
    c = pl.program_id(2)
    Q = x_ref.shape[1]
    P, N, R = SSD_HEADDIM, SSD_STATE, SSD_HPG

    @pl.when(c == 0)
    def _():
        s_sc[...] = s0_ref[0]

    ti = lax.broadcasted_iota(jnp.int32, (Q, Q), 0)
    si = lax.broadcasted_iota(jnp.int32, (Q, Q), 1)
    causal, eye = si <= ti, si == ti
    lo = lax.broadcasted_iota(jnp.int32, (Q, LANE), 1) < P
    dt = _softplus(dt_ref[0] + dtb_ref[...])
    cs = _dot(causal.astype(F32), dt * (-jnp.exp(alog_ref[...])), hi=True)
    x, bm, cm = x_ref[0], bm_ref[0], cm_ref[0]
    cb = _dot(cm, bm, NT)
    s_all = s_sc[...].reshape(R * P, N)
    ys = _dot(cm, s_all, NT)
    y_pairs, xw_pairs, e_last = [], [], []
    for j in range(R // 2):
        xp = x[:, j * LANE:(j + 1) * LANE]
        yh, grow, wend = [], [], []
        for r in (2 * j, 2 * j + 1):
            cs_col, dt_col = cs[:, r:r + 1], dt[:, r:r + 1]
            cs_last = cs[Q - 1:Q, r:r + 1]
            dec = jnp.exp(jnp.where(causal, cs_col - _col2row(cs_col, eye), -jnp.inf))
            yh.append(_dot(cb * dec * _col2row(dt_col, eye), xp))
            grow.append(jnp.exp(cs_col))
            wend.append(jnp.exp(cs_last - cs_col) * dt_col)
            e_last.append(jnp.exp(cs_last))
        yp = jnp.where(lo, yh[0], yh[1]) + ys[:, j * LANE:(j + 1) * LANE] * jnp.where(lo, grow[0], grow[1])
        y_pairs.append(yp + xp * dsk_ref[:, j * LANE:(j + 1) * LANE])
        xw_pairs.append(xp * jnp.where(lo, wend[0], wend[1]))
    upd = _dot(jnp.concatenate(xw_pairs, axis=-1), bm, TN)
    for r in range(R):
        s_sc[r] = s_sc[r] * e_last[r] + upd[r * P:(r + 1) * P, :]
    z = z_ref[0]
    yz = jnp.concatenate(y_pairs, axis=-1) * (z * _sigmoid(z))
    y_ref[0] = yz * lax.rsqrt(jnp.mean(yz * yz, axis=-1, keepdims=True) + NORM_EPS) * nw_ref[...]

    @pl.when(c == pl.num_programs(2) - 1)
    def _():
        so_ref[0] = s_sc[...]


def ssd_chunk_scan(xbc, zx, dt_raw, dt_bias, a_log, d_skip, norm_w, s0, Q):
    B, L, _ = xbc.shape
    G, R, P, N = SSD_GROUPS, SSD_HPG, SSD_HEADDIM, SSD_STATE
    gw = R * P
    assert L % Q == 0 and N == LANE
    pad_heads = lambda t: jnp.pad(t.reshape(G, R), ((0, 0), (0, LANE - R))).reshape(1, G * LANE)
    xs = pl.BlockSpec((1, Q, gw), lambda b, g, c: (b, c, g))
    bs = pl.BlockSpec((1, Q, N), lambda b, g, c: (b, c, SSD_D_INNER // N + g))
    cs = pl.BlockSpec((1, Q, N), lambda b, g, c: (b, c, SSD_D_INNER // N + G + g))
    ds = pl.BlockSpec((1, Q, LANE), lambda b, g, c: (b, c, g))
    hs = pl.BlockSpec((1, LANE), lambda b, g, c: (0, g))
    ws = pl.BlockSpec((1, gw), lambda b, g, c: (0, g))
    ss = pl.BlockSpec((1, R, P, N), lambda b, g, c: (b, g, 0, 0))
    return pl.pallas_call(
        _ssd_scan_kernel, name="ssd_scan",
        grid=(B, G, L // Q),
        in_specs=[xs, bs, cs, xs, ds, hs, hs, ws, ws, ss],
        out_specs=[xs, ss],
        out_shape=[jax.ShapeDtypeStruct((B, L, SSD_D_INNER), F32), jax.ShapeDtypeStruct(s0.shape, F32)],
        scratch_shapes=[pltpu.VMEM((R, P, N), F32)],
        compiler_params=_cparams(("parallel", "parallel", "arbitrary")),
    )(xbc, xbc, xbc, zx, dt_raw, pad_heads(dt_bias), pad_heads(a_log),
      jnp.repeat(d_skip, P).reshape(1, SSD_D_INNER), norm_w.reshape(1, SSD_D_INNER), s0)


def ssd_layer(x, shift, scale, gate, nw, conv_buf, ssm0, w_in, conv_w, conv_b, dt_bias, a_log, d_skip, norm_w,
              w_out, gb, rb, Q):
    G, R, D = x.shape
    nzx = SSD_D_INNER + SSD_CONV_DIM
    h = modnorm(x, shift, scale, nw, gb, rb, BF16)
    zx = mm(h, w_in[:, :nzx])
    w_dt = jnp.pad(w_in[:, nzx:].reshape(D, SSD_GROUPS, SSD_HPG), ((0, 0), (0, 0), (0, LANE - SSD_HPG)))
    dt_raw = mm(h, w_dt.reshape(D, SSD_GROUPS * LANE))
    xbc = ssd_conv(zx, conv_buf, conv_w, conv_b, gb, rb)
    y, s_fin = ssd_chunk_scan(xbc, zx, dt_raw, dt_bias, a_log, d_skip, norm_w, ssm0, Q)
    new_conv = zx[:, R - (SSD_CONV - 1):, SSD_D_INNER:]
    return oproj(y, None, w_out.astype(BF16), x, gate, gb, rb), new_conv, s_fin


def _mlstm_kernel(q_ref, k_ref, v_ref, o_ref, g_ref, bif_ref, nw_ref, c0_ref, n0_ref, m0_ref,
                  y_ref, co_ref, no_ref, mo_ref, c_sc, n_sc, m_sc):
    c = pl.program_id(2)
    Q = q_ref.shape[1]
    DK, DV = ML_QK_DIM, ML_V_DIM

    @pl.when(c == 0)
    def _():
        c_sc[...] = c0_ref[0]
        n_sc[...] = n0_ref[0, 0]
        m_sc[...] = m0_ref[0, 0]

    ti = lax.broadcasted_iota(jnp.int32, (Q, Q), 0)
    si = lax.broadcasted_iota(jnp.int32, (Q, Q), 1)
    causal, eye = si <= ti, si == ti
    gates = g_ref[0] + bif_ref[...]
    bcum = _dot(causal.astype(F32), -_softplus(-gates), hi=True)
    ys = []
    for hh in range(2):
        i_col, b_col = gates[:, hh:hh + 1], bcum[:, 2 + hh:3 + hh]
        m_prev = m_sc[:, hh:hh + 1]
        q = q_ref[0][:, hh * DK:(hh + 1) * DK] * DK ** -0.5
        k = k_ref[0][:, hh * DK:(hh + 1) * DK]
        v = v_ref[0][:, hh * DV:(hh + 1) * DV]
        cst, nst = c_sc[hh], n_sc[hh:hh + 1, :]
        dmat = jnp.where(causal, b_col - _col2row(b_col, eye) + _col2row(i_col, eye), -jnp.inf)
        m_inter = b_col + m_prev
        m_t = jnp.maximum(m_inter, jnp.max(dmat, axis=-1, keepdims=True))
        w_inter = jnp.exp(m_inter - m_t)
        qk = _dot(q, k, NT) * jnp.exp(dmat - m_t)
        num = _dot(qk, v) + w_inter * _dot(q, cst, NT)
        den = jnp.sum(qk, axis=-1, keepdims=True) + w_inter * jnp.sum(q * nst, axis=-1, keepdims=True)
        hout = num / jnp.maximum(jnp.abs(den), jnp.exp(-m_t))
        m_last, b_last = m_t[Q - 1:Q], b_col[Q - 1:Q]
        w_state = jnp.exp(b_last - b_col + i_col - m_last)
        decay = jnp.exp(b_last + m_prev - m_last)
        c_sc[hh] = decay * cst + _dot(v * w_state, k, TN)
        n_sc[hh:hh + 1, :] = decay * nst + jnp.sum(w_state * k, axis=0, keepdims=True)
        m_sc[:, hh:hh + 1] = m_last
        hn = hout * lax.rsqrt(jnp.mean(hout * hout, axis=-1, keepdims=True) + NORM_EPS)
        ys.append(_sigmoid(o_ref[0][:, hh * DV:(hh + 1) * DV]) * (hn * nw_ref[:, hh * DV:(hh + 1) * DV]))
    y_ref[0] = jnp.concatenate(ys, axis=-1)

    @pl.when(c == pl.num_programs(2) - 1)
    def _():
        co_ref[0] = c_sc[...]
        no_ref[0, 0] = n_sc[...]
        mo_ref[0, 0] = m_sc[...]


def mlstm_scan(proj, gates, b_if, norm_w, c0, n0, m0, Q):
    B, L, _ = proj.shape
    H, DK, DV = ML_HEADS, ML_QK_DIM, ML_V_DIM
    assert L % Q == 0 and 2 * DK == LANE
    npair = H // 2
    qs = pl.BlockSpec((1, Q, LANE), lambda b, j, c: (b, c, j))
    ks = pl.BlockSpec((1, Q, LANE), lambda b, j, c: (b, c, npair + j))
    vs = pl.BlockSpec((1, Q, 2 * DV), lambda b, j, c: (b, c, H * DK // DV + j))
    os_ = pl.BlockSpec((1, Q, 2 * DV), lambda b, j, c: (b, c, H * DK // DV + npair + j))
    ys = pl.BlockSpec((1, Q, 2 * DV), lambda b, j, c: (b, c, j))
    cs = pl.BlockSpec((1, 2, DV, DK), lambda b, j, c: (b, j, 0, 0))
    ns = pl.BlockSpec((1, 1, 2, DK), lambda b, j, c: (b, j, 0, 0))
    ms = pl.BlockSpec((1, 1, 1, 2), lambda b, j, c: (b, j, 0, 0))
    bi, bf = b_if[:H].reshape(npair, 2), b_if[H:].reshape(npair, 2)
    bias = jnp.pad(jnp.concatenate([bi, bf], axis=-1), ((0, 0), (0, LANE - 4))).reshape(1, npair * LANE)
    y, c, n, m = pl.pallas_call(
        _mlstm_kernel, name="mlstm",
        grid=(B, npair, L // Q),
        in_specs=[qs, ks, vs, os_, qs, pl.BlockSpec((1, LANE), lambda b, j, c: (0, j)),
                  pl.BlockSpec((1, 2 * DV), lambda b, j, c: (0, j)), cs, ns, ms],
        out_specs=[ys, cs, ns, ms],
        out_shape=[jax.ShapeDtypeStruct((B, L, H * DV), F32), jax.ShapeDtypeStruct((B, H, DV, DK), F32),
                   jax.ShapeDtypeStruct((B, npair, 2, DK), F32), jax.ShapeDtypeStruct((B, npair, 1, 2), F32)],
        scratch_shapes=[pltpu.VMEM((2, DV, DK), F32), pltpu.VMEM((2, DK), F32), pltpu.VMEM((1, 2), F32)],
        compiler_params=_cparams(("parallel", "parallel", "arbitrary")),
    )(proj, proj, proj, proj, gates, bias, norm_w.reshape(1, H * DV), c0,
      n0.reshape(B, npair, 2, DK), m0.reshape(B, npair, 1, 2))
    return y, c, n.reshape(B, H, DK), m.reshape(B, H)


def mlstm_layer(x, shift, scale, gate, nw, c0, n0, m0, w_in, b_if, norm_w, w_o, gb, rb, Q):
    D = x.shape[-1]
    H = ML_HEADS
    nmain = 2 * H * ML_QK_DIM + 2 * H * ML_V_DIM
    h = modnorm(x, shift, scale, nw, gb, rb, BF16)
    proj = mm(h, w_in[:, :nmain])
    wi, wf = w_in[:, nmain:nmain + H].reshape(D, H // 2, 2), w_in[:, nmain + H:].reshape(D, H // 2, 2)
    wg = jnp.pad(jnp.concatenate([wi, wf], axis=-1), ((0, 0), (0, 0), (0, LANE - 4)))
    gates = mm(h, wg.reshape(D, (H // 2) * LANE))
    y, c, n, m = mlstm_scan(proj, gates, b_if, norm_w, c0, n0, m0, Q)
    return oproj(y, None, w_o.astype(BF16), x, gate, gb, rb), c, n, m


def kernel(x_prompt, x_sample, state_ssd_conv, state_ssd, state_rwkv_shift, state_rwkv, cache_k, cache_v, state_mlstm_c, state_mlstm_n, state_mlstm_m, page_table, c_prompt, c_sample, ada_w, ada_b, norm_w, ffn_w_in, ffn_w_out, ssd_w_in, ssd_conv_w, ssd_conv_b, ssd_dt_bias, ssd_a_log, ssd_d, ssd_norm_w, ssd_w_out, rw_mu, rw_w_r, rw_w_k, rw_w_v, rw_w0, rw_w1, rw_w2, rw_a0, rw_a1, rw_a2, rw_g1, rw_g2, rw_k_k, rw_k_a, rw_r_k, rw_lnx_w, rw_lnx_b, rw_w_o, mb_w_qkv, mb_q_norm, mb_k_norm, mb_w_o, ml_w_in, ml_b_if, ml_norm_w, ml_w_o):
    B, S = x_prompt.shape[0], x_prompt.shape[1]
    Bs, Ls = x_sample.shape[0], x_sample.shape[1]
    D = D_MODEL
    pos_p = jnp.arange(S, dtype=jnp.int32)
    pos_s = PAST_LEN + jnp.arange(Ls, dtype=jnp.int32)
    xp, xs = x_prompt, x_sample
    outs = {}
    c_all = jnp.concatenate([c_prompt, c_sample], axis=0)
    sc_all = jax.nn.silu(c_all)
    for i in range(DEPTH):
        kind, j = i % N_MIXERS, i // N_MIXERS
        mod = (mm(sc_all, ada_w[i]) + ada_b[i]).reshape(B + Bs, N_MOD, 1, D)
        mp, ms = mod[:B], mod[B:]
        wi0, wo0 = ffn_w_in[i, 0].astype(BF16), ffn_w_out[i, 0].astype(BF16)
        wi1, wo1 = ffn_w_in[i, 1].astype(BF16), ffn_w_out[i, 1].astype(BF16)
        xp = ffn(xp, mp[:, 0], mp[:, 1], mp[:, 2], norm_w[i, 0], wi0, wo0, 1, 512)
        xs = ffn(xs, ms[:, 0], ms[:, 1], ms[:, 2], norm_w[i, 0], wi0, wo0, 64, Ls)
        m1p, m1s = (mp[:, 3], mp[:, 4], mp[:, 5], norm_w[i, 1]), (ms[:, 3], ms[:, 4], ms[:, 5], norm_w[i, 1])
        if kind == 0:
            w = (ssd_w_in[j], ssd_conv_w[j], ssd_conv_b[j], ssd_dt_bias[j], ssd_a_log[j], ssd_d[j], ssd_norm_w[j], ssd_w_out[j])
            xp, cp, sp = ssd_layer(xp, *m1p, jnp.zeros((B, SSD_CONV - 1, SSD_CONV_DIM), F32),
                                   jnp.zeros((B, SSD_HEADS, SSD_HEADDIM, SSD_STATE), F32), *w, 1, 512, SSD_CHUNK)
            xs, cs_, ss = ssd_layer(xs, *m1s, state_ssd_conv[j], state_ssd[j], *w, 64, Ls, Ls)
            outs[2], outs[3], outs[4], outs[5] = cp[None], cs_[None], sp[None], ss[None]
        elif kind == 1:
            w = (rw_mu[j], rw_w_r[j], rw_w_k[j], rw_w_v[j], rw_w0[j], rw_w1[j], rw_w2[j], rw_a0[j], rw_a1[j],
                 rw_a2[j], rw_g1[j], rw_g2[j], rw_k_k[j], rw_k_a[j], rw_r_k[j], rw_lnx_w[j], rw_lnx_b[j], rw_w_o[j])
            xp, shp, sp = rwkv_layer(xp, *m1p, jnp.zeros((B, D), F32),
                                     jnp.zeros((B, RW_HEADS, RW_HEADDIM, RW_HEADDIM), F32), *w, 1, 256, 64, 4, B, 2)
            xs, shs, ss = rwkv_layer(xs, *m1s, state_rwkv_shift[j], state_rwkv[j], *w, 32, Ls, Ls, RW_HEADS, 2,
                                     RW_HEADS)
            outs[6], outs[7], outs[8], outs[9] = shp[None], shs[None], sp[None], ss[None]
        elif kind == 2:
            hp = modnorm(xp, mp[:, 3], mp[:, 4], norm_w[i, 1], 1, 512, BF16)
            hs = modnorm(xs, ms[:, 3], ms[:, 4], norm_w[i, 1], 64, Ls, BF16)
            qkv_p, qkv_s = mm(hp, mb_w_qkv[j]), mm(hs, mb_w_qkv[j])
            qp, kp = qkpost(qkv_p, pos_p, mb_q_norm[j], mb_k_norm[j], 1, 512)
            qs, ks_ = qkpost(qkv_s, pos_s, mb_q_norm[j], mb_k_norm[j], 16, Ls)
            vp, vs = qkv_p[..., 2 * D:], qkv_s[..., 2 * D:]
            op = moba_prompt_attend_t(qp, kp, vp)
            nl, npool, page = cache_k.shape[0], cache_k.shape[1], cache_k.shape[2]
            tview = lambda t: jnp.transpose(t, (0, 1, 3, 4, 2)).reshape(nl * npool, D, page)
            os_ = moba_sample_attend(qs, ks_, vs, tview(cache_k), tview(cache_v), page_table + j * npool)
            w_o = mb_w_o[j].astype(BF16)
            xp = oproj(op, None, w_o, xp, mp[:, 5], 1, 512)
            xs = oproj(os_, None, w_o, xs, ms[:, 5], 64, Ls)
            hd = (MB_HEADS, MB_HEADDIM)
            outs[10], outs[11] = kp.reshape(1, B, S, *hd), ks_.reshape(1, Bs, Ls, *hd)
            outs[12], outs[13] = vp.reshape(1, B, S, *hd), vs.reshape(1, Bs, Ls, *hd)
        else:
            w = (ml_w_in[j], ml_b_if[j], ml_norm_w[j], ml_w_o[j])
            xp, cp, np_, mp_ = mlstm_layer(xp, *m1p, jnp.zeros((B, ML_HEADS, ML_V_DIM, ML_QK_DIM), F32),
                                           jnp.zeros((B, ML_HEADS, ML_QK_DIM), F32), jnp.zeros((B, ML_HEADS), F32),
                                           *w, 1, 512, 128)
            xs, cs_, ns_, ms_ = mlstm_layer(xs, *m1s, state_mlstm_c[j], state_mlstm_n[j], state_mlstm_m[j],
                                            *w, 64, Ls, Ls)
            outs[14], outs[15], outs[16], outs[17], outs[18], outs[19] = cp[None], cs_[None], np_[None], ns_[None], mp_[None], ms_[None]
        xp = ffn(xp, mp[:, 6], mp[:, 7], mp[:, 8], norm_w[i, 2], wi1, wo1, 1, 512)
        xs = ffn(xs, ms[:, 6], ms[:, 7], ms[:, 8], norm_w[i, 2], wi1, wo1, 64, Ls)
    outs[0], outs[1] = xp, xs
    return tuple(outs[i] for i in range(20))
```

```python
import functools
import math

import jax
import jax.numpy as jnp
from jax import lax
from jax.experimental import pallas as pl
from jax.experimental.pallas import tpu as pltpu

D_MODEL = 1024
DEPTH = 4
PAST_LEN = 2048
N_MIXERS = 4
NORM_EPS = 1e-6
N_MOD = 9
D_FF = 2816

SSD_D_INNER = 2 * D_MODEL
SSD_HEADDIM = 64
SSD_HEADS = SSD_D_INNER // SSD_HEADDIM
SSD_GROUPS = 4
SSD_HPG = SSD_HEADS // SSD_GROUPS
SSD_STATE = 128
SSD_CONV = 4
SSD_CHUNK = 128
SSD_CONV_DIM = SSD_D_INNER + 2 * SSD_GROUPS * SSD_STATE

RW_HEADDIM = 64
RW_HEADS = D_MODEL // RW_HEADDIM
RW_LNX_EPS = 64e-5

MB_HEADS = 16
MB_HEADDIM = D_MODEL // MB_HEADS
MB_BLOCK = 256
MB_TOPK = 3
MB_QBLOCK = 16
ROPE_DIM = MB_HEADDIM // 4
ROPE_THETA = 500000.0

ML_HEADS = 8
ML_QK_DIM = D_MODEL // 2 // ML_HEADS
ML_V_DIM = D_MODEL // ML_HEADS
ML_CHUNK = 64

LANE = 128
VMEM_LIMIT = 48 * 1024 * 1024

BF16 = jnp.bfloat16
F32 = jnp.float32


def _cparams(sem):
    return pltpu.CompilerParams(dimension_semantics=sem, vmem_limit_bytes=VMEM_LIMIT)


def _mm_kernel(a_ref, w_ref, o_ref):
    o_ref[...] = jnp.dot(a_ref[...].astype(BF16), w_ref[...], preferred_element_type=F32)


def _pick(n, cands):
    for c in cands:
        if n % c == 0:
            return c
    raise ValueError(n)


def mm(a, w):
    lead = a.shape[:-1]
    K = a.shape[-1]
    N = w.shape[-1]
    a2 = a.reshape(-1, K)
    M = a2.shape[0]
    Mp = -(-M // 8) * 8
    if Mp != M:
        a2 = jnp.pad(a2, ((0, Mp - M), (0, 0)))
    Np = -(-N // LANE) * LANE
    wb = w.astype(BF16)
    if Np != N:
        wb = jnp.pad(wb, ((0, 0), (0, Np - N)))
    tm = _pick(Mp, (512, 256, 128, 64, 32, 16, 8))
    tn = _pick(Np, (1024, 512, 256, 128))
    out = pl.pallas_call(
        _mm_kernel, name="mm",
        grid=(Mp // tm, Np // tn),
        in_specs=[pl.BlockSpec((tm, K), lambda i, j: (i, 0)),
                  pl.BlockSpec((K, tn), lambda i, j: (0, j))],
        out_specs=pl.BlockSpec((tm, tn), lambda i, j: (i, j)),
        out_shape=jax.ShapeDtypeStruct((Mp, Np), F32),
        compiler_params=_cparams(("parallel", "arbitrary")),
    )(a2, wb)
    return out[:M, :N].reshape(*lead, N)


def _ffn_kernel(x_ref, sh_ref, sc_ref, gt_ref, nw_ref, wg_ref, wu_ref, wo_ref, o_ref, h_sc, acc_sc):
    k = pl.program_id(2)
    gb, rb, d = x_ref.shape

    @pl.when(k == 0)
    def _():
        x = x_ref[...]
        y = x * lax.rsqrt(jnp.mean(x * x, axis=-1, keepdims=True) + NORM_EPS) * nw_ref[...]
        h = y * (1.0 + sc_ref[...]) + sh_ref[...]
        h_sc[...] = h.reshape(gb * rb, d).astype(BF16)

    h = h_sc[...]
    g = jnp.dot(h, wg_ref[...], preferred_element_type=F32)
    u = jnp.dot(h, wu_ref[...], preferred_element_type=F32)
    a = (g * jax.nn.sigmoid(g) * u).astype(BF16)
    part = jnp.dot(a, wo_ref[...], preferred_element_type=F32)

    @pl.when(k == 0)
    def _():
        acc_sc[...] = part

    @pl.when(k > 0)
    def _():
        acc_sc[...] += part

    @pl.when(k == pl.num_programs(2) - 1)
    def _():
        o_ref[...] = x_ref[...] + 0.5 * gt_ref[...] * acc_sc[...].reshape(gb, rb, d)


def ffn(x, shift, scale, gate, nw, w_in_b, w_out_b, gb, rb):
    G, R, D = x.shape
    tf = D_FF // 2
    nk = D_FF // tf
    xspec = pl.BlockSpec((gb, rb, D), lambda g, r, k: (g, r, 0))
    mspec = pl.BlockSpec((gb, 1, D), lambda g, r, k: (g, 0, 0))
    return pl.pallas_call(
        _ffn_kernel, name="ffn",
        grid=(G // gb, R // rb, nk),
        in_specs=[xspec, mspec, mspec, mspec,
                  pl.BlockSpec((1, 1, D), lambda g, r, k: (0, 0, 0)),
                  pl.BlockSpec((D, tf), lambda g, r, k: (0, k)),
                  pl.BlockSpec((D, tf), lambda g, r, k: (0, k + nk)),
                  pl.BlockSpec((tf, D), lambda g, r, k: (k, 0))],
        out_specs=xspec,
        out_shape=jax.ShapeDtypeStruct(x.shape, F32),
        scratch_shapes=[pltpu.VMEM((gb * rb, D), BF16), pltpu.VMEM((gb * rb, D), F32)],
        compiler_params=_cparams(("parallel", "parallel", "arbitrary")),
    )(x, shift, scale, gate, nw.reshape(1, 1, D), w_in_b, w_in_b, w_out_b)


HI = lax.Precision.HIGHEST
NN = ((1,), (0,))
NT = ((1,), (1,))
TN = ((0,), (0,))
NEG = -1e30


def _dot(a, b, dims=NN, hi=False):
    if hi:
        return lax.dot_general(a, b, (dims, ((), ())), precision=HI, preferred_element_type=F32)
    return lax.dot_general(a.astype(BF16), b.astype(BF16), (dims, ((), ())), preferred_element_type=F32)


def _split(a):
    hi = a.astype(BF16)
    return hi, (a - hi.astype(F32)).astype(BF16)


def _dot3s(sa, sb, dims=NN):
    d = lambda x, y: lax.dot_general(x, y, (dims, ((), ())), preferred_element_type=F32)
    return d(sa[0], sb[0]) + d(sa[0], sb[1]) + d(sa[1], sb[0])


def _dot3(a, b, dims=NN):
    return _dot3s(_split(a), _split(b), dims)


def _modnorm_kernel(x_ref, sh_ref, sc_ref, nw_ref, o_ref):
    x = x_ref[...]
    y = x * lax.rsqrt(jnp.mean(x * x, axis=-1, keepdims=True) + NORM_EPS) * nw_ref[...]
    o_ref[...] = (y * (1.0 + sc_ref[...]) + sh_ref[...]).astype(o_ref.dtype)


def modnorm(x, shift, scale, nw, gb, rb, dtype):
    G, R, D = x.shape
    xspec = pl.BlockSpec((gb, rb, D), lambda g, r: (g, r, 0))
    mspec = pl.BlockSpec((gb, 1, D), lambda g, r: (g, 0, 0))
    return pl.pallas_call(
        _modnorm_kernel, name="modnorm",
        grid=(G // gb, R // rb),
        in_specs=[xspec, mspec, mspec, pl.BlockSpec((1, 1, D), lambda g, r: (0, 0, 0))],
        out_specs=xspec,
        out_shape=jax.ShapeDtypeStruct(x.shape, dtype),
        compiler_params=_cparams(("parallel", "parallel")),
    )(x, shift, scale, nw.reshape(1, 1, D))


def _oproj_kernel(*refs, has_mul):
    if has_mul:
        y_ref, mul_ref, w_ref, x_ref, gt_ref, o_ref = refs
    else:
        y_ref, w_ref, x_ref, gt_ref, o_ref = refs
    gb, rb, kd = y_ref.shape
    y = y_ref[...]
    if has_mul:
        y = y * mul_ref[...]
    acc = jnp.dot(y.reshape(gb * rb, kd).astype(BF16), w_ref[...], preferred_element_type=F32)
    o_ref[...] = x_ref[...] + gt_ref[...] * acc.reshape(gb, rb, acc.shape[-1])


def oproj(y, mul, w_b, x, gate, gb, rb):
    G, R, Kd = y.shape
    D = x.shape[-1]
    yspec = pl.BlockSpec((gb, rb, Kd), lambda g, r: (g, r, 0))
    xspec = pl.BlockSpec((gb, rb, D), lambda g, r: (g, r, 0))
    ins = [y] + ([mul] if mul is not None else []) + [w_b, x, gate]
    specs = [yspec] + ([yspec] if mul is not None else []) + [
        pl.BlockSpec((Kd, D), lambda g, r: (0, 0)), xspec,
        pl.BlockSpec((gb, 1, D), lambda g, r: (g, 0, 0))]
    return pl.pallas_call(
        functools.partial(_oproj_kernel, has_mul=mul is not None), name="oproj",
        grid=(G // gb, R // rb),
        in_specs=specs,
        out_specs=xspec,
        out_shape=jax.ShapeDtypeStruct(x.shape, F32),
        compiler_params=_cparams(("parallel", "parallel")),
    )(*ins)


def _qkpost_kernel(q_ref, k_ref, cos_ref, sin_ref, qw_ref, kw_ref, qo_ref, ko_ref):
    gb, rb, w = q_ref.shape
    n = gb * rb
    lane = lax.broadcasted_iota(jnp.int32, (n, w), 1)
    lo = lane < MB_HEADDIM
    first = (lane % MB_HEADDIM) < (ROPE_DIM // 2)
    cos = jnp.broadcast_to(cos_ref[...][None], (gb, rb, w)).reshape(n, w)
    sin = jnp.broadcast_to(sin_ref[...][None], (gb, rb, w)).reshape(n, w)

    def norm_rope(x, wt):
        x2 = x * x
        s_lo = jnp.sum(jnp.where(lo, x2, 0.0), axis=-1, keepdims=True)
        s_hi = jnp.sum(jnp.where(lo, 0.0, x2), axis=-1, keepdims=True)
        ms = jnp.where(lo, s_lo, s_hi) * (1.0 / MB_HEADDIM)
        y = x * lax.rsqrt(ms + NORM_EPS) * wt
        up = pltpu.roll(y, w - ROPE_DIM // 2, 1)
        dn = pltpu.roll(y, ROPE_DIM // 2, 1)
        return y * cos + jnp.where(first, up, dn) * sin

    qo_ref[...] = norm_rope(q_ref[...].reshape(n, w), qw_ref[...]).reshape(gb, rb, w)
    ko_ref[...] = norm_rope(k_ref[...].reshape(n, w), kw_ref[...]).reshape(gb, rb, w)


def _rope_tables(pos):
    half = ROPE_DIM // 2
    inv = ROPE_THETA ** (-jnp.arange(half, dtype=F32) / half)
    ang = pos.astype(F32)[:, None] * inv
    c, s = jnp.cos(ang), jnp.sin(ang)
    ones = jnp.ones((pos.shape[0], MB_HEADDIM - ROPE_DIM), F32)
    ch = jnp.concatenate([c, c, ones], axis=-1)
    sh = jnp.concatenate([-s, s, 0.0 * ones], axis=-1)
    return jnp.concatenate([ch, ch], axis=-1), jnp.concatenate([sh, sh], axis=-1)


def qkpost(qkv, pos, q_norm, k_norm, gb, rb):
    G, R, _ = qkv.shape
    D = D_MODEL
    npair = D // LANE
    cos, sin = _rope_tables(pos)
    tspec = pl.BlockSpec((rb, LANE), lambda g, r, p: (r, 0))
    wspec = pl.BlockSpec((1, LANE), lambda g, r, p: (0, 0))
    ospec = pl.BlockSpec((gb, rb, LANE), lambda g, r, p: (g, r, p))
    return pl.pallas_call(
        _qkpost_kernel, name="qkpost",
        grid=(G // gb, R // rb, npair),
        in_specs=[pl.BlockSpec((gb, rb, LANE), lambda g, r, p: (g, r, p)),
                  pl.BlockSpec((gb, rb, LANE), lambda g, r, p: (g, r, p + npair)),
                  tspec, tspec, wspec, wspec],
        out_specs=[ospec, ospec],
        out_shape=[jax.ShapeDtypeStruct((G, R, D), F32)] * 2,
        compiler_params=_cparams(("parallel", "parallel", "parallel")),
    )(qkv, qkv, cos, sin, jnp.tile(q_norm, 2).reshape(1, LANE), jnp.tile(k_norm, 2).reshape(1, LANE))


def _top3_mask0(g, idx, nb):
    sel = jnp.zeros(g.shape, jnp.bool_)
    for _ in range(MB_TOPK):
        mx = jnp.max(g, axis=0, keepdims=True)
        first = jnp.min(jnp.where(g == mx, idx, nb), axis=0, keepdims=True)
        pick = idx == first
        sel = jnp.logical_or(sel, pick)
        g = jnp.where(pick, -jnp.inf, g)
    return sel


def _moba_prompt_t_kernel(q_ref, k_ref, v_ref, o_ref, kmean_sc, kb_sc, vt_sc, selt_sc):
    c = pl.program_id(2)
    T = MB_BLOCK
    P = MB_HEADDIM
    nb = k_ref.shape[1] // T
    scale = MB_HEADDIM ** -0.5

    @pl.when(c == 0)
    def _():
        for n in range(nb):
            kn = k_ref[0, n * T:(n + 1) * T, :]
            kmean_sc[n:n + 1, :] = jnp.mean(kn, axis=0, keepdims=True)
            kb_sc[n] = kn.astype(BF16)
            vt_sc[n] = v_ref[0, n * T:(n + 1) * T, :].T.astype(BF16)

    q = q_ref[0]
    lo = lax.broadcasted_iota(jnp.int32, (T, LANE), 1) < P
    qh = (jnp.where(lo, q, 0.0), jnp.where(lo, 0.0, q))
    blk = lax.broadcasted_iota(jnp.int32, (nb, T), 0)
    for hh in range(2):
        g = _dot(kmean_sc[...], qh[hh], NT, hi=True)
        g = jnp.where(blk < c, g, -jnp.inf)
        selt_sc[hh] = jnp.logical_and(_top3_mask0(g, blk, nb), blk < c).astype(F32)
    qb = tuple((x * scale).astype(BF16) for x in qh)
    causal = lax.broadcasted_iota(jnp.int32, (T, T), 0) <= lax.broadcasted_iota(jnp.int32, (T, T), 1)
    kc, vc = kb_sc[c], vt_sc[c]
    init = []
    for hh in range(2):
        s = jnp.where(causal, _dot(kc, qb[hh], NT), NEG)
        m = jnp.max(s, axis=0, keepdims=True)
        p = jnp.exp(s - m)
        init += [m, jnp.sum(p, axis=0, keepdims=True), _dot(vc[hh * P:(hh + 1) * P, :], p)]

    def scores(i):
        return tuple(_dot(kb_sc[jnp.minimum(2 * i + d, nb - 1)], qb[hh], NT) for hh in range(2) for d in range(2))

    def body(i, carry):
        state, s_cur = carry
        s_next = scores(i + 1)
        ns = (2 * i, 2 * i + 1)
        out, stage = [], []
        for hh in range(2):
            m, l, acc = state[3 * hh:3 * hh + 3]
            ss = [jnp.where(selt_sc[hh, pl.ds(n, 1), :] > 0.5, s_cur[2 * hh + d], NEG) for d, n in enumerate(ns)]
            m_new = jnp.maximum(m, jnp.max(jnp.maximum(ss[0], ss[1]), axis=0, keepdims=True))
            alpha = jnp.exp(m - m_new)
            ps = [jnp.exp(s - m_new) for s in ss]
            stage.append((m_new, alpha * l + jnp.sum(ps[0] + ps[1], axis=0, keepdims=True), alpha * acc, ps))
        for hh, (m_new, l, acc, ps) in enumerate(stage):
            acc = acc + sum(_dot(vt_sc[n, hh * P:(hh + 1) * P, :], p) for n, p in zip(ns, ps))
            out += [m_new, l, acc]
        return tuple(out), s_next

    res, _ = lax.fori_loop(0, (c + 1) // 2, body, (tuple(init), scores(0)))
    o_ref[0] = jnp.concatenate([res[2] / res[1], res[5] / res[4]], axis=0).T


def moba_prompt_attend_t(q, k, v):
    B, L, D = q.shape
    nb = L // MB_BLOCK
    kvspec = pl.BlockSpec((1, L, LANE), lambda b, p, c: (b, 0, p))
    qspec = pl.BlockSpec((1, MB_BLOCK, LANE), lambda b, p, c: (b, c, p))
    return pl.pallas_call(
        _moba_prompt_t_kernel, name="moba_prompt",
        grid=(B, D // LANE, nb),
        in_specs=[qspec, kvspec, kvspec],
        out_specs=qspec,
        out_shape=jax.ShapeDtypeStruct((B, L, D), F32),
        scratch_shapes=[pltpu.VMEM((nb, LANE), F32), pltpu.VMEM((nb, MB_BLOCK, LANE), BF16),
                        pltpu.VMEM((nb, LANE, MB_BLOCK), BF16), pltpu.VMEM((2, nb, MB_BLOCK), F32)],
        compiler_params=_cparams(("parallel", "parallel", "arbitrary")),
    )(q, k, v)


def _moba_sample_kernel(pt_ref, q_ref, kn_ref, vn_ref, k0_ref, k1_ref, v0_ref, v1_ref, o_ref,
                        qbd_sc, qbdb_sc, kmean_sc, o_sc, m_sc, l_sc):
    del pt_ref
    n = pl.program_id(1)
    nblk = pl.num_programs(1)
    lq, d = q_ref.shape[1], q_ref.shape[2]
    rows = MB_HEADS * lq
    scale = MB_HEADDIM ** -0.5
    own = (lax.broadcasted_iota(jnp.int32, (MB_HEADS, lq, d), 0)
           == lax.broadcasted_iota(jnp.int32, (MB_HEADS, lq, d), 2) // MB_HEADDIM)

    @pl.when(n == 0)
    def _():
        qbd = jnp.where(own, q_ref[0][None], 0.0).reshape(rows, d)
        qbd_sc[...] = qbd
        qbdb_sc[...] = (qbd * scale).astype(BF16)

    qb = qbdb_sc[...]
    k0, k1 = _split(k0_ref[0]), _split(k1_ref[0])
    s0 = _dot(qb, k0[0])
    s1 = _dot(qb, k1[0])
    m = jnp.maximum(jnp.max(s0, axis=-1, keepdims=True), jnp.max(s1, axis=-1, keepdims=True))
    p0 = jnp.exp(s0 - m)
    p1 = jnp.exp(s1 - m)
    o_sc[n] = _dot(p0, v0_ref[0], NT) + _dot(p1, v1_ref[0], NT)
    m_sc[n] = m
    l_sc[n] = jnp.sum(p0, axis=-1, keepdims=True) + jnp.sum(p1, axis=-1, keepdims=True)
    ones = jnp.ones((8, k0_ref.shape[2]), BF16)
    d1 = lambda x: lax.dot_general(ones, x, (NT, ((), ())), preferred_element_type=F32)
    ksum = d1(k0[0]) + d1(k0[1]) + d1(k1[0]) + d1(k1[1])
    kmean_sc[n] = ksum[0:1] * (1.0 / MB_BLOCK)

    @pl.when(n == nblk - 1)
    def _():
        qbd = qbd_sc[...]
        nblocks = o_sc.shape[0]
        s = lax.dot_general(qbd * scale, kn_ref[0], (NT, ((), ())), preferred_element_type=F32)
        qi = lax.broadcasted_iota(jnp.int32, (rows, lq), 0) % lq
        s = jnp.where(lax.broadcasted_iota(jnp.int32, (rows, lq), 1) <= qi, s, NEG)
        m_own = jnp.max(s, axis=-1, keepdims=True)
        p = jnp.exp(s - m_own)
        l_own = jnp.sum(p, axis=-1, keepdims=True)
        o_own = jnp.dot(p, vn_ref[0], preferred_element_type=F32)
        g = [jnp.sum(qbd * kmean_sc[j], axis=-1, keepdims=True) for j in range(nblocks)]
        sel = []
        for j in range(nblocks):
            rank = jnp.zeros((rows, 1), jnp.int32)
            for i in range(nblocks):
                if i != j:
                    ahead = (g[i] >= g[j]) if i < j else (g[i] > g[j])
                    rank = rank + ahead.astype(jnp.int32)
            sel.append(rank < MB_TOPK)
        mx = m_own
        for j in range(nblocks):
            mx = jnp.where(sel[j], jnp.maximum(mx, m_sc[j]), mx)
        w_own = jnp.exp(m_own - mx)
        l = l_own * w_own
        o = o_own * w_own
        for j in range(nblocks):
            wj = jnp.where(sel[j], jnp.exp(m_sc[j] - mx), 0.0)
            l = l + wj * l_sc[j]
            o = o + wj * o_sc[j]
        res = (o / l).reshape(MB_HEADS, lq, d)
        o_ref[0] = jnp.sum(jnp.where(own, res, 0.0), axis=0)


def moba_sample_attend(q, k_new, v_new, cache_k, cache_v, page_table):
    Bs, Ls, D = q.shape
    page = cache_k.shape[2]
    assert MB_BLOCK == 2 * page and PAST_LEN % MB_BLOCK == 0 and Ls <= MB_BLOCK
    nblk = PAST_LEN // MB_BLOCK
    assert nblk >= MB_TOPK
    rows = MB_HEADS * Ls
    newspec = pl.BlockSpec((1, Ls, D), lambda b, n, pt: (b, 0, 0))
    pg0 = pl.BlockSpec((1, D, page), lambda b, n, pt: (pt[b, 2 * n], 0, 0))
    pg1 = pl.BlockSpec((1, D, page), lambda b, n, pt: (pt[b, 2 * n + 1], 0, 0))
    return pl.pallas_call(
        _moba_sample_kernel, name="moba_sample",
        grid_spec=pltpu.PrefetchScalarGridSpec(
            num_scalar_prefetch=1,
            grid=(Bs, nblk),
            in_specs=[newspec, newspec, newspec, pg0, pg1, pg0, pg1],
            out_specs=newspec,
            scratch_shapes=[pltpu.VMEM((rows, D), F32), pltpu.VMEM((rows, D), BF16),
                            pltpu.VMEM((nblk, 1, D), F32), pltpu.VMEM((nblk, rows, D), F32),
                            pltpu.VMEM((nblk, rows, 1), F32), pltpu.VMEM((nblk, rows, 1), F32)]),
        out_shape=jax.ShapeDtypeStruct((Bs, Ls, D), F32),
        compiler_params=_cparams(("parallel", "arbitrary")),
    )(page_table, q, k_new, v_new, cache_k, cache_k, cache_v, cache_v)


def _softplus(z):
    return jnp.maximum(z, 0.0) + jnp.log1p(jnp.exp(-jnp.abs(z)))


def _sigmoid(z):
    return 1.0 / (1.0 + jnp.exp(-z))


def _rwkv_pre_kernel(x_ref, halo_ref, sb_ref, sh_ref, sc_ref, nw_ref, mu_ref, w0_ref, a0_ref,
                     wr_ref, wk_ref, wv_ref, w1_ref, w2_ref, a1_ref, a2_ref, g1_ref, g2_ref,
                     r_o, lw_o, k_o, v_o, a_o, g_o, hl_o, hbuf):
    rr = pl.program_id(1)
    gb, rb, d = x_ref.shape
    n = gb * rb

    def hn(x):
        y = x * lax.rsqrt(jnp.mean(x * x, axis=-1, keepdims=True) + NORM_EPS) * nw_ref[...]
        return y * (1.0 + sc_ref[...]) + sh_ref[...]

    h = hn(x_ref[...])
    prev_row = jnp.where(rr == 0, sb_ref[...], hn(halo_ref[...])[:, 7:8, :])
    hbuf[:, 8:, :] = h
    hbuf[:, 7:8, :] = prev_row
    h2 = h.reshape(n, d)
    xx = hbuf[:, 7:7 + rb, :].reshape(n, d) - h2
    mix = lambda i: h2 + xx * mu_ref[i:i + 1, :]
    dot = lambda a, w_ref: jnp.dot(a.astype(BF16), w_ref[...], preferred_element_type=F32)
    out = lambda o_ref, val: o_ref.__setitem__(Ellipsis, val.reshape(gb, rb, d))
    out(r_o, dot(mix(0), wr_ref))
    out(k_o, dot(mix(2), wk_ref))
    out(v_o, dot(mix(3), wv_ref))
    w_raw = w0_ref[...] + dot(jnp.tanh(dot(mix(1), w1_ref)), w2_ref)
    out(lw_o, -jnp.exp(-_softplus(-w_raw) - 0.5))
    out(a_o, _sigmoid(a0_ref[...] + dot(dot(mix(4), a1_ref), a2_ref)))
    out(g_o, dot(_sigmoid(dot(mix(5), g1_ref)), g2_ref))
    hl_o[...] = h[:, rb - 1:rb, :]


def rwkv_pre(x, shift_buf, shift, scale, nw, mu, w0, a0, ws, gb, rb):
    G, R, D = x.shape
    assert gb == 1 or rb == R
    xspec = pl.BlockSpec((gb, rb, D), lambda g, r: (g, r, 0))
    halo = pl.BlockSpec((gb, 8, D), lambda g, r: (g, jnp.maximum(r * (rb // 8) - 1, 0), 0))
    mspec = pl.BlockSpec((gb, 1, D), lambda g, r: (g, 0, 0))
    vec = pl.BlockSpec((1, D), lambda g, r: (0, 0))
    full = lambda w: pl.BlockSpec(w.shape, lambda g, r: (0, 0))
    mu8 = jnp.pad(mu, ((0, 8 - mu.shape[0]), (0, 0)))
    big = jax.ShapeDtypeStruct((G, R, D), F32)
    return pl.pallas_call(
        _rwkv_pre_kernel, name="rwkv_pre",
        grid=(G // gb, R // rb),
        in_specs=[xspec, halo, mspec, mspec, mspec, vec, full(mu8), vec, vec] + [full(w) for w in ws],
        out_specs=[xspec] * 6 + [mspec],
        out_shape=[big] * 6 + [jax.ShapeDtypeStruct((G, 1, D), F32)],
        scratch_shapes=[pltpu.VMEM((gb, rb + 8, D), F32)],
        compiler_params=_cparams(("parallel", "arbitrary")),
    )(x, x, shift_buf.reshape(G, 1, D), shift, scale, nw.reshape(1, D), mu8, w0.reshape(1, D),
      a0.reshape(1, D), *ws)


def _rwkv_scan_kernel(r_ref, lw_ref, k_ref, v_ref, a_ref, kk_ref, ka_ref, rk_ref, lnw_ref, lnb_ref, s0_ref,
                      y_ref, so_ref, s_sc, *, hs):
    c = pl.program_id(2)
    C = r_ref.shape[1]
    N = RW_HEADDIM

    @pl.when(c == 0)
    def _():
        s_sc[...] = s0_ref[...]

    hg = s_sc.shape[1]
    rows = hs * C
    s_in = [[s_sc[bi, h] for h in range(hg)] for bi in range(s_sc.shape[0])]
    s_out = []

    def chain(bi, g):
        heads = range(g * hs, (g + 1) * hs)
        stack = lambda ref: jnp.concatenate([ref[bi][:, h * N:(h + 1) * N] for h in heads], axis=0)
        pstack = lambda ref: jnp.concatenate(
            [jnp.broadcast_to(ref[:, h * N:(h + 1) * N], (C, N)) for h in heads], axis=0)
        r, lw, kraw, v, a = (stack(ref) for ref in (r_ref, lw_ref, k_ref, v_ref, a_ref))
        kkr = kraw * pstack(kk_ref)
        kk = kkr / jnp.maximum(jnp.sqrt(jnp.sum(kkr * kkr, axis=-1, keepdims=True)), 1e-12)
        k = kraw * (1.0 + (a - 1.0) * pstack(ka_ref))
        tpos = lax.broadcasted_iota(jnp.int32, (rows, N), 0) % C
        cum, sh = lw, 1
        while sh < C:
            cum = cum + jnp.where(tpos >= sh, pltpu.roll(cum, sh, 0), 0.0)
            sh *= 2
        e_in, e_ex, e_neg = jnp.exp(cum), jnp.exp(cum - lw), jnp.exp(-cum)
        ab, bb, kb, rb_ = -kk * e_ex, kk * a * e_neg, k * e_neg, r * e_in
        ri = lax.broadcasted_iota(jnp.int32, (rows, rows), 0)
        ci = lax.broadcasted_iota(jnp.int32, (rows, rows), 1)
        same = (ri // C) == (ci // C)
        strict, incl = jnp.logical_and(same, ci < ri), jnp.logical_and(same, ci <= ri)
        sab, sbb, skb, srb = _split(ab), _split(bb), _split(kb), _split(rb_)
        d1 = lambda x, y, dims=NN: lax.dot_general(x, y, (dims, ((), ())), preferred_element_type=F32)
        lab = jnp.where(strict, d1(sab[0], sbb[0], NT), 0.0)
        lak = jnp.where(strict, d1(sab[0], skb[0], NT), 0.0)
        mrb = jnp.where(incl, d1(srb[0], sbb[0], NT), 0.0)
        mrk = jnp.where(incl, d1(srb[0], skb[0], NT), 0.0)
        tinv = (ri == ci).astype(F32) + lab
        spw = _split(lab)
        yield
        for _ in range(int(math.log2(C)) - 1):
            spw = _split(_dot3s(spw, spw))
            yield
            tinv = tinv + _dot3s(_split(tinv), spw)
            yield
        hsl = lambda t, h: t[h * C:(h + 1) * C]
        xs = [_dot3(jnp.concatenate([hsl(ab, j), hsl(rb_, j)], axis=0), s_in[bi][h], NT) for j, h in enumerate(heads)]
        ab_s = jnp.concatenate([x[:C] for x in xs], axis=0)
        rb_s = jnp.concatenate([x[C:] for x in xs], axis=0)
        yield
        u = _dot3(tinv, ab_s + _dot3(lak, v))
        yield
        y = rb_s + _dot3(jnp.concatenate([mrb, mrk], axis=1), jnp.concatenate([u, v], axis=0))
        yield
        for j, h in enumerate(heads):
            upd = _dot3(jnp.concatenate([hsl(u, j), hsl(v, j)], axis=0),
                        jnp.concatenate([hsl(bb, j), hsl(kb, j)], axis=0), TN)
            s_out.append((bi, h, (s_in[bi][h] + upd) * e_in[(j + 1) * C - 1:(j + 1) * C, :]))
        mean = jnp.mean(y, axis=-1, keepdims=True)
        var = jnp.mean(jnp.square(y - mean), axis=-1, keepdims=True)
        yn = (y - mean) * lax.rsqrt(var + RW_LNX_EPS) * pstack(lnw_ref) + pstack(lnb_ref)
        out = yn + jnp.sum(r * k * pstack(rk_ref), axis=-1, keepdims=True) * v
        y_ref[bi, :, g * hs * N:(g + 1) * hs * N] = jnp.concatenate([hsl(out, j) for j in range(hs)], axis=-1)

    chains = [chain(bi, g) for bi in range(s_sc.shape[0]) for g in range(hg // hs)]
    while chains:
        chains = [g for g in chains if next(g, True) is None]
    for bi, h, val in s_out:
        s_sc[bi, h] = val

    @pl.when(c == pl.num_programs(2) - 1)
    def _():
        so_ref[...] = s_sc[...]


def rwkv_scan(r, lw, k, v, a, k_k, k_a, r_k, lnx_w, lnx_b, s0, C, hg, nbk, hs):
    B, L, D = r.shape
    N = RW_HEADDIM
    assert L % C == 0 and C & (C - 1) == 0 and RW_HEADS % hg == 0 and B % nbk == 0
    assert hg % hs == 0 and (hs * N) % LANE == 0
    tspec = pl.BlockSpec((nbk, C, hg * N), lambda b, p, c: (b, c, p))
    pspec = pl.BlockSpec((1, hg * N), lambda b, p, c: (0, p))
    sspec = pl.BlockSpec((nbk, hg, N, N), lambda b, p, c: (b, p, 0, 0))
    vec = lambda t: t.reshape(1, D)
    return pl.pallas_call(
        functools.partial(_rwkv_scan_kernel, hs=hs), name="rwkv_scan",
        grid=(B // nbk, RW_HEADS // hg, L // C),
        in_specs=[tspec] * 5 + [pspec] * 5 + [sspec],
        out_specs=[tspec, sspec],
        out_shape=[jax.ShapeDtypeStruct((B, L, D), F32), jax.ShapeDtypeStruct(s0.shape, F32)],
        scratch_shapes=[pltpu.VMEM((nbk, hg, N, N), F32)],
        compiler_params=_cparams(("parallel", "parallel", "arbitrary")),
    )(r, lw, k, v, a, vec(k_k), vec(k_a), vec(r_k), vec(lnx_w), vec(lnx_b), s0)


def rwkv_layer(x, shift, scale, gate, nw, shift_buf, wkv0, mu, w_r, w_k, w_v, w0, w1, w2, a0, a1, a2, g1, g2,
               k_k, k_a, r_k, lnx_w, lnx_b, w_o, gb, rb, C, hg, nbk, hs):
    ws = [w.astype(BF16) for w in (w_r, w_k, w_v, w1, w2, a1, a2, g1, g2)]
    r, lw, k, v, a, g, hl = rwkv_pre(x, shift_buf, shift, scale, nw, mu, w0, a0, ws, gb, rb)
    y, s_fin = rwkv_scan(r, lw, k, v, a, k_k, k_a, r_k, lnx_w, lnx_b, wkv0, C, hg, nbk, hs)
    return oproj(y, g, w_o.astype(BF16), x, gate, gb, rb), hl[:, 0], s_fin


def _conv_kernel(x_ref, halo_ref, buf_ref, w_ref, b_ref, o_ref, sc):
    rr = pl.program_id(1)
    gb, rb, cw = x_ref.shape
    sc[:, :8, :] = jnp.where(rr == 0, buf_ref[...], halo_ref[...])
    sc[:, 8:, :] = x_ref[...]
    y = b_ref[...]
    for t in range(SSD_CONV):
        y = y + sc[:, 8 - (SSD_CONV - 1) + t:8 - (SSD_CONV - 1) + t + rb, :] * w_ref[t:t + 1, :]
    o_ref[...] = y * _sigmoid(y)


def ssd_conv(zx, buf, w, b, gb, rb, cw=512):
    G, R, _ = zx.shape
    assert gb == 1 or rb == R
    off = SSD_D_INNER // cw
    buf8 = jnp.pad(buf, ((0, 0), (8 - (SSD_CONV - 1), 0), (0, 0)))
    w8 = jnp.pad(w, ((0, 8 - SSD_CONV), (0, 0)))
    return pl.pallas_call(
        _conv_kernel, name="ssd_conv",
        grid=(G // gb, R // rb, SSD_CONV_DIM // cw),
        in_specs=[pl.BlockSpec((gb, rb, cw), lambda g, r, j: (g, r, off + j)),
                  pl.BlockSpec((gb, 8, cw), lambda g, r, j: (g, jnp.maximum(r * (rb // 8) - 1, 0), off + j)),
                  pl.BlockSpec((gb, 8, cw), lambda g, r, j: (g, 0, j)),
                  pl.BlockSpec((8, cw), lambda g, r, j: (0, j)),
                  pl.BlockSpec((1, cw), lambda g, r, j: (0, j))],
        out_specs=pl.BlockSpec((gb, rb, cw), lambda g, r, j: (g, r, j)),
        out_shape=jax.ShapeDtypeStruct((G, R, SSD_CONV_DIM), F32),
        scratch_shapes=[pltpu.VMEM((gb, rb + 8, cw), F32)],
        compiler_params=_cparams(("parallel", "arbitrary", "arbitrary")),
    )(zx, zx, buf8, w8, b.reshape(1, SSD_CONV_DIM))


def _col2row(col, eye):
    return jnp.sum(jnp.where(eye, col, 0.0), axis=0, keepdims=True)


def _ssd_scan_kernel(x_ref, bm_ref, cm_ref, z_ref, dt_ref, dtb_ref, alog_ref, dsk_ref, nw_ref, s0_ref,
                     y_ref, so_ref, s_sc):
    c = pl.program_id(2)
    Q = x_ref.shape[1]
    P, N, R = SSD_HEADDIM, SSD_STATE, SSD_HPG

    @pl.when(c == 0)
    def _():
        s_sc[...] = s0_ref[0]

    ti = lax.broadcasted_iota(jnp.int32, (Q, Q), 0)
    si = lax.broadcasted_iota(jnp.int32, (Q, Q), 1)
    causal, eye = si <= ti, si == ti
    lo = lax.broadcasted_iota(jnp.int32, (Q, LANE), 1) < P
    dt = _softplus(dt_ref[0] + dtb_ref[...])
    cs = _dot(causal.astype(F32), dt * (-jnp.exp(alog_ref[...])), hi=True)
    x, bm, cm = x_ref[0], bm_ref[0], cm_ref[0]
    cb = _dot(cm, bm, NT)
    s_all = s_sc[...].reshape(R * P, N)
    ys = _dot(cm, s_all, NT)
    y_pairs, xw_pairs, e_last = [], [], []
    for j in range(R // 2):
        xp = x[:, j * LANE:(j + 1) * LANE]
        yh, grow, wend = [], [], []
        for r in (2 * j, 2 * j + 1):
            cs_col, dt_col = cs[:, r:r + 1], dt[:, r:r + 1]
            cs_last = cs[Q - 1:Q, r:r + 1]
            dec = jnp.exp(jnp.where(causal, cs_col - _col2row(cs_col, eye), -jnp.inf))
            yh.append(_dot(cb * dec * _col2row(dt_col, eye), xp))
            grow.append(jnp.exp(cs_col))
            wend.append(jnp.exp(cs_last - cs_col) * dt_col)
            e_last.append(jnp.exp(cs_last))
        yp = jnp.where(lo, yh[0], yh[1]) + ys[:, j * LANE:(j + 1) * LANE] * jnp.where(lo, grow[0], grow[1])
        y_pairs.append(yp + xp * dsk_ref[:, j * LANE:(j + 1) * LANE])
        xw_pairs.append(xp * jnp.where(lo, wend[0], wend[1]))
    upd = _dot(jnp.concatenate(xw_pairs, axis=-1), bm, TN)
    for r in range(R):
        s_sc[r] = s_sc[r] * e_last[r] + upd[r * P:(r + 1) * P, :]
    z = z_ref[0]
    yz = jnp.concatenate(y_pairs, axis=-1) * (z * _sigmoid(z))
    y_ref[0] = yz * lax.rsqrt(jnp.mean(yz * yz, axis=-1, keepdims=True) + NORM_EPS) * nw_ref[...]

    @pl.when(c == pl.num_programs(2) - 1)
    def _():
        so_ref[0] = s_sc[...]


def ssd_chunk_scan(xbc, zx, dt_raw, dt_bias, a_log, d_skip, norm_w, s0, Q):
    B, L, _ = xbc.shape
    G, R, P, N = SSD_GROUPS, SSD_HPG, SSD_HEADDIM, SSD_STATE
    gw = R * P
    assert L % Q == 0 and N == LANE
    pad_heads = lambda t: jnp.pad(t.reshape(G, R), ((0, 0), (0, LANE - R))).reshape(1, G * LANE)
    xs = pl.BlockSpec((1, Q, gw), lambda b, g, c: (b, c, g))
    bs = pl.BlockSpec((1, Q, N), lambda b, g, c: (b, c, SSD_D_INNER // N + g))
    cs = pl.BlockSpec((1, Q, N), lambda b, g, c: (b, c, SSD_D_INNER // N + G + g))
    ds = pl.BlockSpec((1, Q, LANE), lambda b, g, c: (b, c, g))
    hs = pl.BlockSpec((1, LANE), lambda b, g, c: (0, g))
    ws = pl.BlockSpec((1, gw), lambda b, g, c: (0, g))
    ss = pl.BlockSpec((1, R, P, N), lambda b, g, c: (b, g, 0, 0))
    return pl.pallas_call(
        _ssd_scan_kernel, name="ssd_scan",
        grid=(B, G, L // Q),
        in_specs=[xs, bs, cs, xs, ds, hs, hs, ws, ws, ss],
        out_specs=[xs, ss],
        out_shape=[jax.ShapeDtypeStruct((B, L, SSD_D_INNER), F32), jax.ShapeDtypeStruct(s0.shape, F32)],
        scratch_shapes=[pltpu.VMEM((R, P, N), F32)],
        compiler_params=_cparams(("parallel", "parallel", "arbitrary")),
    )(xbc, xbc, xbc, zx, dt_raw, pad_heads(dt_bias), pad_heads(a_log),
      jnp.repeat(d_skip, P).reshape(1, SSD_D_INNER), norm_w.reshape(1, SSD_D_INNER), s0)


def ssd_layer(x, shift, scale, gate, nw, conv_buf, ssm0, w_in, conv_w, conv_b, dt_bias, a_log, d_skip, norm_w,
              w_out, gb, rb, Q):
    G, R, D = x.shape
    nzx = SSD_D_INNER + SSD_CONV_DIM
    h = modnorm(x, shift, scale, nw, gb, rb, BF16)
    zx = mm(h, w_in[:, :nzx])
    w_dt = jnp.pad(w_in[:, nzx:].reshape(D, SSD_GROUPS, SSD_HPG), ((0, 0), (0, 0), (0, LANE - SSD_HPG)))
    dt_raw = mm(h, w_dt.reshape(D, SSD_GROUPS * LANE))
    xbc = ssd_conv(zx, conv_buf, conv_w, conv_b, gb, rb)
    y, s_fin = ssd_chunk_scan(xbc, zx, dt_raw, dt_bias, a_log, d_skip, norm_w, ssm0, Q)
    new_conv = zx[:, R - (SSD_CONV - 1):, SSD_D_INNER:]
    return oproj(y, None, w_out.astype(BF16), x, gate, gb, rb), new_conv, s_fin


def _mlstm_kernel(q_ref, k_ref, v_ref, o_ref, g_ref, bif_ref, nw_ref, c0_ref, n0_ref, m0_ref,
                  y_ref, co_ref, no_ref, mo_ref, c_sc, n_sc, m_sc):
    c = pl.program_id(2)
    Q = q_ref.shape[1]
    DK, DV = ML_QK_DIM, ML_V_DIM

    @pl.when(c == 0)
    def _():
        c_sc[...] = c0_ref[0]
        n_sc[...] = n0_ref[0, 0]
        m_sc[...] = m0_ref[0, 0]

    ti = lax.broadcasted_iota(jnp.int32, (Q, Q), 0)
    si = lax.broadcasted_iota(jnp.int32, (Q, Q), 1)
    causal, eye = si <= ti, si == ti
    gates = g_ref[0] + bif_ref[...]
    bcum = _dot(causal.astype(F32), -_softplus(-gates), hi=True)
    ys = []
    for hh in range(2):
        i_col, b_col = gates[:, hh:hh + 1], bcum[:, 2 + hh:3 + hh]
        m_prev = m_sc[:, hh:hh + 1]
        q = q_ref[0][:, hh * DK:(hh + 1) * DK] * DK ** -0.5
        k = k_ref[0][:, hh * DK:(hh + 1) * DK]
        v = v_ref[0][:, hh * DV:(hh + 1) * DV]
        cst, nst = c_sc[hh], n_sc[hh:hh + 1, :]
        dmat = jnp.where(causal, b_col - _col2row(b_col, eye) + _col2row(i_col, eye), -jnp.inf)
        m_inter = b_col + m_prev
        m_t = jnp.maximum(m_inter, jnp.max(dmat, axis=-1, keepdims=True))
        w_inter = jnp.exp(m_inter - m_t)
        qk = _dot(q, k, NT) * jnp.exp(dmat - m_t)
        num = _dot(qk, v) + w_inter * _dot(q, cst, NT)
        den = jnp.sum(qk, axis=-1, keepdims=True) + w_inter * jnp.sum(q * nst, axis=-1, keepdims=True)
        hout = num / jnp.maximum(jnp.abs(den), jnp.exp(-m_t))
        m_last, b_last = m_t[Q - 1:Q], b_col[Q - 1:Q]
        w_state = jnp.exp(b_last - b_col + i_col - m_last)
        decay = jnp.exp(b_last + m_prev - m_last)
        c_sc[hh] = decay * cst + _dot(v * w_state, k, TN)
        n_sc[hh:hh + 1, :] = decay * nst + jnp.sum(w_state * k, axis=0, keepdims=True)
        m_sc[:, hh:hh + 1] = m_last
        hn = hout * lax.rsqrt(jnp.mean(hout * hout, axis=-1, keepdims=True) + NORM_EPS)
        ys.append(_sigmoid(o_ref[0][:, hh * DV:(hh + 1) * DV]) * (hn * nw_ref[:, hh * DV:(hh + 1) * DV]))
    y_ref[0] = jnp.concatenate(ys, axis=-1)

    @pl.when(c == pl.num_programs(2) - 1)
    def _():
        co_ref[0] = c_sc[...]
        no_ref[0, 0] = n_sc[...]
        mo_ref[0, 0] = m_sc[...]


def mlstm_scan(proj, gates, b_if, norm_w, c0, n0, m0, Q):
    B, L, _ = proj.shape
    H, DK, DV = ML_HEADS, ML_QK_DIM, ML_V_DIM
    assert L % Q == 0 and 2 * DK == LANE
    npair = H // 2
    qs = pl.BlockSpec((1, Q, LANE), lambda b, j, c: (b, c, j))
    ks = pl.BlockSpec((1, Q, LANE), lambda b, j, c: (b, c, npair + j))
    vs = pl.BlockSpec((1, Q, 2 * DV), lambda b, j, c: (b, c, H * DK // DV + j))
    os_ = pl.BlockSpec((1, Q, 2 * DV), lambda b, j, c: (b, c, H * DK // DV + npair + j))
    ys = pl.BlockSpec((1, Q, 2 * DV), lambda b, j, c: (b, c, j))
    cs = pl.BlockSpec((1, 2, DV, DK), lambda b, j, c: (b, j, 0, 0))
    ns = pl.BlockSpec((1, 1, 2, DK), lambda b, j, c: (b, j, 0, 0))
    ms = pl.BlockSpec((1, 1, 1, 2), lambda b, j, c: (b, j, 0, 0))
    bi, bf = b_if[:H].reshape(npair, 2), b_if[H:].reshape(npair, 2)
    bias = jnp.pad(jnp.concatenate([bi, bf], axis=-1), ((0, 0), (0, LANE - 4))).reshape(1, npair * LANE)
    y, c, n, m = pl.pallas_call(
        _mlstm_kernel, name="mlstm",
        grid=(B, npair, L // Q),
        in_specs=[qs, ks, vs, os_, qs, pl.BlockSpec((1, LANE), lambda b, j, c: (0, j)),
                  pl.BlockSpec((1, 2 * DV), lambda b, j, c: (0, j)), cs, ns, ms],
        out_specs=[ys, cs, ns, ms],
        out_shape=[jax.ShapeDtypeStruct((B, L, H * DV), F32), jax.ShapeDtypeStruct((B, H, DV, DK), F32),
                   jax.ShapeDtypeStruct((B, npair, 2, DK), F32), jax.ShapeDtypeStruct((B, npair, 1, 2), F32)],
        scratch_shapes=[pltpu.VMEM((2, DV, DK), F32), pltpu.VMEM((2, DK), F32), pltpu.VMEM((1, 2), F32)],
        compiler_params=_cparams(("parallel", "parallel", "arbitrary")),
    )(proj, proj, proj, proj, gates, bias, norm_w.reshape(1, H * DV), c0,
      n0.reshape(B, npair, 2, DK), m0.reshape(B, npair, 1, 2))
    return y, c, n.reshape(B, H, DK), m.reshape(B, H)


def mlstm_layer(x, shift, scale, gate, nw, c0, n0, m0, w_in, b_if, norm_w, w_o, gb, rb, Q):
    D = x.shape[-1]
    H = ML_HEADS
    nmain = 2 * H * ML_QK_DIM + 2 * H * ML_V_DIM
    h = modnorm(x, shift, scale, nw, gb, rb, BF16)
    proj = mm(h, w_in[:, :nmain])
    wi, wf = w_in[:, nmain:nmain + H].reshape(D, H // 2, 2), w_in[:, nmain + H:].reshape(D, H // 2, 2)
    wg = jnp.pad(jnp.concatenate([wi, wf], axis=-1), ((0, 0), (0, 0), (0, LANE - 4)))
    gates = mm(h, wg.reshape(D, (H // 2) * LANE))
    y, c, n, m = mlstm_scan(proj, gates, b_if, norm_w, c0, n0, m0, Q)
    return oproj(y, None, w_o.astype(BF16), x, gate, gb, rb), c, n, m


def kernel(x_prompt, x_sample, state_ssd_conv, state_ssd, state_rwkv_shift, state_rwkv, cache_k, cache_v, state_mlstm_c, state_mlstm_n, state_mlstm_m, page_table, c_prompt, c_sample, ada_w, ada_b, norm_w, ffn_w_in, ffn_w_out, ssd_w_in, ssd_conv_w, ssd_conv_b, ssd_dt_bias, ssd_a_log, ssd_d, ssd_norm_w, ssd_w_out, rw_mu, rw_w_r, rw_w_k, rw_w_v, rw_w0, rw_w1, rw_w2, rw_a0, rw_a1, rw_a2, rw_g1, rw_g2, rw_k_k, rw_k_a, rw_r_k, rw_lnx_w, rw_lnx_b, rw_w_o, mb_w_qkv, mb_q_norm, mb_k_norm, mb_w_o, ml_w_in, ml_b_if, ml_norm_w, ml_w_o):
    B, S = x_prompt.shape[0], x_prompt.shape[1]
    Bs, Ls = x_sample.shape[0], x_sample.shape[1]
    D = D_MODEL
    pos_p = jnp.arange(S, dtype=jnp.int32)
    pos_s = PAST_LEN + jnp.arange(Ls, dtype=jnp.int32)
    xp, xs = x_prompt, x_sample
    outs = {}
    c_all = jnp.concatenate([c_prompt, c_sample], axis=0)
    sc_all = jax.nn.silu(c_all)
    for i in range(DEPTH):
        kind, j = i % N_MIXERS, i // N_MIXERS
        mod = (mm(sc_all, ada_w[i]) + ada_b[i]).reshape(B + Bs, N_MOD, 1, D)
        mp, ms = mod[:B], mod[B:]
        wi0, wo0 = ffn_w_in[i, 0].astype(BF16), ffn_w_out[i, 0].astype(BF16)
        wi1, wo1 = ffn_w_in[i, 1].astype(BF16), ffn_w_out[i, 1].astype(BF16)
        xp = ffn(xp, mp[:, 0], mp[:, 1], mp[:, 2], norm_w[i, 0], wi0, wo0, 1, 512)
        xs = ffn(xs, ms[:, 0], ms[:, 1], ms[:, 2], norm_w[i, 0], wi0, wo0, 64, Ls)
        m1p, m1s = (mp[:, 3], mp[:, 4], mp[:, 5], norm_w[i, 1]), (ms[:, 3], ms[:, 4], ms[:, 5], norm_w[i, 1])
        if kind == 0:
            w = (ssd_w_in[j], ssd_conv_w[j], ssd_conv_b[j], ssd_dt_bias[j], ssd_a_log[j], ssd_d[j], ssd_norm_w[j], ssd_w_out[j])
            xp, cp, sp = ssd_layer(xp, *m1p, jnp.zeros((B, SSD_CONV - 1, SSD_CONV_DIM), F32),
                                   jnp.zeros((B, SSD_HEADS, SSD_HEADDIM, SSD_STATE), F32), *w, 1, 512, SSD_CHUNK)
            xs, cs_, ss = ssd_layer(xs, *m1s, state_ssd_conv[j], state_ssd[j], *w, 64, Ls, Ls)
            outs[2], outs[3], outs[4], outs[5] = cp[None], cs_[None], sp[None], ss[None]
        elif kind == 1:
            w = (rw_mu[j], rw_w_r[j], rw_w_k[j], rw_w_v[j], rw_w0[j], rw_w1[j], rw_w2[j], rw_a0[j], rw_a1[j],
                 rw_a2[j], rw_g1[j], rw_g2[j], rw_k_k[j], rw_k_a[j], rw_r_k[j], rw_lnx_w[j], rw_lnx_b[j], rw_w_o[j])
            xp, shp, sp = rwkv_layer(xp, *m1p, jnp.zeros((B, D), F32),
                                     jnp.zeros((B, RW_HEADS, RW_HEADDIM, RW_HEADDIM), F32), *w, 1, 256, 64, 8, B, 2)
            xs, shs, ss = rwkv_layer(xs, *m1s, state_rwkv_shift[j], state_rwkv[j], *w, 32, Ls, Ls, RW_HEADS, 2,
                                     RW_HEADS)
            outs[6], outs[7], outs[8], outs[9] = shp[None], shs[None], sp[None], ss[None]
        elif kind == 2:
            hp = modnorm(xp, mp[:, 3], mp[:, 4], norm_w[i, 1], 1, 512, BF16)
            hs = modnorm(xs, ms[:, 3], ms[:, 4], norm_w[i, 1], 64, Ls, BF16)
            qkv_p, qkv_s = mm(hp, mb_w_qkv[j]), mm(hs, mb_w_qkv[j])
            qp, kp = qkpost(qkv_p, pos_p, mb_q_norm[j], mb_k_norm[j], 1, 512)
            qs, ks_ = qkpost(qkv_s, pos_s, mb_q_norm[j], mb_k_norm[j], 16, Ls)
            vp, vs = qkv_p[..., 2 * D:], qkv_s[..., 2 * D:]
            op = moba_prompt_attend_t(qp, kp, vp)
            nl, npool, page = cache_k.shape[0], cache_k.shape[1], cache_k.shape[2]
            tview = lambda t: jnp.transpose(t, (0, 1, 3, 4, 2)).reshape(nl * npool, D, page)
            os_ = moba_sample_attend(qs, ks_, vs, tview(cache_k), tview(cache_v), page_table + j * npool)
            w_o = mb_w_o[j].astype(BF16)
            xp = oproj(op, None, w_o, xp, mp[:, 5], 1, 512)
            xs = oproj(os_, None, w_o, xs, ms[:, 5], 64, Ls)
            hd = (MB_HEADS, MB_HEADDIM)
            outs[10], outs[11] = kp.reshape(1, B, S, *hd), ks_.reshape(1, Bs, Ls, *hd)
            outs[12], outs[13] = vp.reshape(1, B, S, *hd), vs.reshape(1, Bs, Ls, *hd)
        else:
            w = (ml_w_in[j], ml_b_if[j], ml_norm_w[j], ml_w_o[j])
            xp, cp, np_, mp_ = mlstm_layer(xp, *m1p, jnp.zeros((B, ML_HEADS, ML_V_DIM, ML_QK_DIM), F32),
                                           jnp.zeros((B, ML_HEADS, ML_QK_DIM), F32), jnp.zeros((B, ML_HEADS), F32),
                                           *w, 1, 512, 128)
            xs, cs_, ns_, ms_ = mlstm_layer(xs, *m1s, state_mlstm_c[j], state_mlstm_n[j], state_mlstm_m[j],
                                            *w, 64, Ls, Ls)
            outs[14], outs[15], outs[16], outs[17], outs[18], outs[19] = cp[None], cs_[None], np_[None], ns_[None], mp_[None], ms_[None]
        xp = ffn(xp, mp[:, 6], mp[:, 7], mp[:, 8], norm_w[i, 2], wi1, wo1, 1, 512)
        xs = ffn(xs, ms[:, 6], ms[:, 7], ms[:, 8], norm_w[i, 2], wi1, wo1, 64, Ls)
    outs[0], outs[1] = xp, xs
    return tuple(outs[i] for i in range(20))
```
